```python
import math
import jax, jax.numpy as jnp
from jax import lax
import numpy as np

D_MODEL = 4096
BATCH = 8
SEQ = 2048
DEPTH = 4

BRANCH_W = 1024
N_BRANCH = 3
A_HEADS = 8
A_DK = 128
A_DV = 128
A_WIDTH = A_HEADS * A_DK
A_CHUNK = 64
B_HEADS = 8
B_DH = 128
B_KV_RANK = 512
IDX_HEADS = 16
IDX_DH = 64
TOPK_MAX = 256
B_QBLOCK = 128
C_HEADS = 16
C_KV_HEADS = 2
C_DH = 64
WINDOW = 128
REL_BUCKETS = 32
REL_MAX_DIST = 128
D_FF = 4096
FFN_RES = 0.5
COND_RANK = 256
N_MOD = 9
RMS_EPS = 1e-6
NEG_INF = -1e30

IN_SPLITS = (
    A_WIDTH, A_WIDTH, A_HEADS * A_DV, A_HEADS * A_DV,
    B_HEADS * B_DH, B_KV_RANK, IDX_HEADS * IDX_DH, IDX_DH, IDX_HEADS,
    C_HEADS * C_DH, C_KV_HEADS * C_DH, C_KV_HEADS * C_DH,
    N_BRANCH * D_MODEL,
)
D_IN = sum(IN_SPLITS)

kernel_name = "hybrid_hgrn2_dsa_swa_macaron_block"


def rms_norm(x, g):
    x32 = x.astype(jnp.float32)
    y = x32 * lax.rsqrt(jnp.mean(x32 * x32, axis=-1, keepdims=True) + RMS_EPS)
    return (y * g.astype(jnp.float32)).astype(x.dtype)


def t5_bucket(dist):
    max_exact = REL_BUCKETS // 2
    d = jnp.maximum(dist, 0)
    df = jnp.maximum(d, 1).astype(jnp.float32)
    large = max_exact + (jnp.log(df / max_exact) / math.log(REL_MAX_DIST / max_exact)
                         * (REL_BUCKETS - max_exact)).astype(jnp.int32)
    large = jnp.minimum(large, REL_BUCKETS - 1)
    return jnp.where(d < max_exact, d, large)


def swiglu(h, w_in, w_out):
    u, v = jnp.split(h @ w_in, 2, axis=-1)
    return (jax.nn.silu(u) * v) @ w_out


def hgrn2_mixer(q, f_raw, i_raw, g, lb, norm_g):
    bsz, s, _ = q.shape
    nc = s // A_CHUNK
    f = lb + (1.0 - lb) * jax.nn.sigmoid(f_raw.astype(jnp.float32))
    log_f = jnp.log(jnp.maximum(f, 1e-20))
    k = 1.0 - f
    v = jax.nn.silu(i_raw.astype(jnp.float32))

    def to_chunks(t, d):
        return t.astype(jnp.float32).reshape(bsz, nc, A_CHUNK, A_HEADS, d).transpose(1, 0, 3, 2, 4)

    qc, kc, lfc = to_chunks(q, A_DK), to_chunks(k, A_DK), to_chunks(log_f, A_DK)
    vc = to_chunks(v, A_DV)
    causal = jnp.tril(jnp.ones((A_CHUNK, A_CHUNK), dtype=bool))

    def step(state, inp):
        qj, kj, vj, lfj = inp
        b = jnp.cumsum(lfj, axis=2)
        diff = b[:, :, :, None, :] - b[:, :, None, :, :]
        decay = jnp.where(causal[:, :, None], jnp.exp(jnp.minimum(diff, 0.0)), 0.0)
        attn = jnp.einsum('bhtd,bhsd,bhtsd->bhts', qj, kj, decay)
        o = jnp.einsum('bhts,bhsv->bhtv', attn, vj)
        o = o + jnp.einsum('bhtd,bhdv->bhtv', qj * jnp.exp(b), state)
        b_end = b[:, :, -1:, :]
        new_state = jnp.exp(b_end[:, :, 0, :])[..., None] * state + \
            jnp.einsum('bhsd,bhsv->bhdv', kj * jnp.exp(b_end - b), vj)
        return new_state, o

    s0 = jnp.zeros((bsz, A_HEADS, A_DK, A_DV), jnp.float32)
    _, o = lax.scan(step, s0, (qc, kc, vc, lfc))
    o = o.transpose(1, 0, 3, 2, 4).reshape(bsz, s, A_HEADS, A_DV)
    o = rms_norm(o, norm_g).reshape(bsz, s, A_HEADS * A_DV).astype(g.dtype)
    return o * jax.nn.silu(g)


def dsa_mixer(q, latent, iq, ik, iw, kv_norm_g, w_kv_up, rel_table_b):
    bsz, s, _ = q.shape
    kv = rms_norm(latent, kv_norm_g) @ w_kv_up
    k, v = jnp.split(kv, 2, axis=-1)
    nblk = s // B_QBLOCK
    topk = min(TOPK_MAX, s // 4)
    qb = q.reshape(bsz, nblk, B_QBLOCK, B_HEADS, B_DH).transpose(1, 0, 2, 3, 4)
    iqb = iq.reshape(bsz, nblk, B_QBLOCK, IDX_HEADS, IDX_DH).transpose(1, 0, 2, 3, 4)
    iwb = (iw * (IDX_HEADS ** -0.5 * IDX_DH ** -0.5)).reshape(
        bsz, nblk, B_QBLOCK, IDX_HEADS).transpose(1, 0, 2, 3)
    s_pos = jnp.arange(s)
    gather = jax.vmap(lambda table, idx: table[idx])

    def block(args):
        qj, iqj, iwj, j = args
        t_pos = j * B_QBLOCK + jnp.arange(B_QBLOCK)
        score = jax.nn.relu(jnp.einsum('bthd,bsd->bths', iqj, ik))
        score = jnp.einsum('bths,bth->bts', score, iwj).astype(jnp.float32)
        visible = s_pos[None, :] <= t_pos[:, None]
        score = jnp.where(visible[None], score, NEG_INF)
        _, idx = lax.top_k(score, topk)
        k_sel = gather(k, idx)
        v_sel = gather(v, idx)
        dist = t_pos[None, :, None] - idx
        bias = rel_table_b[t5_bucket(dist)].astype(jnp.float32)
        logits = jnp.einsum('bthd,btkd->bthk', qj, k_sel).astype(jnp.float32) * (B_DH ** -0.5) \
            + bias.transpose(0, 1, 3, 2)
        logits = jnp.where((dist >= 0)[:, :, None, :], logits, NEG_INF)
        p = jax.nn.softmax(logits, axis=-1).astype(v.dtype)
        return jnp.einsum('bthk,btkd->bthd', p, v_sel)

    o = lax.map(block, (qb, iqb, iwb, jnp.arange(nblk)))
    return o.transpose(1, 0, 2, 3, 4).reshape(bsz, s, B_HEADS * B_DH)


def swa_mixer(q, k, v, sinks, rel_table_c):
    bsz, s, _ = q.shape
    nb = s // WINDOW
    grp = C_HEADS // C_KV_HEADS
    qb = q.reshape(bsz, nb, WINDOW, C_KV_HEADS, grp, C_DH)

    def band(t):
        t = t.reshape(bsz, nb, WINDOW, C_KV_HEADS, C_DH)
        prev = jnp.pad(t, ((0, 0), (1, 0), (0, 0), (0, 0), (0, 0)))[:, :-1]
        return jnp.concatenate([prev, t], axis=2)

    kb, vb = band(k), band(v)
    r = jnp.arange(WINDOW)
    u = jnp.arange(2 * WINDOW)
    dist = r[:, None] + WINDOW - u[None, :]
    blk = jnp.arange(nb)
    key_ok = (blk[:, None] * WINDOW - WINDOW + u[None, :]) >= 0
    valid = ((dist >= 0) & (dist < WINDOW))[None] & key_ok[:, None, :]
    bias = rel_table_c[t5_bucket(dist)].astype(jnp.float32)
    bias = bias.reshape(WINDOW, 2 * WINDOW, C_KV_HEADS, grp).transpose(2, 3, 0, 1)
    logits = jnp.einsum('bnqhgd,bnkhd->bnhgqk', qb, kb).astype(jnp.float32) * (C_DH ** -0.5) + bias
    logits = jnp.where(valid[None, :, None, None], logits, NEG_INF)
    sink = jnp.broadcast_to(
        sinks.astype(jnp.float32).reshape(C_KV_HEADS, grp)[None, None, :, :, None, None],
        logits.shape[:-1] + (1,))
    p = jax.nn.softmax(jnp.concatenate([logits, sink], axis=-1), axis=-1)[..., :-1]
    o = jnp.einsum('bnhgqk,bnkhd->bnqhgd', p.astype(v.dtype), vb)
    return o.reshape(bsz, s, C_HEADS * C_DH)


def setup_inputs(seed: int = 0) -> dict:
    key = jax.random.key(seed)
    ks = jax.random.split(key, 18)

    def nrm(k, shape, scale):
        return jax.random.normal(k, shape, jnp.float32) * scale

    return {
        "x": nrm(ks[0], (BATCH, SEQ, D_MODEL), 1.0),
        "c": nrm(ks[1], (BATCH, D_MODEL), 1.0),
        "w_c_down": nrm(ks[2], (D_MODEL, COND_RANK), D_MODEL ** -0.5),
        "w_c_up": nrm(ks[3], (DEPTH, COND_RANK, N_MOD * D_MODEL), 0.5 * COND_RANK ** -0.5),
        "norm_gains": 1.0 + nrm(ks[4], (DEPTH, 6, D_MODEL), 0.1),
        "w_in": nrm(ks[5], (DEPTH, D_MODEL, D_IN), D_MODEL ** -0.5),
        "lb_logits": nrm(ks[6], (DEPTH, A_WIDTH), 1.0),
        "hgrn_norm": 1.0 + nrm(ks[7], (DEPTH, A_DV), 0.1),
        "kv_norm": 1.0 + nrm(ks[8], (DEPTH, B_KV_RANK), 0.1),
        "w_kv_up": nrm(ks[9], (DEPTH, B_KV_RANK, 2 * B_DH), B_KV_RANK ** -0.5),
        "rel_table": nrm(ks[10], (REL_BUCKETS, B_HEADS + C_HEADS), 0.5),
        "sinks": nrm(ks[11], (DEPTH, C_HEADS), 1.0),
        "w_branch": nrm(ks[12], (DEPTH, N_BRANCH, BRANCH_W, D_MODEL), BRANCH_W ** -0.5),
        "w_out": nrm(ks[13], (DEPTH, D_MODEL, D_MODEL), D_MODEL ** -0.5),
        "ffn1_in": nrm(ks[14], (DEPTH, D_MODEL, 2 * D_FF), D_MODEL ** -0.5),
        "ffn1_out": nrm(ks[15], (DEPTH, D_FF, D_MODEL), D_FF ** -0.5),
        "ffn2_in": nrm(ks[16], (DEPTH, D_MODEL, 2 * D_FF), D_MODEL ** -0.5),
        "ffn2_out": nrm(ks[17], (DEPTH, D_FF, D_MODEL), D_FF ** -0.5),
    }


def reference(x, c, w_c_down, w_c_up, norm_gains, w_in, lb_logits, hgrn_norm, kv_norm,
              w_kv_up, rel_table, sinks, w_branch, w_out, ffn1_in, ffn1_out, ffn2_in, ffn2_out):
    bsz, s, d = x.shape
    cond = jax.nn.silu(c @ w_c_down)
    lb_p = jax.nn.softmax(lb_logits.astype(jnp.float32), axis=0)
    lb_cs = jnp.cumsum(lb_p, axis=0)
    lower_bounds = lb_cs - lb_cs[0:1]
    split_at = np.cumsum(IN_SPLITS)[:-1].tolist()

    for l in range(DEPTH):
        mod = (cond @ w_c_up[l]).reshape(bsz, N_MOD, d)[:, :, None, :]
        g = norm_gains[l]

        h = rms_norm(x, g[0]) * (1.0 + mod[:, 1]) + mod[:, 0]
        y = swiglu(h, ffn1_in[l], ffn1_out[l])
        x = x + FFN_RES * mod[:, 2] * rms_norm(y, g[1])

        h = rms_norm(x, g[2]) * (1.0 + mod[:, 4]) + mod[:, 3]
        (aq, af, ai, ag, bq, blat, biq, bik, biw, cq, ck, cv, gates) = jnp.split(
            h @ w_in[l], split_at, axis=-1)
        ya = hgrn2_mixer(aq, af, ai, ag, lower_bounds[l], hgrn_norm[l])
        yb = dsa_mixer(bq, blat, biq, bik, biw, kv_norm[l], w_kv_up[l], rel_table[:, :B_HEADS])
        yc = swa_mixer(cq, ck, cv, sinks[l], rel_table[:, B_HEADS:])
        ga, gb, gc = jnp.split(jax.nn.sigmoid(gates), N_BRANCH, axis=-1)
        m = ga * (ya @ w_branch[l, 0]) + gb * (yb @ w_branch[l, 1]) + gc * (yc @ w_branch[l, 2])
        y = m @ w_out[l]
        x = x + mod[:, 5] * rms_norm(y, g[3])

        h = rms_norm(x, g[4]) * (1.0 + mod[:, 7]) + mod[:, 6]
        y = swiglu(h, ffn2_in[l], ffn2_out[l])
        x = x + FFN_RES * mod[:, 8] * rms_norm(y, g[5])
    return x
```

```python
import functools
import math

import jax
import jax.numpy as jnp
from jax import lax
from jax.experimental import pallas as pl
from jax.experimental.pallas import tpu as pltpu

A_HEADS = 8
A_DK = 128
A_DV = 128
A_WIDTH = A_HEADS * A_DK
B_HEADS = 8
B_DH = 128
B_KV_RANK = 512
IDX_HEADS = 16
IDX_DH = 64
TOPK_MAX = 256
C_HEADS = 16
C_KV_HEADS = 2
C_DH = 64
WINDOW = 128
REL_BUCKETS = 32
REL_MAX_DIST = 128
N_BRANCH = 3
BRANCH_W = 1024
FFN_RES = 0.5
N_MOD = 9
RMS_EPS = 1e-6
NEG_INF = -1e30

LANES = 128
VMEM_LIMIT_BYTES = 56 * 1024 * 1024

QBLK = 128
HGRN_CHUNK = 64
HGRN_SUB = 16
B_SMALL_W = 128

BF16 = jnp.bfloat16
F32 = jnp.float32


def _cparams(*sem):
    return pltpu.CompilerParams(dimension_semantics=sem, vmem_limit_bytes=VMEM_LIMIT_BYTES)


def _dot(a, b):
    return jnp.dot(a, b, preferred_element_type=F32)


def _dot_nt(a, b):
    return lax.dot_general(a, b, (((1,), (1,)), ((), ())), preferred_element_type=F32)


def _silu(x):
    return x * jax.nn.sigmoid(x)


def _lower_bound_kernel(x_ref, o_ref):
    x = x_ref[...]
    m = jnp.max(x, axis=0, keepdims=True)
    e = jnp.exp(x - m)
    p = e / jnp.sum(e, axis=0, keepdims=True)
    depth = x.shape[0]
    run = jnp.zeros_like(p[0:1])
    for l in range(depth):
        o_ref[l:l + 1, :] = run
        if l + 1 < depth:
            run = run + p[l + 1:l + 2]


def lower_bounds(lb_logits):
    return pl.pallas_call(
        _lower_bound_kernel,
        out_shape=jax.ShapeDtypeStruct(lb_logits.shape, F32),
        name="lower_bounds",
    )(lb_logits.astype(F32))


def _t5_bucket(dist):
    max_exact = REL_BUCKETS // 2
    d = jnp.maximum(dist, 0)
    df = jnp.maximum(d, 1).astype(F32)
    large = max_exact + (jnp.log(df / max_exact) / math.log(REL_MAX_DIST / max_exact)
                         * (REL_BUCKETS - max_exact)).astype(jnp.int32)
    large = jnp.minimum(large, REL_BUCKETS - 1)
    return jnp.where(d < max_exact, d, large)


def _bias_tile_kernel(tab_ref, o_ref):
    h = pl.program_id(0)
    row = lax.broadcasted_iota(jnp.int32, (QBLK, QBLK), 0)
    col = lax.broadcasted_iota(jnp.int32, (QBLK, QBLK), 1)
    for r in range(3):
        bucket = _t5_bucket(r * QBLK + row - col)
        tile = jnp.zeros((QBLK, QBLK), F32)
        for b in range(REL_BUCKETS):
            tile = jnp.where(bucket == b, tab_ref[h, b], tile)
        o_ref[0, r] = tile


def bias_tiles(rel_table):
    n_heads = rel_table.shape[1]
    return pl.pallas_call(
        _bias_tile_kernel,
        grid=(n_heads,),
        in_specs=[pl.BlockSpec(memory_space=pltpu.SMEM)],
        out_specs=pl.BlockSpec((1, 3, QBLK, QBLK), lambda h: (h, 0, 0, 0)),
        out_shape=jax.ShapeDtypeStruct((n_heads, 3, QBLK, QBLK), F32),
        compiler_params=_cparams("arbitrary"),
        name="bias_tiles",
    )(rel_table.T.astype(F32))


def _rms(x, g):
    return x * lax.rsqrt(jnp.mean(x * x, axis=-1, keepdims=True) + RMS_EPS) * g


def _normmod_kernel(x_ref, g_ref, mod_ref, o_ref, *, gi, shift_i, scale_i):
    y = _rms(x_ref[...], g_ref[gi:gi + 1, :])
    scale = mod_ref[0, scale_i:scale_i + 1, :]
    shift = mod_ref[0, shift_i:shift_i + 1, :]
    o_ref[...] = (y * (1.0 + scale) + shift).astype(o_ref.dtype)


def normmod(x, gains, mod, seq, *, gi, shift_i, scale_i, tm=256):
    m, d = x.shape
    tpb = seq // tm
    return pl.pallas_call(
        functools.partial(_normmod_kernel, gi=gi, shift_i=shift_i, scale_i=scale_i),
        grid=(m // tm,),
        in_specs=[pl.BlockSpec((tm, d), lambda i: (i, 0)),
                  pl.BlockSpec(gains.shape, lambda i: (0, 0)),
                  pl.BlockSpec((1, N_MOD, d), lambda i: (i // tpb, 0, 0))],
        out_specs=pl.BlockSpec((tm, d), lambda i: (i, 0)),
        out_shape=jax.ShapeDtypeStruct((m, d), BF16),
        compiler_params=_cparams("parallel"),
        name="normmod",
    )(x, gains, mod)


def _resid_kernel(x_ref, y_ref, g_ref, mod_ref, o_ref, *, gi, gate_i, coef):
    y = _rms(y_ref[...], g_ref[gi:gi + 1, :])
    gate = mod_ref[0, gate_i:gate_i + 1, :]
    o_ref[...] = x_ref[...] + coef * gate * y


def resid(x, y, gains, mod, seq, *, gi, gate_i, coef, tm=256):
    m, d = x.shape
    tpb = seq // tm
    return pl.pallas_call(
        functools.partial(_resid_kernel, gi=gi, gate_i=gate_i, coef=coef),
        grid=(m // tm,),
        in_specs=[pl.BlockSpec((tm, d), lambda i: (i, 0)),
                  pl.BlockSpec((tm, d), lambda i: (i, 0)),
                  pl.BlockSpec(gains.shape, lambda i: (0, 0)),
                  pl.BlockSpec((1, N_MOD, d), lambda i: (i // tpb, 0, 0))],
        out_specs=pl.BlockSpec((tm, d), lambda i: (i, 0)),
        out_shape=jax.ShapeDtypeStruct((m, d), F32),
        compiler_params=_cparams("parallel"),
        name="resid",
    )(x, y, gains, mod)


def _mm_kernel(a_ref, w_ref, o_ref, *, act):
    acc = _dot(a_ref[...], w_ref[...])
    if act == "sigmoid":
        acc = jax.nn.sigmoid(acc)
    elif act == "silu":
        acc = _silu(acc)
    o_ref[...] = acc.astype(o_ref.dtype)


def matmul(a, w, *, out_dtype, act=None, tm=512, tn=512):
    m, k = a.shape
    n = w.shape[1]
    tm, tn = min(tm, m), min(tn, n)
    assert m % tm == 0 and n % tn == 0
    return pl.pallas_call(
        functools.partial(_mm_kernel, act=act),
        grid=(m // tm, n // tn),
        in_specs=[pl.BlockSpec((tm, k), lambda i, j: (i, 0)),
                  pl.BlockSpec((k, tn), lambda i, j: (0, j))],
        out_specs=pl.BlockSpec((tm, tn), lambda i, j: (i, j)),
        out_shape=jax.ShapeDtypeStruct((m, n), out_dtype),
        compiler_params=_cparams("parallel", "arbitrary"),
        name="matmul",
    )(a, w)


def _swiglu_kernel(a_ref, wu_ref, wv_ref, o_ref):
    a = a_ref[...]
    u = _dot(a, wu_ref[...])
    v = _dot(a, wv_ref[...])
    o_ref[...] = (_silu(u) * v).astype(o_ref.dtype)


def swiglu_in(a, w, *, tm=512, tn=512):
    m, k = a.shape
    f = w.shape[1] // 2
    nf = f // tn
    return pl.pallas_call(
        _swiglu_kernel,
        grid=(m // tm, nf),
        in_specs=[pl.BlockSpec((tm, k), lambda i, j: (i, 0)),
                  pl.BlockSpec((k, tn), lambda i, j: (0, j)),
                  pl.BlockSpec((k, tn), lambda i, j: (0, j + nf))],
        out_specs=pl.BlockSpec((tm, tn), lambda i, j: (i, j)),
        out_shape=jax.ShapeDtypeStruct((m, f), BF16),
        compiler_params=_cparams("parallel", "arbitrary"),
        name="swiglu_in",
    )(a, w, w)


def _merge_kernel(ya_ref, yb_ref, yc_ref, w_ref, ga_ref, gb_ref, gc_ref, o_ref):
    acc = ga_ref[...].astype(F32) * _dot(ya_ref[...], w_ref[0])
    acc = acc + gb_ref[...].astype(F32) * _dot(yb_ref[...], w_ref[1])
    acc = acc + gc_ref[...].astype(F32) * _dot(yc_ref[...], w_ref[2])
    o_ref[...] = acc.astype(o_ref.dtype)


def merge(ya, yb, yc, w_branch, gates, *, tm=512, tn=512):
    m, kw = ya.shape
    d = w_branch.shape[2]
    nd = d // tn
    y_spec = pl.BlockSpec((tm, kw), lambda i, j: (i, 0))
    return pl.pallas_call(
        _merge_kernel,
        grid=(m // tm, nd),
        in_specs=[y_spec, y_spec, y_spec,
                  pl.BlockSpec((N_BRANCH, kw, tn), lambda i, j: (0, 0, j)),
                  pl.BlockSpec((tm, tn), lambda i, j: (i, j)),
                  pl.BlockSpec((tm, tn), lambda i, j: (i, j + nd)),
                  pl.BlockSpec((tm, tn), lambda i, j: (i, j + 2 * nd))],
        out_specs=pl.BlockSpec((tm, tn), lambda i, j: (i, j)),
        out_shape=jax.ShapeDtypeStruct((m, d), BF16),
        compiler_params=_cparams("parallel", "arbitrary"),
        name="merge",
    )(ya, yb, yc, w_branch, gates, gates, gates)


def _kvup_kernel(lat_ref, g_ref, w_ref, k_ref, v_ref):
    h = _rms(lat_ref[...], g_ref[...]).astype(BF16)
    kv = _dot(h, w_ref[...])
    k_ref[...] = kv[:, :B_DH].astype(k_ref.dtype)
    v_ref[...] = kv[:, B_DH:].astype(v_ref.dtype)


def kv_up(bproj, lat_block, kv_norm_g, w_kv_up, *, tm=512):
    m = bproj.shape[0]
    out = jax.ShapeDtypeStruct((m, B_DH), BF16)
    return pl.pallas_call(
        _kvup_kernel,
        grid=(m // tm,),
        in_specs=[pl.BlockSpec((tm, B_KV_RANK), lambda i: (i, lat_block)),
                  pl.BlockSpec((1, B_KV_RANK), lambda i: (0, 0)),
                  pl.BlockSpec((B_KV_RANK, 2 * B_DH), lambda i: (0, 0))],
        out_specs=[pl.BlockSpec((tm, B_DH), lambda i: (i, 0)),
                   pl.BlockSpec((tm, B_DH), lambda i: (i, 0))],
        out_shape=[out, out],
        compiler_params=_cparams("parallel"),
        name="kv_up",
    )(bproj, kv_norm_g.reshape(1, B_KV_RANK), w_kv_up)


def _hgrn_kernel(q_ref, f_ref, i_ref, g_ref, lb_ref, ng_ref, o_ref, st_ref, *, seq):
    cs, sub = HGRN_CHUNK, HGRN_SUB
    st_ref[...] = jnp.zeros_like(st_ref)
    lb = lb_ref[...]
    ng = ng_ref[...]
    row_c = lax.broadcasted_iota(jnp.int32, (cs, cs), 0)
    col_c = lax.broadcasted_iota(jnp.int32, (cs, cs), 1)
    tril = (col_c <= row_c).astype(F32)
    row_s = lax.broadcasted_iota(jnp.int32, (sub, 1), 0)

    def chunk(c, carry):
        r0 = pl.multiple_of(c * cs, cs)
        q = q_ref[pl.ds(r0, cs), :]
        f = lb + (1.0 - lb) * jax.nn.sigmoid(f_ref[pl.ds(r0, cs), :])
        lf = jnp.log(jnp.maximum(f, 1e-20))
        k = 1.0 - f
        v = _silu(i_ref[pl.ds(r0, cs), :])
        b = jnp.dot(tril, lf, precision=lax.Precision.HIGHEST, preferred_element_type=F32)
        st = st_ref[...]
        o_inter = _dot_nt((q * jnp.exp(b)).astype(BF16), st.astype(BF16))

        outs = []
        for si in range(cs // sub):
            lo = si * sub
            q_s = q[lo:lo + sub]
            b_s = b[lo:lo + sub]
            o_s = o_inter[lo:lo + sub]
            if si > 0:
                bref = b[lo - 1:lo]
                qt = (q_s * jnp.exp(b_s - bref)).astype(BF16)
                kj = (k[:lo] * jnp.exp(bref - b[:lo])).astype(BF16)
                attn = _dot_nt(qt, kj)
                o_s = o_s + _dot(attn.astype(BF16), v[:lo].astype(BF16))
            for s in range(sub):
                a = q_s * k[lo + s:lo + s + 1] * jnp.exp(jnp.minimum(b_s - b[lo + s:lo + s + 1], 0.0))
                r = jnp.sum(a, axis=-1, keepdims=True)
                r = jnp.where(row_s >= s, r, 0.0)
                o_s = o_s + r * v[lo + s:lo + s + 1]
            outs.append(o_s)
        o = jnp.concatenate(outs, axis=0)

        b_end = b[cs - 1:cs]
        kd = (k * jnp.exp(b_end - b)).astype(BF16)
        upd = _dot(v.T.astype(BF16), kd)
        st_ref[...] = st * jnp.exp(b_end) + upd

        o = _rms(o, ng)
        o_ref[pl.ds(r0, cs), :] = (o * _silu(g_ref[pl.ds(r0, cs), :])).astype(o_ref.dtype)
        return carry

    lax.fori_loop(0, seq // cs, chunk, 0)


def hgrn2(aproj, lb, norm_g, seq):
    m = aproj.shape[0]
    bsz = m // seq

    def col(part):
        return pl.BlockSpec((seq, A_DK), lambda b, h: (b, part * A_HEADS + h))

    return pl.pallas_call(
        functools.partial(_hgrn_kernel, seq=seq),
        grid=(bsz, A_HEADS),
        in_specs=[col(0), col(1), col(2), col(3),
                  pl.BlockSpec((1, A_DK), lambda b, h: (0, h)),
                  pl.BlockSpec((1, A_DV), lambda b, h: (0, 0))],
        out_specs=pl.BlockSpec((seq, A_DV), lambda b, h: (b, h)),
        out_shape=jax.ShapeDtypeStruct((m, A_HEADS * A_DV), BF16),
        scratch_shapes=[pltpu.VMEM((A_DV, A_DK), F32)],
        compiler_params=_cparams("parallel", "parallel"),
        name="hgrn2",
    )(aproj, aproj, aproj, aproj, lb.reshape(1, A_WIDTH), norm_g.reshape(1, A_DV))


def _dsa_kernel(q_ref, iq_ref, smq_ref, smk_ref, k_ref, v_ref, bt_ref, o_ref,
                sc_ref, iw_ref, acc_ref, *, nblk, topk):
    j = pl.program_id(1)
    row = lax.broadcasted_iota(jnp.int32, (QBLK, QBLK), 0)
    col = lax.broadcasted_iota(jnp.int32, (QBLK, QBLK), 1)
    causal = col <= row

    iq = iq_ref[...].astype(BF16)
    iw = smq_ref[:, IDX_DH:IDX_DH + IDX_HEADS] * (IDX_HEADS ** -0.5 * IDX_DH ** -0.5)
    for h in range(IDX_HEADS):
        iw_ref[h] = jnp.broadcast_to(iw[:, h:h + 1], (QBLK, QBLK))

    for kb in range(nblk):
        @pl.when(kb <= j)
        def _():
            ik = smk_ref[kb * QBLK:(kb + 1) * QBLK, 0:IDX_DH].astype(BF16)
            sc = jnp.zeros((QBLK, QBLK), F32)
            for h in range(IDX_HEADS):
                d = _dot_nt(iq[:, h * IDX_DH:(h + 1) * IDX_DH], ik)
                sc = sc + jnp.maximum(d, 0.0) * iw_ref[h]
            sc_ref[kb] = jnp.where(jnp.logical_or(kb < j, causal), sc, NEG_INF)

        @pl.when(kb > j)
        def _():
            sc_ref[kb] = jnp.full((QBLK, QBLK), NEG_INF, F32)

    bits = lax.bitcast_convert_type(sc_ref[...], jnp.int32)
    key = bits ^ ((bits >> 31) & jnp.int32(0x7FFFFFFF))
    int_min = jnp.int32(-2 ** 31)
    kf = jnp.float32(topk)

    def count(mask):
        c = jnp.sum(jnp.where(mask, 1.0, 0.0), axis=0)
        return jnp.sum(c, axis=-1, keepdims=True)

    def thr_step(it, cand):
        trial = cand | lax.shift_left(jnp.int32(1), 31 - it)
        cnt = count(key >= (trial ^ int_min)[None])
        return jnp.where(cnt >= kf, trial, cand)

    cand = lax.fori_loop(0, 32, thr_step, jnp.zeros((QBLK, 1), jnp.int32))
    thr = (cand ^ int_min)[None]
    gt = key > thr
    eq = key == thr
    need = kf - count(gt)
    kidx = (lax.broadcasted_iota(jnp.int32, (nblk, QBLK, QBLK), 0) * QBLK
            + lax.broadcasted_iota(jnp.int32, (nblk, QBLK, QBLK), 2))
    nbits = (nblk * QBLK - 1).bit_length()

    def idx_step(it, pos):
        trial = pos + lax.shift_left(jnp.int32(1), nbits - 1 - it)
        cnt = count(jnp.logical_and(eq, kidx < trial[None]))
        return jnp.where(cnt < need, trial, pos)

    pos = lax.fori_loop(0, nbits, idx_step, jnp.zeros((QBLK, 1), jnp.int32))
    sel = jnp.logical_or(gt, jnp.logical_and(eq, kidx <= pos[None]))
    sc_ref[...] = jnp.where(sel, 0.0, NEG_INF)

    q = q_ref[...].astype(BF16)
    acc_ref[...] = jnp.zeros_like(acc_ref)
    scale = B_DH ** -0.5

    def att_step(kb, carry):
        ms, ls = carry
        r0 = pl.multiple_of(kb * QBLK, QBLK)
        kt = k_ref[pl.ds(r0, QBLK), :]
        vt = v_ref[pl.ds(r0, QBLK), :]
        mask = sc_ref[kb]
        mask = jnp.where(jnp.logical_or(kb < j, causal), mask, NEG_INF)
        rel = jnp.minimum(j - kb, 2)
        new_ms, new_ls = [], []
        for h in range(B_HEADS):
            lg = _dot_nt(q[:, h * B_DH:(h + 1) * B_DH], kt) * scale + bt_ref[h, rel] + mask
            m_new = jnp.maximum(ms[h], jnp.max(lg, axis=-1, keepdims=True))
            alpha = jnp.exp(ms[h] - m_new)
            p = jnp.exp(lg - m_new)
            new_ls.append(alpha * ls[h] + jnp.sum(p, axis=-1, keepdims=True))
            acc_ref[h] = alpha * acc_ref[h] + _dot(p.astype(BF16), vt)
            new_ms.append(m_new)
        return tuple(new_ms), tuple(new_ls)

    init = (tuple(jnp.full((QBLK, 1), NEG_INF, F32) for _ in range(B_HEADS)),
            tuple(jnp.zeros((QBLK, 1), F32) for _ in range(B_HEADS)))
    _, ls = lax.fori_loop(0, j + 1, att_step, init)
    for h in range(B_HEADS):
        o_ref[:, h * B_DH:(h + 1) * B_DH] = (acc_ref[h] / ls[h]).astype(o_ref.dtype)


def dsa(bproj, k, v, bias_b, seq):
    m = bproj.shape[0]
    bsz = m // seq
    nblk = seq // QBLK
    topk = min(TOPK_MAX, seq // 4)
    qw = B_HEADS * B_DH
    small_blk = (2 * qw + B_KV_RANK) // B_SMALL_W
    return pl.pallas_call(
        functools.partial(_dsa_kernel, nblk=nblk, topk=topk),
        grid=(bsz, nblk),
        in_specs=[pl.BlockSpec((QBLK, qw), lambda b, j: (b * nblk + j, 0)),
                  pl.BlockSpec((QBLK, qw), lambda b, j: (b * nblk + j, 1)),
                  pl.BlockSpec((QBLK, B_SMALL_W), lambda b, j: (b * nblk + j, small_blk)),
                  pl.BlockSpec((seq, B_SMALL_W), lambda b, j: (b, small_blk)),
                  pl.BlockSpec((seq, B_DH), lambda b, j: (b, 0)),
                  pl.BlockSpec((seq, B_DH), lambda b, j: (b, 0)),
                  pl.BlockSpec((B_HEADS, 3, QBLK, QBLK), lambda b, j: (0, 0, 0, 0))],
        out_specs=pl.BlockSpec((QBLK, qw), lambda b, j: (b * nblk + j, 0)),
        out_shape=jax.ShapeDtypeStruct((m, qw), BF16),
        scratch_shapes=[pltpu.VMEM((nblk, QBLK, QBLK), F32),
                        pltpu.VMEM((IDX_HEADS, QBLK, QBLK), F32),
                        pltpu.VMEM((B_HEADS, QBLK, B_DH), F32)],
        compiler_params=_cparams("parallel", "arbitrary"),
        name="dsa",
    )(bproj, bproj, bproj, bproj, k, v, bias_b)


def _swa_kernel(sink_ref, q_ref, kc_ref, vc_ref, kp_ref, vp_ref, bt_ref, o_ref):
    j = pl.program_id(1)
    row = lax.broadcasted_iota(jnp.int32, (QBLK, QBLK), 0)
    col = lax.broadcasted_iota(jnp.int32, (QBLK, QBLK), 1)
    cur_ok = col <= row
    prev_ok = jnp.logical_and(col > row, j > 0)
    q = q_ref[...].astype(BF16)
    kc = kc_ref[...].astype(BF16)
    vc = vc_ref[...].astype(BF16)
    kp = kp_ref[...].astype(BF16)
    vp = vp_ref[...].astype(BF16)
    scale = C_DH ** -0.5
    grp = C_HEADS // C_KV_HEADS
    for h in range(C_HEADS):
        g = h // grp
        qh = q[:, h * C_DH:(h + 1) * C_DH]
        lc = _dot_nt(qh, kc[:, g * C_DH:(g + 1) * C_DH]) * scale + bt_ref[h, 0]
        lp = _dot_nt(qh, kp[:, g * C_DH:(g + 1) * C_DH]) * scale + bt_ref[h, 1]
        lc = jnp.where(cur_ok, lc, NEG_INF)
        lp = jnp.where(prev_ok, lp, NEG_INF)
        sink = sink_ref[h]
        mx = jnp.maximum(jnp.maximum(jnp.max(lc, axis=-1, keepdims=True),
                                     jnp.max(lp, axis=-1, keepdims=True)), sink)
        pc = jnp.exp(lc - mx)
        pp = jnp.exp(lp - mx)
        den = (jnp.sum(pc, axis=-1, keepdims=True) + jnp.sum(pp, axis=-1, keepdims=True)
               + jnp.exp(sink - mx))
        o = (_dot(pc.astype(BF16), vc[:, g * C_DH:(g + 1) * C_DH])
             + _dot(pp.astype(BF16), vp[:, g * C_DH:(g + 1) * C_DH]))
        o_ref[:, h * C_DH:(h + 1) * C_DH] = (o / den).astype(o_ref.dtype)


def swa(cproj, sinks, bias_c, seq):
    m = cproj.shape[0]
    bsz = m // seq
    nblk = seq // QBLK
    qw = C_HEADS * C_DH
    kvw = C_KV_HEADS * C_DH
    kblk = qw // kvw

    def cur(off):
        return pl.BlockSpec((QBLK, kvw), lambda b, j: (b * nblk + j, kblk + off))

    def prev(off):
        return pl.BlockSpec((QBLK, kvw), lambda b, j: (b * nblk + jnp.maximum(j - 1, 0), kblk + off))

    return pl.pallas_call(
        _swa_kernel,
        grid=(bsz, nblk),
        in_specs=[pl.BlockSpec(memory_space=pltpu.SMEM),
                  pl.BlockSpec((QBLK, qw), lambda b, j: (b * nblk + j, 0)),
                  cur(0), cur(1), prev(0), prev(1),
                  pl.BlockSpec((C_HEADS, 3, QBLK, QBLK), lambda b, j: (0, 0, 0, 0))],
        out_specs=pl.BlockSpec((QBLK, qw), lambda b, j: (b * nblk + j, 0)),
        out_shape=jax.ShapeDtypeStruct((m, qw), BF16),
        compiler_params=_cparams("parallel", "parallel"),
        name="swa",
    )(sinks.astype(F32), cproj, cproj, cproj, cproj, cproj, bias_c)


def _pack_w_in(w):
    o = 0
    parts = {}
    for name, width in (("a", 4 * A_WIDTH), ("bq", B_HEADS * B_DH), ("blat", B_KV_RANK),
                        ("biq", IDX_HEADS * IDX_DH), ("bik", IDX_DH), ("biw", IDX_HEADS),
                        ("c", (C_HEADS + 2 * C_KV_HEADS) * C_DH)):
        parts[name] = w[:, o:o + width]
        o += width
    gates = w[:, o:]
    pad = jnp.zeros((w.shape[0], B_SMALL_W - IDX_DH - IDX_HEADS), w.dtype)
    wb = jnp.concatenate([parts["bq"], parts["biq"], parts["blat"], parts["bik"], parts["biw"], pad], axis=1)
    return (parts["a"].astype(BF16), wb.astype(BF16), parts["c"].astype(BF16), gates.astype(BF16))


def kernel(x, c, w_c_down, w_c_up, norm_gains, w_in, lb_logits, hgrn_norm, kv_norm, w_kv_up, rel_table,
           sinks, w_branch, w_out, ffn1_in, ffn1_out, ffn2_in, ffn2_out):
    bsz, seq, d = x.shape
    depth = w_in.shape[0]
    m = bsz * seq
    x = x.reshape(m, d).astype(F32)

    cond = matmul(c.astype(BF16), w_c_down.astype(BF16), out_dtype=BF16, act="silu")
    lbs = lower_bounds(lb_logits)
    bias = bias_tiles(rel_table)
    bias_b, bias_c = bias[:B_HEADS], bias[B_HEADS:]
    lat_block = 2 * B_HEADS * B_DH // B_KV_RANK

    for l in range(depth):
        mod = matmul(cond, w_c_up[l].astype(BF16), out_dtype=F32, tn=4096).reshape(bsz, N_MOD, d)
        g = norm_gains[l].astype(F32)

        h = normmod(x, g, mod, seq, gi=0, shift_i=0, scale_i=1)
        u = swiglu_in(h, ffn1_in[l].astype(BF16))
        y = matmul(u, ffn1_out[l].astype(BF16), out_dtype=F32)
        x = resid(x, y, g, mod, seq, gi=1, gate_i=2, coef=FFN_RES)

        h = normmod(x, g, mod, seq, gi=2, shift_i=3, scale_i=4)
        wa, wb, wc, wg = _pack_w_in(w_in[l])
        aproj = matmul(h, wa, out_dtype=F32)
        bproj = matmul(h, wb, out_dtype=F32, tn=wb.shape[1] // 3)
        cproj = matmul(h, wc, out_dtype=F32, tn=wc.shape[1] // 2)
        gates = matmul(h, wg, out_dtype=BF16, act="sigmoid")
        ya = hgrn2(aproj, lbs[l], hgrn_norm[l].astype(F32), seq)
        kb, vb = kv_up(bproj, lat_block, kv_norm[l].astype(F32), w_kv_up[l].astype(BF16))
        yb = dsa(bproj, kb, vb, bias_b, seq)
        yc = swa(cproj, sinks[l], bias_c, seq)
        mix = merge(ya, yb, yc, w_branch[l].astype(BF16), gates)
        y = matmul(mix, w_out[l].astype(BF16), out_dtype=F32)
        x = resid(x, y, g, mod, seq, gi=3, gate_i=5, coef=1.0)

        h = normmod(x, g, mod, seq, gi=4, shift_i=6, scale_i=7)
        u = swiglu_in(h, ffn2_in[l].astype(BF16))
        y = matmul(u, ffn2_out[l].astype(BF16), out_dtype=F32)
        x = resid(x, y, g, mod, seq, gi=5, gate_i=8, coef=FFN_RES)
    return x.reshape(bsz, seq, d)
```

```python
import functools
import math

import jax
import jax.numpy as jnp
from jax import lax
from jax.experimental import pallas as pl
from jax.experimental.pallas import tpu as pltpu

A_HEADS = 8
A_DK = 128
A_DV = 128
A_WIDTH = A_HEADS * A_DK
B_HEADS = 8
B_DH = 128
B_KV_RANK = 512
IDX_HEADS = 16
IDX_DH = 64
TOPK_MAX = 256
C_HEADS = 16
C_KV_HEADS = 2
C_DH = 64
WINDOW = 128
REL_BUCKETS = 32
REL_MAX_DIST = 128
N_BRANCH = 3
BRANCH_W = 1024
FFN_RES = 0.5
N_MOD = 9
RMS_EPS = 1e-6
NEG_INF = -1e30
LOG2_E = math.log2(math.e)

VMEM_LIMIT_BYTES = 56 * 1024 * 1024

QBLK = 128
HGRN_CHUNK = 64
HGRN_SUB = 16
HGRN_HEADS_PER_STEP = 2
TOPK_GROUP = 4
B_SMALL_W = 128

BF16 = jnp.bfloat16
F32 = jnp.float32


def _cparams(n_grid_dims):
    return pltpu.CompilerParams(dimension_semantics=("arbitrary",) * n_grid_dims,
                                vmem_limit_bytes=VMEM_LIMIT_BYTES)


def _dot(a, b):
    return jnp.dot(a, b, preferred_element_type=F32)


def _dot_nt(a, b):
    return lax.dot_general(a, b, (((1,), (1,)), ((), ())), preferred_element_type=F32)


def _silu(x):
    return x * jax.nn.sigmoid(x)


def _lower_bound_kernel(x_ref, o_ref):
    x = x_ref[...]
    m = jnp.max(x, axis=0, keepdims=True)
    e = jnp.exp(x - m)
    p = e / jnp.sum(e, axis=0, keepdims=True)
    depth = x.shape[0]
    run = jnp.zeros_like(p[0:1])
    for l in range(depth):
        o_ref[l:l + 1, :] = run
        if l + 1 < depth:
            run = run + p[l + 1:l + 2]


def lower_bounds(lb_logits):
    return pl.pallas_call(
        _lower_bound_kernel,
        out_shape=jax.ShapeDtypeStruct(lb_logits.shape, F32),
        name="lower_bounds",
    )(lb_logits.astype(F32))


def _t5_bucket(dist):
    max_exact = REL_BUCKETS // 2
    d = jnp.maximum(dist, 0)
    df = jnp.maximum(d, 1).astype(F32)
    large = max_exact + (jnp.log(df / max_exact) / math.log(REL_MAX_DIST / max_exact)
                         * (REL_BUCKETS - max_exact)).astype(jnp.int32)
    large = jnp.minimum(large, REL_BUCKETS - 1)
    return jnp.where(d < max_exact, d, large)


def _bias_tile_kernel(tab_ref, o_ref, *, keys_on_rows):
    h = pl.program_id(0)
    row = lax.broadcasted_iota(jnp.int32, (QBLK, QBLK), 0)
    col = lax.broadcasted_iota(jnp.int32, (QBLK, QBLK), 1)
    t_minus_s = col - row if keys_on_rows else row - col
    for r in range(3):
        bucket = _t5_bucket(r * QBLK + t_minus_s)
        tile = jnp.zeros((QBLK, QBLK), F32)
        for b in range(REL_BUCKETS):
            tile = jnp.where(bucket == b, tab_ref[h, b], tile)
        if keys_on_rows:
            o_ref[r] = tile
        else:
            o_ref[r, 0] = tile


def bias_tiles(rel_table, *, keys_on_rows):
    n_heads = rel_table.shape[1]
    if keys_on_rows:
        out_spec = pl.BlockSpec((3, QBLK, QBLK), lambda h: (0, 0, h))
        out_shape = (3, QBLK, n_heads * QBLK)
    else:
        out_spec = pl.BlockSpec((3, 1, QBLK, QBLK), lambda h: (0, h, 0, 0))
        out_shape = (3, n_heads, QBLK, QBLK)
    return pl.pallas_call(
        functools.partial(_bias_tile_kernel, keys_on_rows=keys_on_rows),
        grid=(n_heads,),
        in_specs=[pl.BlockSpec(memory_space=pltpu.SMEM)],
        out_specs=out_spec,
        out_shape=jax.ShapeDtypeStruct(out_shape, F32),
        compiler_params=_cparams(1),
        name="bias_tiles",
    )(rel_table.T.astype(F32))


def _rms(x, g):
    return x * lax.rsqrt(jnp.mean(x * x, axis=-1, keepdims=True) + RMS_EPS) * g


def _modulate(x, g_ref, mod_ref, gi, shift_i, scale_i):
    y = _rms(x, g_ref[gi:gi + 1, :])
    return y * (1.0 + mod_ref[0, scale_i:scale_i + 1, :]) + mod_ref[0, shift_i:shift_i + 1, :]


def _normmod_kernel(x_ref, g_ref, mod_ref, h_ref, *, gi, shift_i, scale_i):
    h_ref[...] = _modulate(x_ref[...], g_ref, mod_ref, gi, shift_i, scale_i).astype(h_ref.dtype)


def normmod(x, gains, mod, seq, *, gi, shift_i, scale_i, tm=256):
    m, d = x.shape
    tpb = seq // tm
    return pl.pallas_call(
        functools.partial(_normmod_kernel, gi=gi, shift_i=shift_i, scale_i=scale_i),
        grid=(m // tm,),
        in_specs=[pl.BlockSpec((tm, d), lambda i: (i, 0)),
                  pl.BlockSpec(gains.shape, lambda i: (0, 0)),
                  pl.BlockSpec((1, N_MOD, d), lambda i: (i // tpb, 0, 0))],
        out_specs=pl.BlockSpec((tm, d), lambda i: (i, 0)),
        out_shape=jax.ShapeDtypeStruct((m, d), BF16),
        compiler_params=_cparams(1),
        name="normmod",
    )(x, gains, mod)


def _resid_kernel(x_ref, y_ref, g_ref, mod_ref, x_out_ref, *, gi, gate_i, coef):
    y = _rms(y_ref[...], g_ref[gi:gi + 1, :])
    x_out_ref[...] = x_ref[...] + coef * mod_ref[0, gate_i:gate_i + 1, :] * y


def resid(x, y, gains, mod, seq, *, gi, gate_i, coef, tm=256):
    m, d = x.shape
    tpb = seq // tm
    row = pl.BlockSpec((tm, d), lambda i: (i, 0))
    return pl.pallas_call(
        functools.partial(_resid_kernel, gi=gi, gate_i=gate_i, coef=coef),
        grid=(m // tm,),
        in_specs=[row, row,
                  pl.BlockSpec(gains.shape, lambda i: (0, 0)),
                  pl.BlockSpec((1, N_MOD, d), lambda i: (i // tpb, 0, 0))],
        out_specs=row,
        out_shape=jax.ShapeDtypeStruct((m, d), F32),
        compiler_params=_cparams(1),
        name="resid",
    )(x, y, gains, mod)


def _resid_normmod_kernel(x_ref, y_ref, g_ref, mod_ref, g2_ref, mod2_ref, x_out_ref, h_ref, *,
                          gi, gate_i, coef, gi2, shift_i, scale_i):
    y = _rms(y_ref[...], g_ref[gi:gi + 1, :])
    x_new = x_ref[...] + coef * mod_ref[0, gate_i:gate_i + 1, :] * y
    x_out_ref[...] = x_new
    h_ref[...] = _modulate(x_new, g2_ref, mod2_ref, gi2, shift_i, scale_i).astype(h_ref.dtype)


def resid_normmod(x, y, gains, mod, gains2, mod2, seq, *, gi, gate_i, coef, gi2, shift_i, scale_i, tm=256):
    m, d = x.shape
    tpb = seq // tm
    row = pl.BlockSpec((tm, d), lambda i: (i, 0))
    gspec = pl.BlockSpec(gains.shape, lambda i: (0, 0))
    mspec = pl.BlockSpec((1, N_MOD, d), lambda i: (i // tpb, 0, 0))
    return pl.pallas_call(
        functools.partial(_resid_normmod_kernel, gi=gi, gate_i=gate_i, coef=coef, gi2=gi2,
                          shift_i=shift_i, scale_i=scale_i),
        grid=(m // tm,),
        in_specs=[row, row, gspec, mspec, gspec, mspec],
        out_specs=[row, row],
        out_shape=[jax.ShapeDtypeStruct((m, d), F32), jax.ShapeDtypeStruct((m, d), BF16)],
        compiler_params=_cparams(1),
        name="resid_normmod",
    )(x, y, gains, mod, gains2, mod2)


def _mm_kernel(a_ref, w_ref, o_ref, *, act):
    acc = _dot(a_ref[...], w_ref[...])
    if act == "sigmoid":
        acc = jax.nn.sigmoid(acc)
    elif act == "silu":
        acc = _silu(acc)
    o_ref[...] = acc.astype(o_ref.dtype)


def matmul(a, w, *, out_dtype, act=None, tm=1024, tn=512):
    m, k = a.shape
    n = w.shape[1]
    tm, tn = min(tm, m), min(tn, n)
    assert m % tm == 0 and n % tn == 0
    return pl.pallas_call(
        functools.partial(_mm_kernel, act=act),
        grid=(m // tm, n // tn),
        in_specs=[pl.BlockSpec((tm, k), lambda i, j: (i, 0)),
                  pl.BlockSpec((k, tn), lambda i, j: (0, j))],
        out_specs=pl.BlockSpec((tm, tn), lambda i, j: (i, j)),
        out_shape=jax.ShapeDtypeStruct((m, n), out_dtype),
        compiler_params=_cparams(2),
        name="matmul",
    )(a, w)


def _swiglu_kernel(a_ref, wu_ref, wv_ref, o_ref):
    a = a_ref[...]
    u = _dot(a, wu_ref[...])
    v = _dot(a, wv_ref[...])
    o_ref[...] = (_silu(u) * v).astype(o_ref.dtype)


def swiglu_in(a, w, *, tm=1024, tn=512):
    m, k = a.shape
    f = w.shape[1] // 2
    nf = f // tn
    return pl.pallas_call(
        _swiglu_kernel,
        grid=(m // tm, nf),
        in_specs=[pl.BlockSpec((tm, k), lambda i, j: (i, 0)),
                  pl.BlockSpec((k, tn), lambda i, j: (0, j)),
                  pl.BlockSpec((k, tn), lambda i, j: (0, j + nf))],
        out_specs=pl.BlockSpec((tm, tn), lambda i, j: (i, j)),
        out_shape=jax.ShapeDtypeStruct((m, f), BF16),
        compiler_params=_cparams(2),
        name="swiglu_in",
    )(a, w, w)


def _merge_kernel(ya_ref, yb_ref, yc_ref, w_ref, ga_ref, gb_ref, gc_ref, o_ref):
    acc = ga_ref[...].astype(F32) * _dot(ya_ref[...], w_ref[0])
    acc = acc + gb_ref[...].astype(F32) * _dot(yb_ref[...], w_ref[1])
    acc = acc + gc_ref[...].astype(F32) * _dot(yc_ref[...], w_ref[2])
    o_ref[...] = acc.astype(o_ref.dtype)


def merge(ya, yb, yc, w_branch, gates, *, tm=1024, tn=512):
    m, kw = ya.shape
    d = w_branch.shape[2]
    nd = d // tn
    y_spec = pl.BlockSpec((tm, kw), lambda i, j: (i, 0))
    return pl.pallas_call(
        _merge_kernel,
        grid=(m // tm, nd),
        in_specs=[y_spec, y_spec, y_spec,
                  pl.BlockSpec((N_BRANCH, kw, tn), lambda i, j: (0, 0, j)),
                  pl.BlockSpec((tm, tn), lambda i, j: (i, j)),
                  pl.BlockSpec((tm, tn), lambda i, j: (i, j + nd)),
                  pl.BlockSpec((tm, tn), lambda i, j: (i, j + 2 * nd))],
        out_specs=pl.BlockSpec((tm, tn), lambda i, j: (i, j)),
        out_shape=jax.ShapeDtypeStruct((m, d), BF16),
        compiler_params=_cparams(2),
        name="merge",
    )(ya, yb, yc, w_branch, gates, gates, gates)


def _kvup_kernel(lat_ref, g_ref, w_ref, k_ref, vt_ref):
    h = _rms(lat_ref[...], g_ref[...]).astype(BF16)
    kv = _dot(h, w_ref[...])
    k_ref[...] = kv[:, :B_DH].astype(k_ref.dtype)
    v = kv[:, B_DH:]
    for u in range(vt_ref.shape[0]):
        vt_ref[u] = v[u * QBLK:(u + 1) * QBLK].T.astype(vt_ref.dtype)


def kv_up(bproj, lat_block, kv_norm_g, w_kv_up, *, tm=512):
    m = bproj.shape[0]
    return pl.pallas_call(
        _kvup_kernel,
        grid=(m // tm,),
        in_specs=[pl.BlockSpec((tm, B_KV_RANK), lambda i: (i, lat_block)),
                  pl.BlockSpec((1, B_KV_RANK), lambda i: (0, 0)),
                  pl.BlockSpec((B_KV_RANK, 2 * B_DH), lambda i: (0, 0))],
        out_specs=[pl.BlockSpec((tm, B_DH), lambda i: (i, 0)),
                   pl.BlockSpec((tm // QBLK, B_DH, QBLK), lambda i: (i, 0, 0))],
        out_shape=[jax.ShapeDtypeStruct((m, B_DH), BF16),
                   jax.ShapeDtypeStruct((m // QBLK, B_DH, QBLK), BF16)],
        compiler_params=_cparams(1),
        name="kv_up",
    )(bproj, kv_norm_g.reshape(1, B_KV_RANK), w_kv_up)


def _hgrn_chunk(q, fr, ir, gr, lb, ng, st, row_s, tril):
    cs, sub = HGRN_CHUNK, HGRN_SUB
    f = lb + (1.0 - lb) * jax.nn.sigmoid(fr)
    lf = jnp.log(jnp.maximum(f, 1e-20))
    k = 1.0 - f
    v = _silu(ir)
    b = jnp.dot(tril, lf, precision=lax.Precision.HIGHEST, preferred_element_type=F32) * LOG2_E
    o_inter = _dot_nt((q * jnp.exp2(b)).astype(BF16), st.astype(BF16))

    outs = []
    for si in range(cs // sub):
        lo = si * sub
        q_s = q[lo:lo + sub]
        b_s = b[lo:lo + sub]
        o_s = o_inter[lo:lo + sub]
        if si > 0:
            bref = b[lo - 1:lo]
            qt = (q_s * jnp.exp2(b_s - bref)).astype(BF16)
            kj = (k[:lo] * jnp.exp2(bref - b[:lo])).astype(BF16)
            attn = _dot_nt(qt, kj)
            o_s = o_s + _dot(attn.astype(BF16), v[:lo].astype(BF16))
        for s in range(sub):
            a = q_s * k[lo + s:lo + s + 1] * jnp.exp2(b_s - b[lo + s:lo + s + 1])
            r = jnp.sum(a, axis=-1, keepdims=True)
            r = jnp.where(row_s >= s, r, 0.0)
            o_s = o_s + r * v[lo + s:lo + s + 1]
        outs.append(o_s)
    o = jnp.concatenate(outs, axis=0)

    b_end = b[cs - 1:cs]
    kd = (k * jnp.exp2(b_end - b)).astype(BF16)
    upd = _dot(v.T.astype(BF16), kd)
    st_new = st * jnp.exp2(b_end) + upd
    return _rms(o, ng) * _silu(gr), st_new


def _hgrn_kernel(q_ref, f_ref, i_ref, g_ref, lb_ref, ng_ref, o_ref, st_ref, *, seq):
    cs = HGRN_CHUNK
    st_ref[...] = jnp.zeros_like(st_ref)
    ng = ng_ref[...]
    row_c = lax.broadcasted_iota(jnp.int32, (cs, cs), 0)
    col_c = lax.broadcasted_iota(jnp.int32, (cs, cs), 1)
    tril = (col_c <= row_c).astype(F32)
    row_s = lax.broadcasted_iota(jnp.int32, (HGRN_SUB, 1), 0)

    def chunk(c, carry):
        r0 = pl.multiple_of(c * cs, cs)
        for h in range(HGRN_HEADS_PER_STEP):
            cols = slice(h * A_DK, (h + 1) * A_DK)
            o, st_new = _hgrn_chunk(q_ref[pl.ds(r0, cs), cols], f_ref[pl.ds(r0, cs), cols],
                                    i_ref[pl.ds(r0, cs), cols], g_ref[pl.ds(r0, cs), cols],
                                    lb_ref[:, cols], ng, st_ref[h], row_s, tril)
            st_ref[h] = st_new
            o_ref[pl.ds(r0, cs), cols] = o.astype(o_ref.dtype)
        return carry

    lax.fori_loop(0, seq // cs, chunk, 0)


def hgrn2(aproj, lb, norm_g, seq):
    m = aproj.shape[0]
    bsz = m // seq
    nh = HGRN_HEADS_PER_STEP
    steps = A_HEADS // nh

    def col(part):
        return pl.BlockSpec((seq, nh * A_DK), lambda b, h: (b, part * steps + h))

    return pl.pallas_call(
        functools.partial(_hgrn_kernel, seq=seq),
        grid=(bsz, steps),
        in_specs=[col(0), col(1), col(2), col(3),
                  pl.BlockSpec((1, nh * A_DK), lambda b, h: (0, h)),
                  pl.BlockSpec((1, A_DV), lambda b, h: (0, 0))],
        out_specs=pl.BlockSpec((seq, nh * A_DV), lambda b, h: (b, h)),
        out_shape=jax.ShapeDtypeStruct((m, A_HEADS * A_DV), BF16),
        scratch_shapes=[pltpu.VMEM((nh, A_DV, A_DK), F32)],
        compiler_params=_cparams(2),
        name="hgrn2",
    )(aproj, aproj, aproj, aproj, lb.reshape(1, A_WIDTH), norm_g.reshape(1, A_DV))


def _dsa_kernel(q_ref, iq_ref, smq_ref, smk_ref, k_ref, vt_ref, bt_ref, o_ref,
                key_ref, msk_ref, lg_ref, mx_ref, ls_ref, acc_ref, *, nblk, topk):
    j = pl.program_id(1)
    row = lax.broadcasted_iota(jnp.int32, (QBLK, QBLK), 0)
    col = lax.broadcasted_iota(jnp.int32, (QBLK, QBLK), 1)
    causal = row <= col
    n_grp = j // TOPK_GROUP + 1
    n_pair = j // 2 + 1
    int_min = jnp.int32(-2 ** 31)

    def sort_key(score):
        bits = lax.bitcast_convert_type(score, jnp.int32)
        return bits ^ ((bits >> 31) & jnp.int32(0x7FFFFFFF))

    iq = iq_ref[...].astype(BF16)
    iqs = jnp.concatenate([iq[:, h * IDX_DH:(h + 1) * IDX_DH] for h in range(IDX_HEADS)], axis=0)
    iwt = (smq_ref[:, IDX_DH:IDX_DH + IDX_HEADS] * (IDX_HEADS ** -0.5 * IDX_DH ** -0.5)).T

    def visible(kb):
        return jnp.logical_or(kb < j, jnp.logical_and(kb == j, causal))

    def score_tile(kb):
        r0 = pl.multiple_of(kb * QBLK, QBLK)
        ik = smk_ref[pl.ds(r0, QBLK), 0:IDX_DH].astype(BF16)
        d = _dot_nt(ik, iqs)
        sc = jnp.zeros((QBLK, QBLK), F32)
        for h in range(IDX_HEADS):
            sc = sc + jnp.maximum(d[:, h * QBLK:(h + 1) * QBLK], 0.0) * iwt[h:h + 1, :]
        key_ref[kb] = sort_key(jnp.where(visible(kb), sc, NEG_INF))

    def pairwise(tile_fn):
        def step(kp, carry):
            tile_fn(2 * kp)
            tile_fn(2 * kp + 1)
            return carry
        lax.fori_loop(0, n_pair, step, 0)

    pairwise(score_tile)

    def fill_step(kb, carry):
        key_ref[kb] = sort_key(jnp.full((QBLK, QBLK), NEG_INF, F32))
        return carry

    lax.fori_loop(2 * n_pair, n_grp * TOPK_GROUP, fill_step, 0)

    kf = jnp.float32(topk)

    def count(pred):
        def grp(gi, c):
            for u in range(TOPK_GROUP):
                kb = gi * TOPK_GROUP + u
                c = c + jnp.where(pred(kb, key_ref[kb]), 1.0, 0.0)
            return c
        c = lax.fori_loop(0, n_grp, grp, jnp.zeros((QBLK, QBLK), F32))
        return jnp.sum(c, axis=0, keepdims=True)

    def thr_step(it, cand):
        trial = cand | lax.shift_left(jnp.int32(1), 31 - it)
        t_s = trial ^ int_min
        cnt = count(lambda kb, key: key >= t_s)
        return jnp.where(cnt >= kf, trial, cand)

    cand = lax.fori_loop(0, 32, thr_step, jnp.zeros((1, QBLK), jnp.int32))
    thr = cand ^ int_min
    n_gt = count(lambda kb, key: key > thr)
    n_ge = count(lambda kb, key: key >= thr)
    need = kf - n_gt
    nbits = (nblk * QBLK - 1).bit_length()

    def tie_search():
        def idx_step(it, pos):
            trial = pos + lax.shift_left(jnp.int32(1), nbits - 1 - it)
            cnt = count(lambda kb, key: jnp.logical_and(key == thr, kb * QBLK + row < trial))
            return jnp.where(cnt < need, trial, pos)
        return lax.fori_loop(0, nbits, idx_step, jnp.zeros((1, QBLK), jnp.int32))

    has_ties = jnp.max(n_ge) > kf
    pos = lax.cond(has_ties, tie_search, lambda: jnp.full((1, QBLK), nblk * QBLK, jnp.int32))

    def mask_tile(kb):
        key = key_ref[kb]
        sel = jnp.logical_or(key > thr, jnp.logical_and(key == thr, kb * QBLK + row <= pos))
        msk_ref[kb] = jnp.where(jnp.logical_and(sel, visible(kb)), 0.0, NEG_INF)

    pairwise(mask_tile)

    q = q_ref[...].astype(BF16)
    qs = jnp.concatenate([q[:, h * B_DH:(h + 1) * B_DH] for h in range(B_HEADS)], axis=0)
    scale = B_DH ** -0.5
    mx_ref[...] = jnp.full(mx_ref.shape, NEG_INF, F32)

    def logit_tile(kb):
        r0 = pl.multiple_of(kb * QBLK, QBLK)
        lg = _dot_nt(k_ref[pl.ds(r0, QBLK), :], qs)
        msk = msk_ref[kb]
        lg = lg * scale + bt_ref[jnp.clip(j - kb, 0, 2)] + jnp.concatenate([msk] * B_HEADS, axis=1)
        lg_ref[kb] = lg
        mx_ref[...] = jnp.maximum(mx_ref[...], lg)

    pairwise(logit_tile)
    mx = jnp.max(mx_ref[...], axis=0, keepdims=True)
    ls_ref[...] = jnp.zeros_like(ls_ref)
    acc_ref[...] = jnp.zeros_like(acc_ref)

    def pv_tile(kb):
        p = jnp.exp(lg_ref[kb] - mx)
        ls_ref[...] += p
        acc_ref[...] += _dot(vt_ref[kb], p.astype(BF16))

    pairwise(pv_tile)
    out_t = acc_ref[...] / jnp.sum(ls_ref[...], axis=0, keepdims=True)
    for h in range(B_HEADS):
        o_ref[:, h * B_DH:(h + 1) * B_DH] = out_t[:, h * QBLK:(h + 1) * QBLK].T.astype(o_ref.dtype)


def dsa(bproj, k, vt, bias_b, seq):
    m = bproj.shape[0]
    bsz = m // seq
    nblk = seq // QBLK
    assert nblk % TOPK_GROUP == 0
    topk = min(TOPK_MAX, seq // 4)
    qw = B_HEADS * B_DH
    small_blk = (2 * qw + B_KV_RANK) // B_SMALL_W
    cols = B_HEADS * QBLK
    return pl.pallas_call(
        functools.partial(_dsa_kernel, nblk=nblk, topk=topk),
        grid=(bsz, nblk),
        in_specs=[pl.BlockSpec((QBLK, qw), lambda b, j: (b * nblk + j, 0)),
                  pl.BlockSpec((QBLK, qw), lambda b, j: (b * nblk + j, 1)),
                  pl.BlockSpec((QBLK, B_SMALL_W), lambda b, j: (b * nblk + j, small_blk)),
                  pl.BlockSpec((seq, B_SMALL_W), lambda b, j: (b, small_blk)),
                  pl.BlockSpec((seq, B_DH), lambda b, j: (b, 0)),
                  pl.BlockSpec((nblk, B_DH, QBLK), lambda b, j: (b, 0, 0)),
                  pl.BlockSpec((3, QBLK, cols), lambda b, j: (0, 0, 0))],
        out_specs=pl.BlockSpec((QBLK, qw), lambda b, j: (b * nblk + j, 0)),
        out_shape=jax.ShapeDtypeStruct((m, qw), BF16),
        scratch_shapes=[pltpu.VMEM((nblk, QBLK, QBLK), jnp.int32),
                        pltpu.VMEM((nblk, QBLK, QBLK), F32),
                        pltpu.VMEM((nblk, QBLK, cols), F32),
                        pltpu.VMEM((QBLK, cols), F32),
                        pltpu.VMEM((QBLK, cols), F32),
                        pltpu.VMEM((B_DH, cols), F32)],
        compiler_params=_cparams(2),
        name="dsa",
    )(bproj, bproj, bproj, bproj, k, vt, bias_b)


def _swa_kernel(sink_ref, q_ref, kc_ref, vc_ref, kp_ref, vp_ref, bt_ref, o_ref):
    j = pl.program_id(1)
    row = lax.broadcasted_iota(jnp.int32, (QBLK, 2 * QBLK), 0)
    col = lax.broadcasted_iota(jnp.int32, (QBLK, 2 * QBLK), 1)
    ok = jnp.logical_or(jnp.logical_and(jnp.logical_and(col < QBLK, col > row), j > 0),
                        jnp.logical_and(col >= QBLK, col - QBLK <= row))
    q = q_ref[...].astype(BF16)
    kcat = jnp.concatenate([kp_ref[...], kc_ref[...]], axis=0).astype(BF16)
    vcat = jnp.concatenate([vp_ref[...], vc_ref[...]], axis=0).astype(BF16)
    scale = C_DH ** -0.5
    grp = C_HEADS // C_KV_HEADS
    for g in range(C_KV_HEADS):
        heads = range(g * grp, (g + 1) * grp)
        qs = jnp.concatenate([q[:, h * C_DH:(h + 1) * C_DH] for h in heads], axis=0)
        bias = jnp.concatenate([bt_ref[1, g * grp:(g + 1) * grp], bt_ref[0, g * grp:(g + 1) * grp]], axis=-1)
        sink = jnp.concatenate([jnp.full((1, QBLK, 1), sink_ref[h], F32) for h in heads], axis=0)
        lg = _dot_nt(qs, kcat[:, g * C_DH:(g + 1) * C_DH]).reshape(grp, QBLK, 2 * QBLK)
        lg = jnp.where(ok[None], lg * scale + bias, NEG_INF)
        mx = jnp.maximum(jnp.max(lg, axis=-1, keepdims=True), sink)
        p = jnp.exp(lg - mx)
        den = jnp.sum(p, axis=-1, keepdims=True) + jnp.exp(sink - mx)
        o = _dot(p.reshape(grp * QBLK, 2 * QBLK).astype(BF16), vcat[:, g * C_DH:(g + 1) * C_DH])
        o = o / den.reshape(grp * QBLK, 1)
        for u, h in enumerate(heads):
            o_ref[:, h * C_DH:(h + 1) * C_DH] = o[u * QBLK:(u + 1) * QBLK].astype(o_ref.dtype)


def swa(cproj, sinks, bias_c, seq):
    m = cproj.shape[0]
    bsz = m // seq
    nblk = seq // QBLK
    qw = C_HEADS * C_DH
    kvw = C_KV_HEADS * C_DH
    kblk = qw // kvw

    def cur(off):
        return pl.BlockSpec((QBLK, kvw), lambda b, j: (b * nblk + j, kblk + off))

    def prev(off):
        return pl.BlockSpec((QBLK, kvw), lambda b, j: (b * nblk + jnp.maximum(j - 1, 0), kblk + off))

    return pl.pallas_call(
        _swa_kernel,
        grid=(bsz, nblk),
        in_specs=[pl.BlockSpec(memory_space=pltpu.SMEM),
                  pl.BlockSpec((QBLK, qw), lambda b, j: (b * nblk + j, 0)),
                  cur(0), cur(1), prev(0), prev(1),
                  pl.BlockSpec((3, C_HEADS, QBLK, QBLK), lambda b, j: (0, 0, 0, 0))],
        out_specs=pl.BlockSpec((QBLK, qw), lambda b, j: (b * nblk + j, 0)),
        out_shape=jax.ShapeDtypeStruct((m, qw), BF16),
        compiler_params=_cparams(2),
        name="swa",
    )(sinks.astype(F32), cproj, cproj, cproj, cproj, cproj, bias_c)


def _pack_w_in(w):
    o = 0
    parts = {}
    for name, width in (("a", 4 * A_WIDTH), ("bq", B_HEADS * B_DH), ("blat", B_KV_RANK),
                        ("biq", IDX_HEADS * IDX_DH), ("bik", IDX_DH), ("biw", IDX_HEADS),
                        ("c", (C_HEADS + 2 * C_KV_HEADS) * C_DH)):
        parts[name] = w[:, o:o + width]
        o += width
    gates = w[:, o:]
    pad = jnp.zeros((w.shape[0], B_SMALL_W - IDX_DH - IDX_HEADS), w.dtype)
    wb = jnp.concatenate([parts["bq"], parts["biq"], parts["blat"], parts["bik"], parts["biw"], pad], axis=1)
    return (parts["a"].astype(BF16), wb.astype(BF16), parts["c"].astype(BF16), gates.astype(BF16))


def kernel(x, c, w_c_down, w_c_up, norm_gains, w_in, lb_logits, hgrn_norm, kv_norm, w_kv_up, rel_table,
           sinks, w_branch, w_out, ffn1_in, ffn1_out, ffn2_in, ffn2_out):
    bsz, seq, d = x.shape
    depth = w_in.shape[0]
    m = bsz * seq
    x = x.reshape(m, d).astype(F32)

    cond = matmul(c.astype(BF16), w_c_down.astype(BF16), out_dtype=BF16, act="silu")
    lbs = lower_bounds(lb_logits)
    bias_b = bias_tiles(rel_table[:, :B_HEADS], keys_on_rows=True)
    bias_c = bias_tiles(rel_table[:, B_HEADS:], keys_on_rows=False)
    lat_block = 2 * B_HEADS * B_DH // B_KV_RANK
    mods = [matmul(cond, w_c_up[l].astype(BF16), out_dtype=F32, tn=4096).reshape(bsz, N_MOD, d)
            for l in range(depth)]
    gains = norm_gains.astype(F32)

    h = normmod(x, gains[0], mods[0], seq, gi=0, shift_i=0, scale_i=1)
    for l in range(depth):
        mod, g = mods[l], gains[l]

        u = swiglu_in(h, ffn1_in[l].astype(BF16))
        y = matmul(u, ffn1_out[l].astype(BF16), out_dtype=F32)
        x, h = resid_normmod(x, y, g, mod, g, mod, seq, gi=1, gate_i=2, coef=FFN_RES,
                             gi2=2, shift_i=3, scale_i=4)

        wa, wb, wc, wg = _pack_w_in(w_in[l])
        aproj = matmul(h, wa, out_dtype=F32)
        bproj = matmul(h, wb, out_dtype=F32, tn=wb.shape[1] // 3)
        cproj = matmul(h, wc, out_dtype=F32, tn=wc.shape[1] // 2)
        gates = matmul(h, wg, out_dtype=BF16, act="sigmoid")
        ya = hgrn2(aproj, lbs[l], hgrn_norm[l].astype(F32), seq)
        kb, vb = kv_up(bproj, lat_block, kv_norm[l].astype(F32), w_kv_up[l].astype(BF16))
        yb = dsa(bproj, kb, vb, bias_b, seq)
        yc = swa(cproj, sinks[l], bias_c, seq)
        mix = merge(ya, yb, yc, w_branch[l].astype(BF16), gates)
        y = matmul(mix, w_out[l].astype(BF16), out_dtype=F32)
        x, h = resid_normmod(x, y, g, mod, g, mod, seq, gi=3, gate_i=5, coef=1.0,
                             gi2=4, shift_i=6, scale_i=7)

        u = swiglu_in(h, ffn2_in[l].astype(BF16))
        y = matmul(u, ffn2_out[l].astype(BF16), out_dtype=F32)
        if l + 1 < depth:
            x, h = resid_normmod(x, y, g, mod, gains[l + 1], mods[l + 1], seq, gi=5, gate_i=8,
                                 coef=FFN_RES, gi2=0, shift_i=0, scale_i=1)
        else:
            x = resid(x, y, g, mod, seq, gi=5, gate_i=8, coef=FFN_RES)
    return x.reshape(bsz, seq, d)
```

```python
import functools
import math

import jax
import jax.numpy as jnp
from jax import lax
from jax.experimental import pallas as pl
from jax.experimental.pallas import tpu as pltpu

A_HEADS = 8
A_DK = 128
A_DV = 128
A_WIDTH = A_HEADS * A_DK
B_HEADS = 8
B_DH = 128
B_KV_RANK = 512
IDX_HEADS = 16
IDX_DH = 64
TOPK_MAX = 256
C_HEADS = 16
C_KV_HEADS = 2
C_DH = 64
WINDOW = 128
REL_BUCKETS = 32
REL_MAX_DIST = 128
N_BRANCH = 3
BRANCH_W = 1024
FFN_RES = 0.5
N_MOD = 9
RMS_EPS = 1e-6
NEG_INF = -1e30
LOG2_E = math.log2(math.e)

VMEM_LIMIT_BYTES = 56 * 1024 * 1024

QBLK = 128
HGRN_CHUNK = 64
HGRN_SUB = 16
HGRN_HEADS_PER_STEP = 4
TOPK_GROUP = 4
B_SMALL_W = 128

BF16 = jnp.bfloat16
F32 = jnp.float32


def _cparams(n_grid_dims):
    return pltpu.CompilerParams(dimension_semantics=("arbitrary",) * n_grid_dims,
                                vmem_limit_bytes=VMEM_LIMIT_BYTES)


def _dot(a, b):
    return jnp.dot(a, b, preferred_element_type=F32)


def _dot_nt(a, b):
    return lax.dot_general(a, b, (((1,), (1,)), ((), ())), preferred_element_type=F32)


def _silu(x):
    return x * jax.nn.sigmoid(x)


def _lower_bound_kernel(x_ref, o_ref):
    x = x_ref[...]
    m = jnp.max(x, axis=0, keepdims=True)
    e = jnp.exp(x - m)
    p = e / jnp.sum(e, axis=0, keepdims=True)
    depth = x.shape[0]
    run = jnp.zeros_like(p[0:1])
    for l in range(depth):
        o_ref[l:l + 1, :] = run
        if l + 1 < depth:
            run = run + p[l + 1:l + 2]


def lower_bounds(lb_logits):
    return pl.pallas_call(
        _lower_bound_kernel,
        out_shape=jax.ShapeDtypeStruct(lb_logits.shape, F32),
        name="lower_bounds",
    )(lb_logits.astype(F32))


def _t5_bucket(dist):
    max_exact = REL_BUCKETS // 2
    d = jnp.maximum(dist, 0)
    df = jnp.maximum(d, 1).astype(F32)
    large = max_exact + (jnp.log(df / max_exact) / math.log(REL_MAX_DIST / max_exact)
                         * (REL_BUCKETS - max_exact)).astype(jnp.int32)
    large = jnp.minimum(large, REL_BUCKETS - 1)
    return jnp.where(d < max_exact, d, large)


def _bias_tile_kernel(tab_ref, o_ref, *, keys_on_rows):
    h = pl.program_id(0)
    row = lax.broadcasted_iota(jnp.int32, (QBLK, QBLK), 0)
    col = lax.broadcasted_iota(jnp.int32, (QBLK, QBLK), 1)
    t_minus_s = col - row if keys_on_rows else row - col
    for r in range(3):
        bucket = _t5_bucket(r * QBLK + t_minus_s)
        tile = jnp.zeros((QBLK, QBLK), F32)
        for b in range(REL_BUCKETS):
            tile = jnp.where(bucket == b, tab_ref[h, b], tile)
        if keys_on_rows:
            o_ref[r] = tile
        else:
            o_ref[r, 0] = tile


def bias_tiles(rel_table, *, keys_on_rows):
    n_heads = rel_table.shape[1]
    if keys_on_rows:
        out_spec = pl.BlockSpec((3, QBLK, QBLK), lambda h: (0, 0, h))
        out_shape = (3, QBLK, n_heads * QBLK)
    else:
        out_spec = pl.BlockSpec((3, 1, QBLK, QBLK), lambda h: (0, h, 0, 0))
        out_shape = (3, n_heads, QBLK, QBLK)
    return pl.pallas_call(
        functools.partial(_bias_tile_kernel, keys_on_rows=keys_on_rows),
        grid=(n_heads,),
        in_specs=[pl.BlockSpec(memory_space=pltpu.SMEM)],
        out_specs=out_spec,
        out_shape=jax.ShapeDtypeStruct(out_shape, F32),
        compiler_params=_cparams(1),
        name="bias_tiles",
    )(rel_table.T.astype(F32))


def _rms(x, g):
    return x * lax.rsqrt(jnp.mean(x * x, axis=-1, keepdims=True) + RMS_EPS) * g


def _modulate(x, g_ref, mod_ref, gi, shift_i, scale_i):
    y = _rms(x, g_ref[gi:gi + 1, :])
    return y * (1.0 + mod_ref[0, scale_i:scale_i + 1, :]) + mod_ref[0, shift_i:shift_i + 1, :]


def _normmod_kernel(x_ref, g_ref, mod_ref, h_ref, *, gi, shift_i, scale_i):
    h_ref[...] = _modulate(x_ref[...], g_ref, mod_ref, gi, shift_i, scale_i).astype(h_ref.dtype)


def normmod(x, gains, mod, seq, *, gi, shift_i, scale_i, tm=256):
    m, d = x.shape
    tpb = seq // tm
    return pl.pallas_call(
        functools.partial(_normmod_kernel, gi=gi, shift_i=shift_i, scale_i=scale_i),
        grid=(m // tm,),
        in_specs=[pl.BlockSpec((tm, d), lambda i: (i, 0)),
                  pl.BlockSpec(gains.shape, lambda i: (0, 0)),
                  pl.BlockSpec((1, N_MOD, d), lambda i: (i // tpb, 0, 0))],
        out_specs=pl.BlockSpec((tm, d), lambda i: (i, 0)),
        out_shape=jax.ShapeDtypeStruct((m, d), BF16),
        compiler_params=_cparams(1),
        name="normmod",
    )(x, gains, mod)


def _resid_kernel(x_ref, y_ref, g_ref, mod_ref, x_out_ref, *, gi, gate_i, coef):
    y = _rms(y_ref[...], g_ref[gi:gi + 1, :])
    x_out_ref[...] = x_ref[...] + coef * mod_ref[0, gate_i:gate_i + 1, :] * y


def resid(x, y, gains, mod, seq, *, gi, gate_i, coef, tm=256):
    m, d = x.shape
    tpb = seq // tm
    row = pl.BlockSpec((tm, d), lambda i: (i, 0))
    return pl.pallas_call(
        functools.partial(_resid_kernel, gi=gi, gate_i=gate_i, coef=coef),
        grid=(m // tm,),
        in_specs=[row, row,
                  pl.BlockSpec(gains.shape, lambda i: (0, 0)),
                  pl.BlockSpec((1, N_MOD, d), lambda i: (i // tpb, 0, 0))],
        out_specs=row,
        out_shape=jax.ShapeDtypeStruct((m, d), F32),
        compiler_params=_cparams(1),
        name="resid",
    )(x, y, gains, mod)


def _resid_normmod_kernel(x_ref, y_ref, g_ref, mod_ref, g2_ref, mod2_ref, x_out_ref, h_ref, *,
                          gi, gate_i, coef, gi2, shift_i, scale_i):
    y = _rms(y_ref[...], g_ref[gi:gi + 1, :])
    x_new = x_ref[...] + coef * mod_ref[0, gate_i:gate_i + 1, :] * y
    x_out_ref[...] = x_new
    h_ref[...] = _modulate(x_new, g2_ref, mod2_ref, gi2, shift_i, scale_i).astype(h_ref.dtype)


def resid_normmod(x, y, gains, mod, gains2, mod2, seq, *, gi, gate_i, coef, gi2, shift_i, scale_i, tm=256):
    m, d = x.shape
    tpb = seq // tm
    row = pl.BlockSpec((tm, d), lambda i: (i, 0))
    gspec = pl.BlockSpec(gains.shape, lambda i: (0, 0))
    mspec = pl.BlockSpec((1, N_MOD, d), lambda i: (i // tpb, 0, 0))
    return pl.pallas_call(
        functools.partial(_resid_normmod_kernel, gi=gi, gate_i=gate_i, coef=coef, gi2=gi2,
                          shift_i=shift_i, scale_i=scale_i),
        grid=(m // tm,),
        in_specs=[row, row, gspec, mspec, gspec, mspec],
        out_specs=[row, row],
        out_shape=[jax.ShapeDtypeStruct((m, d), F32), jax.ShapeDtypeStruct((m, d), BF16)],
        compiler_params=_cparams(1),
        name="resid_normmod",
    )(x, y, gains, mod, gains2, mod2)


def _mm_kernel(a_ref, w_ref, o_ref, *, act):
    acc = _dot(a_ref[...], w_ref[...])
    if act == "silu":
        acc = _silu(acc)
    o_ref[...] = acc.astype(o_ref.dtype)


def matmul(a, w, *, out_dtype, act=None, tm=1024, tn=512):
    m, k = a.shape
    n = w.shape[1]
    tm, tn = min(tm, m), min(tn, n)
    assert m % tm == 0 and n % tn == 0
    return pl.pallas_call(
        functools.partial(_mm_kernel, act=act),
        grid=(m // tm, n // tn),
        in_specs=[pl.BlockSpec((tm, k), lambda i, j: (i, 0)),
                  pl.BlockSpec((k, tn), lambda i, j: (0, j))],
        out_specs=pl.BlockSpec((tm, tn), lambda i, j: (i, j)),
        out_shape=jax.ShapeDtypeStruct((m, n), out_dtype),
        compiler_params=_cparams(2),
        name="matmul",
    )(a, w)


def _swiglu_kernel(a_ref, wu_ref, wv_ref, o_ref):
    a = a_ref[...]
    u = _dot(a, wu_ref[...])
    v = _dot(a, wv_ref[...])
    o_ref[...] = (_silu(u) * v).astype(o_ref.dtype)


def swiglu_in(a, w, *, tm=1024, tn=512):
    m, k = a.shape
    f = w.shape[1] // 2
    tm = min(tm, m)
    nf = f // tn
    return pl.pallas_call(
        _swiglu_kernel,
        grid=(m // tm, nf),
        in_specs=[pl.BlockSpec((tm, k), lambda i, j: (i, 0)),
                  pl.BlockSpec((k, tn), lambda i, j: (0, j)),
                  pl.BlockSpec((k, tn), lambda i, j: (0, j + nf))],
        out_specs=pl.BlockSpec((tm, tn), lambda i, j: (i, j)),
        out_shape=jax.ShapeDtypeStruct((m, f), BF16),
        compiler_params=_cparams(2),
        name="swiglu_in",
    )(a, w, w)


def _merge_kernel(h_ref, wga_ref, wgb_ref, wgc_ref, ya_ref, yb_ref, yc_ref, wb_ref, o_ref):
    h = h_ref[...]
    acc = jax.nn.sigmoid(_dot(h, wga_ref[...])) * _dot(ya_ref[...], wb_ref[0])
    acc = acc + jax.nn.sigmoid(_dot(h, wgb_ref[...])) * _dot(yb_ref[...], wb_ref[1])
    acc = acc + jax.nn.sigmoid(_dot(h, wgc_ref[...])) * _dot(yc_ref[...], wb_ref[2])
    o_ref[...] = acc.astype(o_ref.dtype)


def merge(h, w_gates, ya, yb, yc, w_branch, *, tm=1024, tn=256):
    m, k = h.shape
    kw = ya.shape[1]
    d = w_branch.shape[2]
    tm = min(tm, m)
    nd = d // tn
    y_spec = pl.BlockSpec((tm, kw), lambda i, j: (i, 0))
    return pl.pallas_call(
        _merge_kernel,
        grid=(m // tm, nd),
        in_specs=[pl.BlockSpec((tm, k), lambda i, j: (i, 0)),
                  pl.BlockSpec((k, tn), lambda i, j: (0, j)),
                  pl.BlockSpec((k, tn), lambda i, j: (0, j + nd)),
                  pl.BlockSpec((k, tn), lambda i, j: (0, j + 2 * nd)),
                  y_spec, y_spec, y_spec,
                  pl.BlockSpec((N_BRANCH, kw, tn), lambda i, j: (0, 0, j))],
        out_specs=pl.BlockSpec((tm, tn), lambda i, j: (i, j)),
        out_shape=jax.ShapeDtypeStruct((m, d), BF16),
        compiler_params=_cparams(2),
        name="merge",
    )(h, w_gates, w_gates, w_gates, ya, yb, yc, w_branch)


def _kvup_kernel(lat_ref, g_ref, w_ref, k_ref, vt_ref):
    h = _rms(lat_ref[...], g_ref[...]).astype(BF16)
    kv = _dot(h, w_ref[...])
    k_ref[...] = kv[:, :B_DH].astype(k_ref.dtype)
    v = kv[:, B_DH:]
    for u in range(vt_ref.shape[0]):
        vt_ref[u] = v[u * QBLK:(u + 1) * QBLK].T.astype(vt_ref.dtype)


def kv_up(bproj, lat_block, kv_norm_g, w_kv_up, *, tm=512):
    m = bproj.shape[0]
    return pl.pallas_call(
        _kvup_kernel,
        grid=(m // tm,),
        in_specs=[pl.BlockSpec((tm, B_KV_RANK), lambda i: (i, lat_block)),
                  pl.BlockSpec((1, B_KV_RANK), lambda i: (0, 0)),
                  pl.BlockSpec((B_KV_RANK, 2 * B_DH), lambda i: (0, 0))],
        out_specs=[pl.BlockSpec((tm, B_DH), lambda i: (i, 0)),
                   pl.BlockSpec((tm // QBLK, B_DH, QBLK), lambda i: (i, 0, 0))],
        out_shape=[jax.ShapeDtypeStruct((m, B_DH), BF16),
                   jax.ShapeDtypeStruct((m // QBLK, B_DH, QBLK), BF16)],
        compiler_params=_cparams(1),
        name="kv_up",
    )(bproj, kv_norm_g.reshape(1, B_KV_RANK), w_kv_up)


def _hgrn_chunk(q, fr, ir, gr, lb, ng, st, tril):
    cs, sub = HGRN_CHUNK, HGRN_SUB
    f = lb + (1.0 - lb) * jax.nn.sigmoid(fr)
    lf = jnp.log(jnp.maximum(f, 1e-20))
    k = 1.0 - f
    v = _silu(ir)
    b = jnp.dot(tril, lf, precision=lax.Precision.HIGHEST, preferred_element_type=F32) * LOG2_E
    o_inter = _dot_nt((q * jnp.exp2(b)).astype(BF16), st.astype(BF16))

    col_s = lax.broadcasted_iota(jnp.int32, (sub, cs), 1)
    blocks = []
    for si in range(cs // sub):
        lo = si * sub
        q_s = q[lo:lo + sub]
        b_s = b[lo:lo + sub]
        if si > 0:
            bref = b[lo - 1:lo]
            qt = (q_s * jnp.exp2(b_s - bref)).astype(BF16)
            kj = (k[:lo] * jnp.exp2(bref - b[:lo])).astype(BF16)
            attn = jnp.concatenate([_dot_nt(qt, kj), jnp.zeros((sub, cs - lo), F32)], axis=1)
        else:
            attn = jnp.zeros((sub, cs), F32)
        for s in range(sub):
            a = q_s * k[lo + s:lo + s + 1] * jnp.exp2(b_s - b[lo + s:lo + s + 1])
            attn = jnp.where(col_s == lo + s, jnp.sum(a, axis=-1, keepdims=True), attn)
        blocks.append(attn)
    attn = jnp.where(tril > 0.0, jnp.concatenate(blocks, axis=0), 0.0)
    o = o_inter + _dot(attn.astype(BF16), v.astype(BF16))

    b_end = b[cs - 1:cs]
    kd = (k * jnp.exp2(b_end - b)).astype(BF16)
    upd = _dot(v.T.astype(BF16), kd)
    st_new = st * jnp.exp2(b_end) + upd
    return _rms(o, ng) * _silu(gr), st_new


def _hgrn_kernel(q_ref, f_ref, i_ref, g_ref, lb_ref, ng_ref, o_ref, st_ref, *, seq):
    cs = HGRN_CHUNK
    st_ref[...] = jnp.zeros_like(st_ref)
    ng = ng_ref[...]
    row_c = lax.broadcasted_iota(jnp.int32, (cs, cs), 0)
    col_c = lax.broadcasted_iota(jnp.int32, (cs, cs), 1)
    tril = (col_c <= row_c).astype(F32)

    def chunk(c, carry):
        r0 = pl.multiple_of(c * cs, cs)
        for h in range(HGRN_HEADS_PER_STEP):
            cols = slice(h * A_DK, (h + 1) * A_DK)
            o, st_new = _hgrn_chunk(q_ref[pl.ds(r0, cs), cols], f_ref[pl.ds(r0, cs), cols],
                                    i_ref[pl.ds(r0, cs), cols], g_ref[pl.ds(r0, cs), cols],
                                    lb_ref[:, cols], ng, st_ref[h], tril)
            st_ref[h] = st_new
            o_ref[pl.ds(r0, cs), cols] = o.astype(o_ref.dtype)
        return carry

    lax.fori_loop(0, seq // cs, chunk, 0)


def hgrn2(aproj, lb, norm_g, seq):
    m = aproj.shape[0]
    bsz = m // seq
    nh = HGRN_HEADS_PER_STEP
    steps = A_HEADS // nh

    def col(part):
        return pl.BlockSpec((seq, nh * A_DK), lambda b, h: (b, part * steps + h))

    return pl.pallas_call(
        functools.partial(_hgrn_kernel, seq=seq),
        grid=(bsz, steps),
        in_specs=[col(0), col(1), col(2), col(3),
                  pl.BlockSpec((1, nh * A_DK), lambda b, h: (0, h)),
                  pl.BlockSpec((1, A_DV), lambda b, h: (0, 0))],
        out_specs=pl.BlockSpec((seq, nh * A_DV), lambda b, h: (b, h)),
        out_shape=jax.ShapeDtypeStruct((m, A_HEADS * A_DV), BF16),
        scratch_shapes=[pltpu.VMEM((nh, A_DV, A_DK), F32)],
        compiler_params=_cparams(2),
        name="hgrn2",
    )(aproj, aproj, aproj, aproj, lb.reshape(1, A_WIDTH), norm_g.reshape(1, A_DV))


def _dsa_kernel(q_ref, iq_ref, smq_ref, smk_ref, k_ref, vt_ref, bt_ref, o_ref,
                key_ref, msk_ref, lg_ref, mx_ref, ls_ref, acc_ref, *, nblk, topk):
    j = pl.program_id(1)
    row = lax.broadcasted_iota(jnp.int32, (QBLK, QBLK), 0)
    col = lax.broadcasted_iota(jnp.int32, (QBLK, QBLK), 1)
    causal = row <= col
    n_grp = j // TOPK_GROUP + 1
    n_pair = j // 2 + 1
    int_min = jnp.int32(-2 ** 31)

    def sort_key(score):
        bits = lax.bitcast_convert_type(score, jnp.int32)
        return bits ^ ((bits >> 31) & jnp.int32(0x7FFFFFFF))

    iq = iq_ref[...].astype(BF16)
    iqs = jnp.concatenate([iq[:, h * IDX_DH:(h + 1) * IDX_DH] for h in range(IDX_HEADS)], axis=0)
    iwt = (smq_ref[:, IDX_DH:IDX_DH + IDX_HEADS] * (IDX_HEADS ** -0.5 * IDX_DH ** -0.5)).T

    def visible(kb):
        return jnp.logical_or(kb < j, jnp.logical_and(kb == j, causal))

    def score_tile(kb):
        r0 = pl.multiple_of(kb * QBLK, QBLK)
        ik = smk_ref[pl.ds(r0, QBLK), 0:IDX_DH].astype(BF16)
        d = _dot_nt(ik, iqs)
        sc = jnp.zeros((QBLK, QBLK), F32)
        for h in range(IDX_HEADS):
            sc = sc + jnp.maximum(d[:, h * QBLK:(h + 1) * QBLK], 0.0) * iwt[h:h + 1, :]
        key_ref[kb] = sort_key(jnp.where(visible(kb), sc, NEG_INF))

    def pairwise(tile_fn):
        def step(kp, carry):
            tile_fn(2 * kp)
            tile_fn(2 * kp + 1)
            return carry
        lax.fori_loop(0, n_pair, step, 0)

    pairwise(score_tile)

    def fill_step(kb, carry):
        key_ref[kb] = sort_key(jnp.full((QBLK, QBLK), NEG_INF, F32))
        return carry

    lax.fori_loop(2 * n_pair, n_grp * TOPK_GROUP, fill_step, 0)

    kf = jnp.float32(topk)

    def count(pred):
        def grp(gi, c):
            for u in range(TOPK_GROUP):
                kb = gi * TOPK_GROUP + u
                c = c + jnp.where(pred(kb, key_ref[kb]), 1.0, 0.0)
            return c
        c = lax.fori_loop(0, n_grp, grp, jnp.zeros((QBLK, QBLK), F32))
        return jnp.sum(c, axis=0, keepdims=True)

    def thr_step(it, cand):
        trial = cand | lax.shift_left(jnp.int32(1), 31 - it)
        t_s = trial ^ int_min
        cnt = count(lambda kb, key: key >= t_s)
        return jnp.where(cnt >= kf, trial, cand)

    cand = lax.fori_loop(0, 32, thr_step, jnp.zeros((1, QBLK), jnp.int32))
    thr = cand ^ int_min
    n_gt = count(lambda kb, key: key > thr)
    n_ge = count(lambda kb, key: key >= thr)
    need = kf - n_gt
    nbits = (nblk * QBLK - 1).bit_length()

    def tie_search():
        def idx_step(it, pos):
            trial = pos + lax.shift_left(jnp.int32(1), nbits - 1 - it)
            cnt = count(lambda kb, key: jnp.logical_and(key == thr, kb * QBLK + row < trial))
            return jnp.where(cnt < need, trial, pos)
        return lax.fori_loop(0, nbits, idx_step, jnp.zeros((1, QBLK), jnp.int32))

    has_ties = jnp.max(n_ge) > kf
    pos = lax.cond(has_ties, tie_search, lambda: jnp.full((1, QBLK), nblk * QBLK, jnp.int32))

    def mask_tile(kb):
        key = key_ref[kb]
        sel = jnp.logical_or(key > thr, jnp.logical_and(key == thr, kb * QBLK + row <= pos))
        msk_ref[kb] = jnp.where(jnp.logical_and(sel, visible(kb)), 0.0, NEG_INF)

    pairwise(mask_tile)

    q = q_ref[...].astype(BF16)
    qs = jnp.concatenate([q[:, h * B_DH:(h + 1) * B_DH] for h in range(B_HEADS)], axis=0)
    scale = B_DH ** -0.5
    mx_ref[...] = jnp.full(mx_ref.shape, NEG_INF, F32)

    def logit_tile(kb):
        r0 = pl.multiple_of(kb * QBLK, QBLK)
        lg = _dot_nt(k_ref[pl.ds(r0, QBLK), :], qs)
        msk = msk_ref[kb]
        lg = lg * scale + bt_ref[jnp.clip(j - kb, 0, 2)] + jnp.concatenate([msk] * B_HEADS, axis=1)
        lg_ref[kb] = lg
        mx_ref[...] = jnp.maximum(mx_ref[...], lg)

    pairwise(logit_tile)
    mx = jnp.max(mx_ref[...], axis=0, keepdims=True)
    ls_ref[...] = jnp.zeros_like(ls_ref)
    acc_ref[...] = jnp.zeros_like(acc_ref)

    def pv_tile(kb):
        p = jnp.exp(lg_ref[kb] - mx)
        ls_ref[...] += p
        acc_ref[...] += _dot(vt_ref[kb], p.astype(BF16))

    pairwise(pv_tile)
    out_t = acc_ref[...] / jnp.sum(ls_ref[...], axis=0, keepdims=True)
    for h in range(B_HEADS):
        o_ref[:, h * B_DH:(h + 1) * B_DH] = out_t[:, h * QBLK:(h + 1) * QBLK].T.astype(o_ref.dtype)


def dsa(bproj, k, vt, bias_b, seq):
    m = bproj.shape[0]
    bsz = m // seq
    nblk = seq // QBLK
    assert nblk % TOPK_GROUP == 0
    topk = min(TOPK_MAX, seq // 4)
    qw = B_HEADS * B_DH
    small_blk = (2 * qw + B_KV_RANK) // B_SMALL_W
    cols = B_HEADS * QBLK
    return pl.pallas_call(
        functools.partial(_dsa_kernel, nblk=nblk, topk=topk),
        grid=(bsz, nblk),
        in_specs=[pl.BlockSpec((QBLK, qw), lambda b, j: (b * nblk + j, 0)),
                  pl.BlockSpec((QBLK, qw), lambda b, j: (b * nblk + j, 1)),
                  pl.BlockSpec((QBLK, B_SMALL_W), lambda b, j: (b * nblk + j, small_blk)),
                  pl.BlockSpec((seq, B_SMALL_W), lambda b, j: (b, small_blk)),
                  pl.BlockSpec((seq, B_DH), lambda b, j: (b, 0)),
                  pl.BlockSpec((nblk, B_DH, QBLK), lambda b, j: (b, 0, 0)),
                  pl.BlockSpec((3, QBLK, cols), lambda b, j: (0, 0, 0))],
        out_specs=pl.BlockSpec((QBLK, qw), lambda b, j: (b * nblk + j, 0)),
        out_shape=jax.ShapeDtypeStruct((m, qw), BF16),
        scratch_shapes=[pltpu.VMEM((nblk, QBLK, QBLK), jnp.int32),
                        pltpu.VMEM((nblk, QBLK, QBLK), F32),
                        pltpu.VMEM((nblk, QBLK, cols), F32),
                        pltpu.VMEM((QBLK, cols), F32),
                        pltpu.VMEM((QBLK, cols), F32),
                        pltpu.VMEM((B_DH, cols), F32)],
        compiler_params=_cparams(2),
        name="dsa",
    )(bproj, bproj, bproj, bproj, k, vt, bias_b)


def _swa_kernel(sink_ref, q_ref, kc_ref, vc_ref, kp_ref, vp_ref, bt_ref, o_ref):
    j = pl.program_id(1)
    grp = C_HEADS // C_KV_HEADS
    row = lax.broadcasted_iota(jnp.int32, (QBLK, QBLK), 0)
    col = lax.broadcasted_iota(jnp.int32, (QBLK, QBLK), 1)
    cur_ok = jnp.concatenate([row <= col] * grp, axis=1)
    prev_ok = jnp.concatenate([jnp.logical_and(row > col, j > 0)] * grp, axis=1)
    q = q_ref[...].astype(BF16)
    scale = C_DH ** -0.5
    for g in range(C_KV_HEADS):
        heads = range(g * grp, (g + 1) * grp)
        kv_cols = slice(g * C_DH, (g + 1) * C_DH)
        lanes = slice(g * grp * QBLK, (g + 1) * grp * QBLK)
        qs = jnp.concatenate([q[:, h * C_DH:(h + 1) * C_DH] for h in heads], axis=0)
        sink = jnp.concatenate([jnp.full((1, QBLK), sink_ref[h], F32) for h in heads], axis=1)
        lc = _dot_nt(kc_ref[:, kv_cols].astype(BF16), qs)
        lp = _dot_nt(kp_ref[:, kv_cols].astype(BF16), qs)
        lc = jnp.where(cur_ok, lc * scale + bt_ref[0, :, lanes], NEG_INF)
        lp = jnp.where(prev_ok, lp * scale + bt_ref[1, :, lanes], NEG_INF)
        mx = jnp.maximum(jnp.maximum(jnp.max(lc, axis=0, keepdims=True),
                                     jnp.max(lp, axis=0, keepdims=True)), sink)
        pc = jnp.exp(lc - mx)
        pp = jnp.exp(lp - mx)
        den = (jnp.sum(pc, axis=0, keepdims=True) + jnp.sum(pp, axis=0, keepdims=True)
               + jnp.exp(sink - mx))
        o_t = (_dot(vc_ref[:, kv_cols].T.astype(BF16), pc.astype(BF16))
               + _dot(vp_ref[:, kv_cols].T.astype(BF16), pp.astype(BF16))) / den
        for u in range(0, grp, 2):
            pair = jnp.concatenate([o_t[:, u * QBLK:(u + 1) * QBLK],
                                    o_t[:, (u + 1) * QBLK:(u + 2) * QBLK]], axis=0)
            h0 = g * grp + u
            o_ref[:, h0 * C_DH:(h0 + 2) * C_DH] = pair.T.astype(o_ref.dtype)


def swa(cproj, sinks, bias_c, seq):
    m = cproj.shape[0]
    bsz = m // seq
    nblk = seq // QBLK
    qw = C_HEADS * C_DH
    kvw = C_KV_HEADS * C_DH
    kblk = qw // kvw

    def cur(off):
        return pl.BlockSpec((QBLK, kvw), lambda b, j: (b * nblk + j, kblk + off))

    def prev(off):
        return pl.BlockSpec((QBLK, kvw), lambda b, j: (b * nblk + jnp.maximum(j - 1, 0), kblk + off))

    return pl.pallas_call(
        _swa_kernel,
        grid=(bsz, nblk),
        in_specs=[pl.BlockSpec(memory_space=pltpu.SMEM),
                  pl.BlockSpec((QBLK, qw), lambda b, j: (b * nblk + j, 0)),
                  cur(0), cur(1), prev(0), prev(1),
                  pl.BlockSpec((3, QBLK, C_HEADS * QBLK), lambda b, j: (0, 0, 0))],
        out_specs=pl.BlockSpec((QBLK, qw), lambda b, j: (b * nblk + j, 0)),
        out_shape=jax.ShapeDtypeStruct((m, qw), BF16),
        compiler_params=_cparams(2),
        name="swa",
    )(sinks.astype(F32), cproj, cproj, cproj, cproj, cproj, bias_c)


def _pack_w_in(w):
    w = w.astype(BF16)
    o = 0
    parts = {}
    for name, width in (("a", 4 * A_WIDTH), ("bq", B_HEADS * B_DH), ("blat", B_KV_RANK),
                        ("biq", IDX_HEADS * IDX_DH), ("bik", IDX_DH), ("biw", IDX_HEADS),
                        ("c", (C_HEADS + 2 * C_KV_HEADS) * C_DH)):
        parts[name] = w[:, o:o + width]
        o += width
    gates = w[:, o:]
    pad = jnp.zeros((w.shape[0], B_SMALL_W - IDX_DH - IDX_HEADS), w.dtype)
    wb = jnp.concatenate([parts["bq"], parts["biq"], parts["blat"], parts["bik"], parts["biw"], pad], axis=1)
    return parts["a"], wb, parts["c"], gates


def kernel(x, c, w_c_down, w_c_up, norm_gains, w_in, lb_logits, hgrn_norm, kv_norm, w_kv_up, rel_table,
           sinks, w_branch, w_out, ffn1_in, ffn1_out, ffn2_in, ffn2_out):
    bsz, seq, d = x.shape
    depth = w_in.shape[0]
    m = bsz * seq
    x = x.reshape(m, d).astype(F32)

    cond = matmul(c.astype(BF16), w_c_down.astype(BF16), out_dtype=BF16, act="silu")
    lbs = lower_bounds(lb_logits)
    bias_b = bias_tiles(rel_table[:, :B_HEADS], keys_on_rows=True)
    bias_c = bias_tiles(rel_table[:, B_HEADS:], keys_on_rows=True)
    lat_block = 2 * B_HEADS * B_DH // B_KV_RANK
    mods = [matmul(cond, w_c_up[l].astype(BF16), out_dtype=F32, tn=4096).reshape(bsz, N_MOD, d)
            for l in range(depth)]
    gains = norm_gains.astype(F32)

    h = normmod(x, gains[0], mods[0], seq, gi=0, shift_i=0, scale_i=1)
    for l in range(depth):
        mod, g = mods[l], gains[l]

        u = swiglu_in(h, ffn1_in[l].astype(BF16))
        y = matmul(u, ffn1_out[l].astype(BF16), out_dtype=F32)
        x, h = resid_normmod(x, y, g, mod, g, mod, seq, gi=1, gate_i=2, coef=FFN_RES,
                             gi2=2, shift_i=3, scale_i=4)

        wa, wb, wc, wg = _pack_w_in(w_in[l])
        aproj = matmul(h, wa, out_dtype=F32)
        bproj = matmul(h, wb, out_dtype=F32, tn=wb.shape[1] // 3)
        cproj = matmul(h, wc, out_dtype=F32, tn=wc.shape[1] // 2)
        ya = hgrn2(aproj, lbs[l], hgrn_norm[l].astype(F32), seq)
        kb, vb = kv_up(bproj, lat_block, kv_norm[l].astype(F32), w_kv_up[l].astype(BF16))
        yb = dsa(bproj, kb, vb, bias_b, seq)
        yc = swa(cproj, sinks[l], bias_c, seq)
        mix = merge(h, wg, ya, yb, yc, w_branch[l].astype(BF16))
        y = matmul(mix, w_out[l].astype(BF16), out_dtype=F32)
        x, h = resid_normmod(x, y, g, mod, g, mod, seq, gi=3, gate_i=5, coef=1.0,
                             gi2=4, shift_i=6, scale_i=7)

        u = swiglu_in(h, ffn2_in[l].astype(BF16))
        y = matmul(u, ffn2_out[l].astype(BF16), out_dtype=F32)
        if l + 1 < depth:
            x, h = resid_normmod(x, y, g, mod, gains[l + 1], mods[l + 1], seq, gi=5, gate_i=8,
                                 coef=FFN_RES, gi2=0, shift_i=0, scale_i=1)
        else:
            x = resid(x, y, g, mod, seq, gi=5, gate_i=8, coef=FFN_RES)
    return x.reshape(bsz, seq, d)
```

```python
import functools
import math

import jax
import jax.numpy as jnp
from jax import lax
from jax.experimental import pallas as pl
from jax.experimental.pallas import tpu as pltpu

A_HEADS = 8
A_DK = 128
A_DV = 128
A_WIDTH = A_HEADS * A_DK
B_HEADS = 8
B_DH = 128
B_KV_RANK = 512
IDX_HEADS = 16
IDX_DH = 64
TOPK_MAX = 256
C_HEADS = 16
C_KV_HEADS = 2
C_DH = 64
WINDOW = 128
REL_BUCKETS = 32
REL_MAX_DIST = 128
N_BRANCH = 3
BRANCH_W = 1024
FFN_RES = 0.5
N_MOD = 9
RMS_EPS = 1e-6
NEG_INF = -1e30
LOG2_E = math.log2(math.e)

VMEM_LIMIT_BYTES = 56 * 1024 * 1024

QBLK = 128
HGRN_CHUNK = 64
HGRN_SUB = 16
HGRN_HEADS_PER_STEP = 4
TOPK_GROUP = 4
B_SMALL_W = 128

BF16 = jnp.bfloat16
F32 = jnp.float32


def _cparams(n_grid_dims):
    return pltpu.CompilerParams(dimension_semantics=("arbitrary",) * n_grid_dims,
                                vmem_limit_bytes=VMEM_LIMIT_BYTES)


def _dot(a, b):
    return jnp.dot(a, b, preferred_element_type=F32)


def _dot_nt(a, b):
    return lax.dot_general(a, b, (((1,), (1,)), ((), ())), preferred_element_type=F32)


def _silu(x):
    return x * jax.nn.sigmoid(x)


def _lower_bound_kernel(x_ref, o_ref):
    x = x_ref[...]
    m = jnp.max(x, axis=0, keepdims=True)
    e = jnp.exp(x - m)
    p = e / jnp.sum(e, axis=0, keepdims=True)
    depth = x.shape[0]
    run = jnp.zeros_like(p[0:1])
    for l in range(depth):
        o_ref[l:l + 1, :] = run
        if l + 1 < depth:
            run = run + p[l + 1:l + 2]


def lower_bounds(lb_logits):
    return pl.pallas_call(
        _lower_bound_kernel,
        out_shape=jax.ShapeDtypeStruct(lb_logits.shape, F32),
        name="lower_bounds",
    )(lb_logits.astype(F32))


def _t5_bucket(dist):
    max_exact = REL_BUCKETS // 2
    d = jnp.maximum(dist, 0)
    df = jnp.maximum(d, 1).astype(F32)
    large = max_exact + (jnp.log(df / max_exact) / math.log(REL_MAX_DIST / max_exact)
                         * (REL_BUCKETS - max_exact)).astype(jnp.int32)
    large = jnp.minimum(large, REL_BUCKETS - 1)
    return jnp.where(d < max_exact, d, large)


def _bias_tile_kernel(tab_ref, o_ref, *, keys_on_rows):
    h = pl.program_id(0)
    row = lax.broadcasted_iota(jnp.int32, (QBLK, QBLK), 0)
    col = lax.broadcasted_iota(jnp.int32, (QBLK, QBLK), 1)
    t_minus_s = col - row if keys_on_rows else row - col
    for r in range(3):
        bucket = _t5_bucket(r * QBLK + t_minus_s)
        tile = jnp.zeros((QBLK, QBLK), F32)
        for b in range(REL_BUCKETS):
            tile = jnp.where(bucket == b, tab_ref[h, b], tile)
        if keys_on_rows:
            o_ref[r] = tile
        else:
            o_ref[r, 0] = tile


def bias_tiles(rel_table, *, keys_on_rows):
    n_heads = rel_table.shape[1]
    if keys_on_rows:
        out_spec = pl.BlockSpec((3, QBLK, QBLK), lambda h: (0, 0, h))
        out_shape = (3, QBLK, n_heads * QBLK)
    else:
        out_spec = pl.BlockSpec((3, 1, QBLK, QBLK), lambda h: (0, h, 0, 0))
        out_shape = (3, n_heads, QBLK, QBLK)
    return pl.pallas_call(
        functools.partial(_bias_tile_kernel, keys_on_rows=keys_on_rows),
        grid=(n_heads,),
        in_specs=[pl.BlockSpec(memory_space=pltpu.SMEM)],
        out_specs=out_spec,
        out_shape=jax.ShapeDtypeStruct(out_shape, F32),
        compiler_params=_cparams(1),
        name="bias_tiles",
    )(rel_table.T.astype(F32))


def _rms(x, g):
    return x * lax.rsqrt(jnp.mean(x * x, axis=-1, keepdims=True) + RMS_EPS) * g


def _modulate(x, g_ref, mod_ref, gi, shift_i, scale_i):
    y = _rms(x, g_ref[gi:gi + 1, :])
    return y * (1.0 + mod_ref[0, scale_i:scale_i + 1, :]) + mod_ref[0, shift_i:shift_i + 1, :]


def _normmod_kernel(x_ref, g_ref, mod_ref, h_ref, *, gi, shift_i, scale_i):
    h_ref[...] = _modulate(x_ref[...], g_ref, mod_ref, gi, shift_i, scale_i).astype(h_ref.dtype)


def normmod(x, gains, mod, seq, *, gi, shift_i, scale_i, tm=256):
    m, d = x.shape
    tpb = seq // tm
    return pl.pallas_call(
        functools.partial(_normmod_kernel, gi=gi, shift_i=shift_i, scale_i=scale_i),
        grid=(m // tm,),
        in_specs=[pl.BlockSpec((tm, d), lambda i: (i, 0)),
                  pl.BlockSpec(gains.shape, lambda i: (0, 0)),
                  pl.BlockSpec((1, N_MOD, d), lambda i: (i // tpb, 0, 0))],
        out_specs=pl.BlockSpec((tm, d), lambda i: (i, 0)),
        out_shape=jax.ShapeDtypeStruct((m, d), BF16),
        compiler_params=_cparams(1),
        name="normmod",
    )(x, gains, mod)


def _resid_kernel(x_ref, y_ref, g_ref, mod_ref, x_out_ref, *, gi, gate_i, coef):
    y = _rms(y_ref[...].astype(F32), g_ref[gi:gi + 1, :])
    x_out_ref[...] = x_ref[...] + coef * mod_ref[0, gate_i:gate_i + 1, :] * y


def resid(x, y, gains, mod, seq, *, gi, gate_i, coef, tm=256):
    m, d = x.shape
    tpb = seq // tm
    row = pl.BlockSpec((tm, d), lambda i: (i, 0))
    return pl.pallas_call(
        functools.partial(_resid_kernel, gi=gi, gate_i=gate_i, coef=coef),
        grid=(m // tm,),
        in_specs=[row, row,
                  pl.BlockSpec(gains.shape, lambda i: (0, 0)),
                  pl.BlockSpec((1, N_MOD, d), lambda i: (i // tpb, 0, 0))],
        out_specs=row,
        out_shape=jax.ShapeDtypeStruct((m, d), F32),
        compiler_params=_cparams(1),
        name="resid",
    )(x, y, gains, mod)


def _resid_normmod_kernel(x_ref, y_ref, g_ref, mod_ref, g2_ref, mod2_ref, x_out_ref, h_ref, *,
                          gi, gate_i, coef, gi2, shift_i, scale_i):
    y = _rms(y_ref[...].astype(F32), g_ref[gi:gi + 1, :])
    x_new = x_ref[...] + coef * mod_ref[0, gate_i:gate_i + 1, :] * y
    x_out_ref[...] = x_new
    h_ref[...] = _modulate(x_new, g2_ref, mod2_ref, gi2, shift_i, scale_i).astype(h_ref.dtype)


def resid_normmod(x, y, gains, mod, gains2, mod2, seq, *, gi, gate_i, coef, gi2, shift_i, scale_i, tm=256):
    m, d = x.shape
    tpb = seq // tm
    row = pl.BlockSpec((tm, d), lambda i: (i, 0))
    gspec = pl.BlockSpec(gains.shape, lambda i: (0, 0))
    mspec = pl.BlockSpec((1, N_MOD, d), lambda i: (i // tpb, 0, 0))
    return pl.pallas_call(
        functools.partial(_resid_normmod_kernel, gi=gi, gate_i=gate_i, coef=coef, gi2=gi2,
                          shift_i=shift_i, scale_i=scale_i),
        grid=(m // tm,),
        in_specs=[row, row, gspec, mspec, gspec, mspec],
        out_specs=[row, row],
        out_shape=[jax.ShapeDtypeStruct((m, d), F32), jax.ShapeDtypeStruct((m, d), BF16)],
        compiler_params=_cparams(1),
        name="resid_normmod",
    )(x, y, gains, mod, gains2, mod2)


def _mm_kernel(a_ref, w_ref, o_ref, *, act):
    acc = _dot(a_ref[...], w_ref[...])
    if act == "silu":
        acc = _silu(acc)
    o_ref[...] = acc.astype(o_ref.dtype)


def matmul(a, w, *, out_dtype, act=None, tm=1024, tn=512):
    m, k = a.shape
    n = w.shape[1]
    tm, tn = min(tm, m), min(tn, n)
    assert m % tm == 0 and n % tn == 0
    return pl.pallas_call(
        functools.partial(_mm_kernel, act=act),
        grid=(m // tm, n // tn),
        in_specs=[pl.BlockSpec((tm, k), lambda i, j: (i, 0)),
                  pl.BlockSpec((k, tn), lambda i, j: (0, j))],
        out_specs=pl.BlockSpec((tm, tn), lambda i, j: (i, j)),
        out_shape=jax.ShapeDtypeStruct((m, n), out_dtype),
        compiler_params=_cparams(2),
        name="matmul",
    )(a, w)


def _swiglu_kernel(a_ref, wu_ref, wv_ref, o_ref):
    a = a_ref[...]
    u = _dot(a, wu_ref[...])
    v = _dot(a, wv_ref[...])
    o_ref[...] = (_silu(u) * v).astype(o_ref.dtype)


def swiglu_in(a, w, *, tm=1024, tn=512):
    m, k = a.shape
    f = w.shape[1] // 2
    tm = min(tm, m)
    nf = f // tn
    return pl.pallas_call(
        _swiglu_kernel,
        grid=(m // tm, nf),
        in_specs=[pl.BlockSpec((tm, k), lambda i, j: (i, 0)),
                  pl.BlockSpec((k, tn), lambda i, j: (0, j)),
                  pl.BlockSpec((k, tn), lambda i, j: (0, j + nf))],
        out_specs=pl.BlockSpec((tm, tn), lambda i, j: (i, j)),
        out_shape=jax.ShapeDtypeStruct((m, f), BF16),
        compiler_params=_cparams(2),
        name="swiglu_in",
    )(a, w, w)


def _merge_kernel(h_ref, wga_ref, wgb_ref, wgc_ref, ya_ref, yb_ref, yc_ref, wb_ref, o_ref):
    h = h_ref[...]
    acc = jax.nn.sigmoid(_dot(h, wga_ref[...])) * _dot(ya_ref[...], wb_ref[0])
    acc = acc + jax.nn.sigmoid(_dot(h, wgb_ref[...])) * _dot(yb_ref[...], wb_ref[1])
    acc = acc + jax.nn.sigmoid(_dot(h, wgc_ref[...])) * _dot(yc_ref[...], wb_ref[2])
    o_ref[...] = acc.astype(o_ref.dtype)


def merge(h, w_gates, ya, yb, yc, w_branch, *, tm=1024, tn=256):
    m, k = h.shape
    kw = ya.shape[1]
    d = w_branch.shape[2]
    tm = min(tm, m)
    nd = d // tn
    y_spec = pl.BlockSpec((tm, kw), lambda i, j: (i, 0))
    return pl.pallas_call(
        _merge_kernel,
        grid=(m // tm, nd),
        in_specs=[pl.BlockSpec((tm, k), lambda i, j: (i, 0)),
                  pl.BlockSpec((k, tn), lambda i, j: (0, j)),
                  pl.BlockSpec((k, tn), lambda i, j: (0, j + nd)),
                  pl.BlockSpec((k, tn), lambda i, j: (0, j + 2 * nd)),
                  y_spec, y_spec, y_spec,
                  pl.BlockSpec((N_BRANCH, kw, tn), lambda i, j: (0, 0, j))],
        out_specs=pl.BlockSpec((tm, tn), lambda i, j: (i, j)),
        out_shape=jax.ShapeDtypeStruct((m, d), BF16),
        compiler_params=_cparams(2),
        name="merge",
    )(h, w_gates, w_gates, w_gates, ya, yb, yc, w_branch)


def _kvup_kernel(lat_ref, g_ref, w_ref, k_ref, vt_ref):
    h = _rms(lat_ref[...], g_ref[...]).astype(BF16)
    kv = _dot(h, w_ref[...])
    k_ref[...] = kv[:, :B_DH].astype(k_ref.dtype)
    v = kv[:, B_DH:]
    for u in range(vt_ref.shape[0]):
        vt_ref[u] = v[u * QBLK:(u + 1) * QBLK].T.astype(vt_ref.dtype)


def kv_up(bproj, lat_block, kv_norm_g, w_kv_up, *, tm=512):
    m = bproj.shape[0]
    return pl.pallas_call(
        _kvup_kernel,
        grid=(m // tm,),
        in_specs=[pl.BlockSpec((tm, B_KV_RANK), lambda i: (i, lat_block)),
                  pl.BlockSpec((1, B_KV_RANK), lambda i: (0, 0)),
                  pl.BlockSpec((B_KV_RANK, 2 * B_DH), lambda i: (0, 0))],
        out_specs=[pl.BlockSpec((tm, B_DH), lambda i: (i, 0)),
                   pl.BlockSpec((tm // QBLK, B_DH, QBLK), lambda i: (i, 0, 0))],
        out_shape=[jax.ShapeDtypeStruct((m, B_DH), BF16),
                   jax.ShapeDtypeStruct((m // QBLK, B_DH, QBLK), BF16)],
        compiler_params=_cparams(1),
        name="kv_up",
    )(bproj, kv_norm_g.reshape(1, B_KV_RANK), w_kv_up)


def _hgrn_chunk(q, fr, ir, gr, lb, ng, st, tril):
    cs, sub = HGRN_CHUNK, HGRN_SUB
    f = lb + (1.0 - lb) * jax.nn.sigmoid(fr)
    lf = jnp.log(jnp.maximum(f, 1e-20))
    k = 1.0 - f
    v = _silu(ir)
    b = jnp.dot(tril, lf, precision=lax.Precision.HIGHEST, preferred_element_type=F32) * LOG2_E
    o_inter = _dot_nt((q * jnp.exp2(b)).astype(BF16), st.astype(BF16))

    col_s = lax.broadcasted_iota(jnp.int32, (sub, cs), 1)
    blocks = []
    for si in range(cs // sub):
        lo = si * sub
        q_s = q[lo:lo + sub]
        b_s = b[lo:lo + sub]
        if si > 0:
            bref = b[lo - 1:lo]
            qt = (q_s * jnp.exp2(b_s - bref)).astype(BF16)
            kj = (k[:lo] * jnp.exp2(bref - b[:lo])).astype(BF16)
            attn = jnp.concatenate([_dot_nt(qt, kj), jnp.zeros((sub, cs - lo), F32)], axis=1)
        else:
            attn = jnp.zeros((sub, cs), F32)
        for s in range(sub):
            a = q_s * k[lo + s:lo + s + 1] * jnp.exp2(b_s - b[lo + s:lo + s + 1])
            attn = jnp.where(col_s == lo + s, jnp.sum(a, axis=-1, keepdims=True), attn)
        blocks.append(attn)
    attn = jnp.where(tril > 0.0, jnp.concatenate(blocks, axis=0), 0.0)
    o = o_inter + _dot(attn.astype(BF16), v.astype(BF16))

    b_end = b[cs - 1:cs]
    kd = (k * jnp.exp2(b_end - b)).astype(BF16)
    upd = _dot(v.T.astype(BF16), kd)
    st_new = st * jnp.exp2(b_end) + upd
    return _rms(o, ng) * _silu(gr), st_new


def _hgrn_kernel(q_ref, f_ref, i_ref, g_ref, lb_ref, ng_ref, o_ref, st_ref, *, seq):
    cs = HGRN_CHUNK
    st_ref[...] = jnp.zeros_like(st_ref)
    ng = ng_ref[...]
    row_c = lax.broadcasted_iota(jnp.int32, (cs, cs), 0)
    col_c = lax.broadcasted_iota(jnp.int32, (cs, cs), 1)
    tril = (col_c <= row_c).astype(F32)

    def chunk(c, carry):
        r0 = pl.multiple_of(c * cs, cs)
        for h in range(HGRN_HEADS_PER_STEP):
            cols = slice(h * A_DK, (h + 1) * A_DK)
            o, st_new = _hgrn_chunk(q_ref[pl.ds(r0, cs), cols], f_ref[pl.ds(r0, cs), cols],
                                    i_ref[pl.ds(r0, cs), cols], g_ref[pl.ds(r0, cs), cols],
                                    lb_ref[:, cols], ng, st_ref[h], tril)
            st_ref[h] = st_new
            o_ref[pl.ds(r0, cs), cols] = o.astype(o_ref.dtype)
        return carry

    lax.fori_loop(0, seq // cs, chunk, 0, unroll=2)


def hgrn2(aproj, lb, norm_g, seq):
    m = aproj.shape[0]
    bsz = m // seq
    nh = HGRN_HEADS_PER_STEP
    steps = A_HEADS // nh

    def col(part):
        return pl.BlockSpec((seq, nh * A_DK), lambda b, h: (b, part * steps + h))

    return pl.pallas_call(
        functools.partial(_hgrn_kernel, seq=seq),
        grid=(bsz, steps),
        in_specs=[col(0), col(1), col(2), col(3),
                  pl.BlockSpec((1, nh * A_DK), lambda b, h: (0, h)),
                  pl.BlockSpec((1, A_DV), lambda b, h: (0, 0))],
        out_specs=pl.BlockSpec((seq, nh * A_DV), lambda b, h: (b, h)),
        out_shape=jax.ShapeDtypeStruct((m, A_HEADS * A_DV), BF16),
        scratch_shapes=[pltpu.VMEM((nh, A_DV, A_DK), F32)],
        compiler_params=_cparams(2),
        name="hgrn2",
    )(aproj, aproj, aproj, aproj, lb.reshape(1, A_WIDTH), norm_g.reshape(1, A_DV))


def _dsa_kernel(q_ref, iq_ref, smq_ref, smk_ref, k_ref, vt_ref, bt_ref, o_ref,
                key_ref, msk_ref, lg_ref, mx_ref, ls_ref, acc_ref, *, nblk, topk):
    j = pl.program_id(1)
    row = lax.broadcasted_iota(jnp.int32, (QBLK, QBLK), 0)
    col = lax.broadcasted_iota(jnp.int32, (QBLK, QBLK), 1)
    causal = row <= col
    n_grp = j // TOPK_GROUP + 1
    n_pair = j // 2 + 1
    int_min = jnp.int32(-2 ** 31)

    def sort_key(score):
        bits = lax.bitcast_convert_type(score, jnp.int32)
        return bits ^ ((bits >> 31) & jnp.int32(0x7FFFFFFF))

    iq = iq_ref[...].astype(BF16)
    iqs = jnp.concatenate([iq[:, h * IDX_DH:(h + 1) * IDX_DH] for h in range(IDX_HEADS)], axis=0)
    iwt = (smq_ref[:, IDX_DH:IDX_DH + IDX_HEADS] * (IDX_HEADS ** -0.5 * IDX_DH ** -0.5)).T

    def visible(kb):
        return jnp.logical_or(kb < j, jnp.logical_and(kb == j, causal))

    def score_tile(kb):
        r0 = pl.multiple_of(kb * QBLK, QBLK)
        ik = smk_ref[pl.ds(r0, QBLK), 0:IDX_DH].astype(BF16)
        d = _dot_nt(ik, iqs)
        sc = jnp.zeros((QBLK, QBLK), F32)
        for h in range(IDX_HEADS):
            sc = sc + jnp.maximum(d[:, h * QBLK:(h + 1) * QBLK], 0.0) * iwt[h:h + 1, :]
        key_ref[kb] = sort_key(jnp.where(visible(kb), sc, NEG_INF))

    def pairwise(tile_fn):
        def step(kp, carry):
            tile_fn(2 * kp)
            tile_fn(2 * kp + 1)
            return carry
        lax.fori_loop(0, n_pair, step, 0)

    def groupwise(tile_fn):
        def step(gi, carry):
            for u in range(TOPK_GROUP):
                tile_fn(gi * TOPK_GROUP + u)
            return carry
        lax.fori_loop(0, n_grp, step, 0)

    pairwise(score_tile)

    def fill_step(kb, carry):
        key_ref[kb] = sort_key(jnp.full((QBLK, QBLK), NEG_INF, F32))
        return carry

    lax.fori_loop(2 * n_pair, n_grp * TOPK_GROUP, fill_step, 0)

    kf = jnp.float32(topk)

    def count(pred):
        def grp(gi, c):
            for u in range(TOPK_GROUP):
                kb = gi * TOPK_GROUP + u
                c = c + jnp.where(pred(kb, key_ref[kb]), 1.0, 0.0)
            return c
        c = lax.fori_loop(0, n_grp, grp, jnp.zeros((QBLK, QBLK), F32))
        return jnp.sum(c, axis=0, keepdims=True)

    def thr_step(it, cand):
        trial = cand | lax.shift_left(jnp.int32(1), 31 - it)
        t_s = trial ^ int_min
        cnt = count(lambda kb, key: key >= t_s)
        return jnp.where(cnt >= kf, trial, cand)

    cand = lax.fori_loop(0, 32, thr_step, jnp.zeros((1, QBLK), jnp.int32))
    thr = cand ^ int_min
    n_gt = count(lambda kb, key: key > thr)
    n_ge = count(lambda kb, key: key >= thr)
    need = kf - n_gt
    nbits = (nblk * QBLK - 1).bit_length()

    def tie_search():
        def idx_step(it, pos):
            trial = pos + lax.shift_left(jnp.int32(1), nbits - 1 - it)
            cnt = count(lambda kb, key: jnp.logical_and(key == thr, kb * QBLK + row < trial))
            return jnp.where(cnt < need, trial, pos)
        return lax.fori_loop(0, nbits, idx_step, jnp.zeros((1, QBLK), jnp.int32))

    has_ties = jnp.max(n_ge) > kf
    pos = lax.cond(has_ties, tie_search, lambda: jnp.full((1, QBLK), nblk * QBLK, jnp.int32))

    def mask_tile(kb):
        key = key_ref[kb]
        sel = jnp.logical_or(key > thr, jnp.logical_and(key == thr, kb * QBLK + row <= pos))
        msk_ref[kb] = jnp.where(jnp.logical_and(sel, visible(kb)), 0.0, NEG_INF)

    groupwise(mask_tile)

    q = q_ref[...].astype(BF16)
    qs = jnp.concatenate([q[:, h * B_DH:(h + 1) * B_DH] for h in range(B_HEADS)], axis=0)
    scale = B_DH ** -0.5
    mx_ref[...] = jnp.full(mx_ref.shape, NEG_INF, F32)

    def logit_tile(kb):
        r0 = pl.multiple_of(kb * QBLK, QBLK)
        lg = _dot_nt(k_ref[pl.ds(r0, QBLK), :], qs)
        msk = msk_ref[kb]
        lg = lg * scale + bt_ref[jnp.clip(j - kb, 0, 2)] + jnp.concatenate([msk] * B_HEADS, axis=1)
        lg_ref[kb] = lg
        mx_ref[...] = jnp.maximum(mx_ref[...], lg)

    groupwise(logit_tile)
    mx = jnp.max(mx_ref[...], axis=0, keepdims=True)
    ls_ref[...] = jnp.zeros_like(ls_ref)
    acc_ref[...] = jnp.zeros_like(acc_ref)

    def pv_tile(kb):
        p = jnp.exp(lg_ref[kb] - mx)
        ls_ref[...] += p
        acc_ref[...] += _dot(vt_ref[kb], p.astype(BF16))

    groupwise(pv_tile)
    out_t = acc_ref[...] / jnp.sum(ls_ref[...], axis=0, keepdims=True)
    for h in range(B_HEADS):
        o_ref[:, h * B_DH:(h + 1) * B_DH] = out_t[:, h * QBLK:(h + 1) * QBLK].T.astype(o_ref.dtype)


def dsa(bproj, k, vt, bias_b, seq):
    m = bproj.shape[0]
    bsz = m // seq
    nblk = seq // QBLK
    assert nblk % TOPK_GROUP == 0
    topk = min(TOPK_MAX, seq // 4)
    qw = B_HEADS * B_DH
    small_blk = (2 * qw + B_KV_RANK) // B_SMALL_W
    cols = B_HEADS * QBLK
    return pl.pallas_call(
        functools.partial(_dsa_kernel, nblk=nblk, topk=topk),
        grid=(bsz, nblk),
        in_specs=[pl.BlockSpec((QBLK, qw), lambda b, j: (b * nblk + j, 0)),
                  pl.BlockSpec((QBLK, qw), lambda b, j: (b * nblk + j, 1)),
                  pl.BlockSpec((QBLK, B_SMALL_W), lambda b, j: (b * nblk + j, small_blk)),
                  pl.BlockSpec((seq, B_SMALL_W), lambda b, j: (b, small_blk)),
                  pl.BlockSpec((seq, B_DH), lambda b, j: (b, 0)),
                  pl.BlockSpec((nblk, B_DH, QBLK), lambda b, j: (b, 0, 0)),
                  pl.BlockSpec((3, QBLK, cols), lambda b, j: (0, 0, 0))],
        out_specs=pl.BlockSpec((QBLK, qw), lambda b, j: (b * nblk + j, 0)),
        out_shape=jax.ShapeDtypeStruct((m, qw), BF16),
        scratch_shapes=[pltpu.VMEM((nblk, QBLK, QBLK), jnp.int32),
                        pltpu.VMEM((nblk, QBLK, QBLK), F32),
                        pltpu.VMEM((nblk, QBLK, cols), F32),
                        pltpu.VMEM((QBLK, cols), F32),
                        pltpu.VMEM((QBLK, cols), F32),
                        pltpu.VMEM((B_DH, cols), F32)],
        compiler_params=_cparams(2),
        name="dsa",
    )(bproj, bproj, bproj, bproj, k, vt, bias_b)


def _swa_kernel(sink_ref, q_ref, kc_ref, vc_ref, kp_ref, vp_ref, bt_ref, o_ref):
    j = pl.program_id(1)
    grp = C_HEADS // C_KV_HEADS
    row = lax.broadcasted_iota(jnp.int32, (QBLK, QBLK), 0)
    col = lax.broadcasted_iota(jnp.int32, (QBLK, QBLK), 1)
    cur_ok = jnp.concatenate([row <= col] * grp, axis=1)
    prev_ok = jnp.concatenate([jnp.logical_and(row > col, j > 0)] * grp, axis=1)
    q = q_ref[...].astype(BF16)
    scale = C_DH ** -0.5
    for g in range(C_KV_HEADS):
        heads = range(g * grp, (g + 1) * grp)
        kv_cols = slice(g * C_DH, (g + 1) * C_DH)
        lanes = slice(g * grp * QBLK, (g + 1) * grp * QBLK)
        qs = jnp.concatenate([q[:, h * C_DH:(h + 1) * C_DH] for h in heads], axis=0)
        sink = jnp.concatenate([jnp.full((1, QBLK), sink_ref[h], F32) for h in heads], axis=1)
        lc = _dot_nt(kc_ref[:, kv_cols].astype(BF16), qs)
        lp = _dot_nt(kp_ref[:, kv_cols].astype(BF16), qs)
        lc = jnp.where(cur_ok, lc * scale + bt_ref[0, :, lanes], NEG_INF)
        lp = jnp.where(prev_ok, lp * scale + bt_ref[1, :, lanes], NEG_INF)
        mx = jnp.maximum(jnp.maximum(jnp.max(lc, axis=0, keepdims=True),
                                     jnp.max(lp, axis=0, keepdims=True)), sink)
        pc = jnp.exp(lc - mx)
        pp = jnp.exp(lp - mx)
        den = (jnp.sum(pc, axis=0, keepdims=True) + jnp.sum(pp, axis=0, keepdims=True)
               + jnp.exp(sink - mx))
        o_t = (_dot(vc_ref[:, kv_cols].T.astype(BF16), pc.astype(BF16))
               + _dot(vp_ref[:, kv_cols].T.astype(BF16), pp.astype(BF16))) / den
        for u in range(0, grp, 2):
            pair = jnp.concatenate([o_t[:, u * QBLK:(u + 1) * QBLK],
                                    o_t[:, (u + 1) * QBLK:(u + 2) * QBLK]], axis=0)
            h0 = g * grp + u
            o_ref[:, h0 * C_DH:(h0 + 2) * C_DH] = pair.T.astype(o_ref.dtype)


def swa(cproj, sinks, bias_c, seq):
    m = cproj.shape[0]
    bsz = m // seq
    nblk = seq // QBLK
    qw = C_HEADS * C_DH
    kvw = C_KV_HEADS * C_DH
    kblk = qw // kvw

    def cur(off):
        return pl.BlockSpec((QBLK, kvw), lambda b, j: (b * nblk + j, kblk + off))

    def prev(off):
        return pl.BlockSpec((QBLK, kvw), lambda b, j: (b * nblk + jnp.maximum(j - 1, 0), kblk + off))

    return pl.pallas_call(
        _swa_kernel,
        grid=(bsz, nblk),
        in_specs=[pl.BlockSpec(memory_space=pltpu.SMEM),
                  pl.BlockSpec((QBLK, qw), lambda b, j: (b * nblk + j, 0)),
                  cur(0), cur(1), prev(0), prev(1),
                  pl.BlockSpec((3, QBLK, C_HEADS * QBLK), lambda b, j: (0, 0, 0))],
        out_specs=pl.BlockSpec((QBLK, qw), lambda b, j: (b * nblk + j, 0)),
        out_shape=jax.ShapeDtypeStruct((m, qw), BF16),
        compiler_params=_cparams(2),
        name="swa",
    )(sinks.astype(F32), cproj, cproj, cproj, cproj, cproj, bias_c)


def _pack_w_in(w):
    o = 0
    parts = {}
    for name, width in (("a", 4 * A_WIDTH), ("bq", B_HEADS * B_DH), ("blat", B_KV_RANK),
                        ("biq", IDX_HEADS * IDX_DH), ("bik", IDX_DH), ("biw", IDX_HEADS),
                        ("c", (C_HEADS + 2 * C_KV_HEADS) * C_DH)):
        parts[name] = w[:, o:o + width]
        o += width
    gates = w[:, o:]
    pad = jnp.zeros((w.shape[0], B_SMALL_W - IDX_DH - IDX_HEADS), w.dtype)
    wb = jnp.concatenate([parts["bq"], parts["biq"], parts["blat"], parts["bik"], parts["biw"], pad], axis=1)
    return parts["a"], wb, parts["c"], gates


def kernel(x, c, w_c_down, w_c_up, norm_gains, w_in, lb_logits, hgrn_norm, kv_norm, w_kv_up, rel_table,
           sinks, w_branch, w_out, ffn1_in, ffn1_out, ffn2_in, ffn2_out):
    bsz, seq, d = x.shape
    depth = w_in.shape[0]
    m = bsz * seq
    x = x.reshape(m, d).astype(F32)

    cond = matmul(c.astype(BF16), w_c_down.astype(BF16), out_dtype=BF16, act="silu")
    lbs = lower_bounds(lb_logits)
    bias_b = bias_tiles(rel_table[:, :B_HEADS], keys_on_rows=True)
    bias_c = bias_tiles(rel_table[:, B_HEADS:], keys_on_rows=True)
    lat_block = 2 * B_HEADS * B_DH // B_KV_RANK
    mods = [matmul(cond, w_c_up[l].astype(BF16), out_dtype=F32, tn=4096).reshape(bsz, N_MOD, d)
            for l in range(depth)]
    gains = norm_gains.astype(F32)
    w_in_bf16 = w_in.astype(BF16)

    h = normmod(x, gains[0], mods[0], seq, gi=0, shift_i=0, scale_i=1)
    for l in range(depth):
        mod, g = mods[l], gains[l]

        u = swiglu_in(h, ffn1_in[l].astype(BF16))
        y = matmul(u, ffn1_out[l].astype(BF16), out_dtype=BF16)
        x, h = resid_normmod(x, y, g, mod, g, mod, seq, gi=1, gate_i=2, coef=FFN_RES,
                             gi2=2, shift_i=3, scale_i=4)

        wa, wb, wc, wg = _pack_w_in(w_in_bf16[l])
        aproj = matmul(h, wa, out_dtype=F32)
        bproj = matmul(h, wb, out_dtype=F32, tn=wb.shape[1] // 3)
        cproj = matmul(h, wc, out_dtype=BF16, tn=wc.shape[1] // 2)
        ya = hgrn2(aproj, lbs[l], hgrn_norm[l].astype(F32), seq)
        kb, vb = kv_up(bproj, lat_block, kv_norm[l].astype(F32), w_kv_up[l].astype(BF16))
        yb = dsa(bproj, kb, vb, bias_b, seq)
        yc = swa(cproj, sinks[l], bias_c, seq)
        mix = merge(h, wg, ya, yb, yc, w_branch[l].astype(BF16))
        y = matmul(mix, w_out[l].astype(BF16), out_dtype=BF16)
        x, h = resid_normmod(x, y, g, mod, g, mod, seq, gi=3, gate_i=5, coef=1.0,
                             gi2=4, shift_i=6, scale_i=7)

        u = swiglu_in(h, ffn2_in[l].astype(BF16))
        y = matmul(u, ffn2_out[l].astype(BF16), out_dtype=BF16)
        if l + 1 < depth:
            x, h = resid_normmod(x, y, g, mod, gains[l + 1], mods[l + 1], seq, gi=5, gate_i=8,
                                 coef=FFN_RES, gi2=0, shift_i=0, scale_i=1)
        else:
            x = resid(x, y, g, mod, seq, gi=5, gate_i=8, coef=FFN_RES)
    return x.reshape(bsz, seq, d)
```

```python
import functools
import math

import jax
import jax.numpy as jnp
from jax import lax
from jax.experimental import pallas as pl
from jax.experimental.pallas import tpu as pltpu

A_HEADS = 8
A_DK = 128
A_DV = 128
A_WIDTH = A_HEADS * A_DK
B_HEADS = 8
B_DH = 128
B_KV_RANK = 512
IDX_HEADS = 16
IDX_DH = 64
TOPK_MAX = 256
C_HEADS = 16
C_KV_HEADS = 2
C_DH = 64
WINDOW = 128
REL_BUCKETS = 32
REL_MAX_DIST = 128
N_BRANCH = 3
BRANCH_W = 1024
FFN_RES = 0.5
N_MOD = 9
RMS_EPS = 1e-6
NEG_INF = -1e30
LOG2_E = math.log2(math.e)

VMEM_LIMIT_BYTES = 56 * 1024 * 1024

QBLK = 128
HGRN_CHUNK = 64
HGRN_SUB = 16
HGRN_HEADS_PER_STEP = 4
TOPK_GROUP = 4
B_SMALL_W = 128

BF16 = jnp.bfloat16
F32 = jnp.float32


def _cparams(n_grid_dims):
    return pltpu.CompilerParams(dimension_semantics=("arbitrary",) * n_grid_dims,
                                vmem_limit_bytes=VMEM_LIMIT_BYTES)


def _dot(a, b):
    return jnp.dot(a, b, preferred_element_type=F32)


def _dot_nt(a, b):
    return lax.dot_general(a, b, (((1,), (1,)), ((), ())), preferred_element_type=F32)


def _silu(x):
    return x * jax.nn.sigmoid(x)


def _lower_bound_kernel(x_ref, o_ref):
    x = x_ref[...]
    m = jnp.max(x, axis=0, keepdims=True)
    e = jnp.exp(x - m)
    p = e / jnp.sum(e, axis=0, keepdims=True)
    depth = x.shape[0]
    run = jnp.zeros_like(p[0:1])
    for l in range(depth):
        o_ref[l:l + 1, :] = run
        if l + 1 < depth:
            run = run + p[l + 1:l + 2]


def lower_bounds(lb_logits):
    return pl.pallas_call(
        _lower_bound_kernel,
        out_shape=jax.ShapeDtypeStruct(lb_logits.shape, F32),
        name="lower_bounds",
    )(lb_logits.astype(F32))


def _t5_bucket(dist):
    max_exact = REL_BUCKETS // 2
    d = jnp.maximum(dist, 0)
    df = jnp.maximum(d, 1).astype(F32)
    large = max_exact + (jnp.log(df / max_exact) / math.log(REL_MAX_DIST / max_exact)
                         * (REL_BUCKETS - max_exact)).astype(jnp.int32)
    large = jnp.minimum(large, REL_BUCKETS - 1)
    return jnp.where(d < max_exact, d, large)


def _bias_tile_kernel(tab_ref, o_ref, *, keys_on_rows):
    h = pl.program_id(0)
    row = lax.broadcasted_iota(jnp.int32, (QBLK, QBLK), 0)
    col = lax.broadcasted_iota(jnp.int32, (QBLK, QBLK), 1)
    t_minus_s = col - row if keys_on_rows else row - col
    for r in range(3):
        bucket = _t5_bucket(r * QBLK + t_minus_s)
        tile = jnp.zeros((QBLK, QBLK), F32)
        for b in range(REL_BUCKETS):
            tile = jnp.where(bucket == b, tab_ref[h, b], tile)
        if keys_on_rows:
            o_ref[r] = tile
        else:
            o_ref[r, 0] = tile


def bias_tiles(rel_table, *, keys_on_rows):
    n_heads = rel_table.shape[1]
    if keys_on_rows:
        out_spec = pl.BlockSpec((3, QBLK, QBLK), lambda h: (0, 0, h))
        out_shape = (3, QBLK, n_heads * QBLK)
    else:
        out_spec = pl.BlockSpec((3, 1, QBLK, QBLK), lambda h: (0, h, 0, 0))
        out_shape = (3, n_heads, QBLK, QBLK)
    return pl.pallas_call(
        functools.partial(_bias_tile_kernel, keys_on_rows=keys_on_rows),
        grid=(n_heads,),
        in_specs=[pl.BlockSpec(memory_space=pltpu.SMEM)],
        out_specs=out_spec,
        out_shape=jax.ShapeDtypeStruct(out_shape, F32),
        compiler_params=_cparams(1),
        name="bias_tiles",
    )(rel_table.T.astype(F32))


def _rms(x, g):
    return x * lax.rsqrt(jnp.mean(x * x, axis=-1, keepdims=True) + RMS_EPS) * g


def _modulate(x, g_ref, mod_ref, gi, shift_i, scale_i):
    y = _rms(x, g_ref[gi:gi + 1, :])
    return y * (1.0 + mod_ref[0, scale_i:scale_i + 1, :]) + mod_ref[0, shift_i:shift_i + 1, :]


def _normmod_kernel(x_ref, g_ref, mod_ref, h_ref, *, gi, shift_i, scale_i):
    h_ref[...] = _modulate(x_ref[...], g_ref, mod_ref, gi, shift_i, scale_i).astype(h_ref.dtype)


def normmod(x, gains, mod, seq, *, gi, shift_i, scale_i, tm=256):
    m, d = x.shape
    tpb = seq // tm
    return pl.pallas_call(
        functools.partial(_normmod_kernel, gi=gi, shift_i=shift_i, scale_i=scale_i),
        grid=(m // tm,),
        in_specs=[pl.BlockSpec((tm, d), lambda i: (i, 0)),
                  pl.BlockSpec(gains.shape, lambda i: (0, 0)),
                  pl.BlockSpec((1, N_MOD, d), lambda i: (i // tpb, 0, 0))],
        out_specs=pl.BlockSpec((tm, d), lambda i: (i, 0)),
        out_shape=jax.ShapeDtypeStruct((m, d), BF16),
        compiler_params=_cparams(1),
        name="normmod",
    )(x, gains, mod)


def _resid_kernel(x_ref, y_ref, g_ref, mod_ref, x_out_ref, *, gi, gate_i, coef):
    y = _rms(y_ref[...].astype(F32), g_ref[gi:gi + 1, :])
    x_out_ref[...] = x_ref[...] + coef * mod_ref[0, gate_i:gate_i + 1, :] * y


def resid(x, y, gains, mod, seq, *, gi, gate_i, coef, tm=256):
    m, d = x.shape
    tpb = seq // tm
    row = pl.BlockSpec((tm, d), lambda i: (i, 0))
    return pl.pallas_call(
        functools.partial(_resid_kernel, gi=gi, gate_i=gate_i, coef=coef),
        grid=(m // tm,),
        in_specs=[row, row,
                  pl.BlockSpec(gains.shape, lambda i: (0, 0)),
                  pl.BlockSpec((1, N_MOD, d), lambda i: (i // tpb, 0, 0))],
        out_specs=row,
        out_shape=jax.ShapeDtypeStruct((m, d), F32),
        compiler_params=_cparams(1),
        name="resid",
    )(x, y, gains, mod)


def _resid_normmod_kernel(x_ref, y_ref, g_ref, mod_ref, g2_ref, mod2_ref, x_out_ref, h_ref, *,
                          gi, gate_i, coef, gi2, shift_i, scale_i):
    y = _rms(y_ref[...].astype(F32), g_ref[gi:gi + 1, :])
    x_new = x_ref[...] + coef * mod_ref[0, gate_i:gate_i + 1, :] * y
    x_out_ref[...] = x_new
    h_ref[...] = _modulate(x_new, g2_ref, mod2_ref, gi2, shift_i, scale_i).astype(h_ref.dtype)


def resid_normmod(x, y, gains, mod, gains2, mod2, seq, *, gi, gate_i, coef, gi2, shift_i, scale_i, tm=256):
    m, d = x.shape
    tpb = seq // tm
    row = pl.BlockSpec((tm, d), lambda i: (i, 0))
    gspec = pl.BlockSpec(gains.shape, lambda i: (0, 0))
    mspec = pl.BlockSpec((1, N_MOD, d), lambda i: (i // tpb, 0, 0))
    return pl.pallas_call(
        functools.partial(_resid_normmod_kernel, gi=gi, gate_i=gate_i, coef=coef, gi2=gi2,
                          shift_i=shift_i, scale_i=scale_i),
        grid=(m // tm,),
        in_specs=[row, row, gspec, mspec, gspec, mspec],
        out_specs=[row, row],
        out_shape=[jax.ShapeDtypeStruct((m, d), F32), jax.ShapeDtypeStruct((m, d), BF16)],
        compiler_params=_cparams(1),
        name="resid_normmod",
    )(x, y, gains, mod, gains2, mod2)


def _mm_kernel(a_ref, w_ref, o_ref, *, act):
    acc = _dot(a_ref[...], w_ref[...])
    if act == "silu":
        acc = _silu(acc)
    o_ref[...] = acc.astype(o_ref.dtype)


def _layer_spec(tail_block, tail_index, layer):
    if layer is None:
        return pl.BlockSpec(tail_block, tail_index)
    return pl.BlockSpec((None,) + tail_block, lambda *g: (layer,) + tail_index(*g))


def matmul(a, w, *, out_dtype, layer=None, act=None, tm=1024, tn=512):
    m, k = a.shape
    n = w.shape[-1]
    tm, tn = min(tm, m), min(tn, n)
    assert m % tm == 0 and n % tn == 0
    return pl.pallas_call(
        functools.partial(_mm_kernel, act=act),
        grid=(m // tm, n // tn),
        in_specs=[pl.BlockSpec((tm, k), lambda i, j: (i, 0)),
                  _layer_spec((k, tn), lambda i, j: (0, j), layer)],
        out_specs=pl.BlockSpec((tm, tn), lambda i, j: (i, j)),
        out_shape=jax.ShapeDtypeStruct((m, n), out_dtype),
        compiler_params=_cparams(2),
        name="matmul",
    )(a, w)


def _swiglu_kernel(a_ref, wu_ref, wv_ref, o_ref):
    a = a_ref[...]
    u = _dot(a, wu_ref[...])
    v = _dot(a, wv_ref[...])
    o_ref[...] = (_silu(u) * v).astype(o_ref.dtype)


def swiglu_in(a, w, layer, *, tm=1024, tn=512):
    m, k = a.shape
    f = w.shape[-1] // 2
    tm = min(tm, m)
    nf = f // tn
    return pl.pallas_call(
        _swiglu_kernel,
        grid=(m // tm, nf),
        in_specs=[pl.BlockSpec((tm, k), lambda i, j: (i, 0)),
                  _layer_spec((k, tn), lambda i, j: (0, j), layer),
                  _layer_spec((k, tn), lambda i, j: (0, j + nf), layer)],
        out_specs=pl.BlockSpec((tm, tn), lambda i, j: (i, j)),
        out_shape=jax.ShapeDtypeStruct((m, f), BF16),
        compiler_params=_cparams(2),
        name="swiglu_in",
    )(a, w, w)


def _merge_kernel(h_ref, wga_ref, wgb_ref, wgc_ref, ya_ref, yb_ref, yc_ref, wb_ref, o_ref):
    h = h_ref[...]
    acc = jax.nn.sigmoid(_dot(h, wga_ref[...])) * _dot(ya_ref[...], wb_ref[0])
    acc = acc + jax.nn.sigmoid(_dot(h, wgb_ref[...])) * _dot(yb_ref[...], wb_ref[1])
    acc = acc + jax.nn.sigmoid(_dot(h, wgc_ref[...])) * _dot(yc_ref[...], wb_ref[2])
    o_ref[...] = acc.astype(o_ref.dtype)


def merge(h, w_gates, ya, yb, yc, w_branch, layer, *, tm=1024, tn=256):
    m, k = h.shape
    kw = ya.shape[1]
    d = w_branch.shape[-1]
    tm = min(tm, m)
    nd = d // tn
    y_spec = pl.BlockSpec((tm, kw), lambda i, j: (i, 0))
    return pl.pallas_call(
        _merge_kernel,
        grid=(m // tm, nd),
        in_specs=[pl.BlockSpec((tm, k), lambda i, j: (i, 0)),
                  _layer_spec((k, tn), lambda i, j: (0, j), layer),
                  _layer_spec((k, tn), lambda i, j: (0, j + nd), layer),
                  _layer_spec((k, tn), lambda i, j: (0, j + 2 * nd), layer),
                  y_spec, y_spec, y_spec,
                  _layer_spec((N_BRANCH, kw, tn), lambda i, j: (0, 0, j), layer)],
        out_specs=pl.BlockSpec((tm, tn), lambda i, j: (i, j)),
        out_shape=jax.ShapeDtypeStruct((m, d), BF16),
        compiler_params=_cparams(2),
        name="merge",
    )(h, w_gates, w_gates, w_gates, ya, yb, yc, w_branch)


def _kvup_kernel(lat_ref, g_ref, w_ref, k_ref, vt_ref):
    h = _rms(lat_ref[...], g_ref[...]).astype(BF16)
    kv = _dot(h, w_ref[...])
    k_ref[...] = kv[:, :B_DH].astype(k_ref.dtype)
    v = kv[:, B_DH:]
    for u in range(vt_ref.shape[0]):
        vt_ref[u] = v[u * QBLK:(u + 1) * QBLK].T.astype(vt_ref.dtype)


def kv_up(bproj, lat_block, kv_norm_g, w_kv_up, layer, *, tm=512):
    m = bproj.shape[0]
    return pl.pallas_call(
        _kvup_kernel,
        grid=(m // tm,),
        in_specs=[pl.BlockSpec((tm, B_KV_RANK), lambda i: (i, lat_block)),
                  pl.BlockSpec((1, B_KV_RANK), lambda i: (0, 0)),
                  _layer_spec((B_KV_RANK, 2 * B_DH), lambda i: (0, 0), layer)],
        out_specs=[pl.BlockSpec((tm, B_DH), lambda i: (i, 0)),
                   pl.BlockSpec((tm // QBLK, B_DH, QBLK), lambda i: (i, 0, 0))],
        out_shape=[jax.ShapeDtypeStruct((m, B_DH), BF16),
                   jax.ShapeDtypeStruct((m // QBLK, B_DH, QBLK), BF16)],
        compiler_params=_cparams(1),
        name="kv_up",
    )(bproj, kv_norm_g.reshape(1, B_KV_RANK), w_kv_up)


def _hgrn_chunk(q, fr, ir, gr, lb, ng, st, tril):
    cs, sub = HGRN_CHUNK, HGRN_SUB
    f = lb + (1.0 - lb) * jax.nn.sigmoid(fr)
    lf = jnp.log(jnp.maximum(f, 1e-20))
    k = 1.0 - f
    v = _silu(ir)
    b = jnp.dot(tril, lf, precision=lax.Precision.HIGHEST, preferred_element_type=F32) * LOG2_E
    o_inter = _dot_nt((q * jnp.exp2(b)).astype(BF16), st.astype(BF16))

    col_s = lax.broadcasted_iota(jnp.int32, (sub, cs), 1)
    blocks = []
    for si in range(cs // sub):
        lo = si * sub
        q_s = q[lo:lo + sub]
        b_s = b[lo:lo + sub]
        if si > 0:
            bref = b[lo - 1:lo]
            qt = (q_s * jnp.exp2(b_s - bref)).astype(BF16)
            kj = (k[:lo] * jnp.exp2(bref - b[:lo])).astype(BF16)
            attn = jnp.concatenate([_dot_nt(qt, kj), jnp.zeros((sub, cs - lo), F32)], axis=1)
        else:
            attn = jnp.zeros((sub, cs), F32)
        for s in range(sub):
            a = q_s * k[lo + s:lo + s + 1] * jnp.exp2(b_s - b[lo + s:lo + s + 1])
            attn = jnp.where(col_s == lo + s, jnp.sum(a, axis=-1, keepdims=True), attn)
        blocks.append(attn)
    attn = jnp.where(tril > 0.0, jnp.concatenate(blocks, axis=0), 0.0)
    o = o_inter + _dot(attn.astype(BF16), v.astype(BF16))

    b_end = b[cs - 1:cs]
    kd = (k * jnp.exp2(b_end - b)).astype(BF16)
    upd = _dot(v.T.astype(BF16), kd)
    st_new = st * jnp.exp2(b_end) + upd
    return _rms(o, ng) * _silu(gr), st_new


def _hgrn_kernel(q_ref, f_ref, i_ref, g_ref, lb_ref, ng_ref, o_ref, st_ref, *, seq):
    cs = HGRN_CHUNK
    st_ref[...] = jnp.zeros_like(st_ref)
    ng = ng_ref[...]
    row_c = lax.broadcasted_iota(jnp.int32, (cs, cs), 0)
    col_c = lax.broadcasted_iota(jnp.int32, (cs, cs), 1)
    tril = (col_c <= row_c).astype(F32)

    def chunk(c, carry):
        r0 = pl.multiple_of(c * cs, cs)
        for h in range(HGRN_HEADS_PER_STEP):
            cols = slice(h * A_DK, (h + 1) * A_DK)
            o, st_new = _hgrn_chunk(q_ref[pl.ds(r0, cs), cols], f_ref[pl.ds(r0, cs), cols],
                                    i_ref[pl.ds(r0, cs), cols], g_ref[pl.ds(r0, cs), cols],
                                    lb_ref[:, cols], ng, st_ref[h], tril)
            st_ref[h] = st_new
            o_ref[pl.ds(r0, cs), cols] = o.astype(o_ref.dtype)
        return carry

    lax.fori_loop(0, seq // cs, chunk, 0, unroll=2)


def hgrn2(aproj, lb, norm_g, seq):
    m = aproj.shape[0]
    bsz = m // seq
    nh = HGRN_HEADS_PER_STEP
    steps = A_HEADS // nh

    def col(part):
        return pl.BlockSpec((seq, nh * A_DK), lambda b, h: (b, part * steps + h))

    return pl.pallas_call(
        functools.partial(_hgrn_kernel, seq=seq),
        grid=(bsz, steps),
        in_specs=[col(0), col(1), col(2), col(3),
                  pl.BlockSpec((1, nh * A_DK), lambda b, h: (0, h)),
                  pl.BlockSpec((1, A_DV), lambda b, h: (0, 0))],
        out_specs=pl.BlockSpec((seq, nh * A_DV), lambda b, h: (b, h)),
        out_shape=jax.ShapeDtypeStruct((m, A_HEADS * A_DV), BF16),
        scratch_shapes=[pltpu.VMEM((nh, A_DV, A_DK), F32)],
        compiler_params=_cparams(2),
        name="hgrn2",
    )(aproj, aproj, aproj, aproj, lb.reshape(1, A_WIDTH), norm_g.reshape(1, A_DV))


def _dsa_kernel(q_ref, iq_ref, smq_ref, smk_ref, k_ref, vt_ref, bt_ref, o_ref,
                key_ref, msk_ref, lg_ref, mx_ref, ls_ref, acc_ref, *, nblk, topk):
    j = pl.program_id(1)
    row = lax.broadcasted_iota(jnp.int32, (QBLK, QBLK), 0)
    col = lax.broadcasted_iota(jnp.int32, (QBLK, QBLK), 1)
    causal = row <= col
    n_grp = j // TOPK_GROUP + 1
    n_pair = j // 2 + 1
    int_min = jnp.int32(-2 ** 31)

    def sort_key(score):
        bits = lax.bitcast_convert_type(score, jnp.int32)
        return bits ^ ((bits >> 31) & jnp.int32(0x7FFFFFFF))

    iq = iq_ref[...].astype(BF16)
    iqs = jnp.concatenate([iq[:, h * IDX_DH:(h + 1) * IDX_DH] for h in range(IDX_HEADS)], axis=0)
    iwt = (smq_ref[:, IDX_DH:IDX_DH + IDX_HEADS] * (IDX_HEADS ** -0.5 * IDX_DH ** -0.5)).T

    def visible(kb):
        return jnp.logical_or(kb < j, jnp.logical_and(kb == j, causal))

    def score_tile(kb):
        r0 = pl.multiple_of(kb * QBLK, QBLK)
        ik = smk_ref[pl.ds(r0, QBLK), 0:IDX_DH].astype(BF16)
        d = _dot_nt(ik, iqs)
        sc = jnp.zeros((QBLK, QBLK), F32)
        for h in range(IDX_HEADS):
            sc = sc + jnp.maximum(d[:, h * QBLK:(h + 1) * QBLK], 0.0) * iwt[h:h + 1, :]
        key_ref[kb] = sort_key(jnp.where(visible(kb), sc, NEG_INF))

    def pairwise(tile_fn):
        def step(kp, carry):
            tile_fn(2 * kp)
            tile_fn(2 * kp + 1)
            return carry
        lax.fori_loop(0, n_pair, step, 0)

    def groupwise(tile_fn):
        def step(gi, carry):
            for u in range(TOPK_GROUP):
                tile_fn(gi * TOPK_GROUP + u)
            return carry
        lax.fori_loop(0, n_grp, step, 0)

    pairwise(score_tile)

    def fill_step(kb, carry):
        key_ref[kb] = sort_key(jnp.full((QBLK, QBLK), NEG_INF, F32))
        return carry

    lax.fori_loop(2 * n_pair, n_grp * TOPK_GROUP, fill_step, 0)

    kf = jnp.float32(topk)

    def count(pred):
        def grp(gi, c):
            for u in range(TOPK_GROUP):
                kb = gi * TOPK_GROUP + u
                c = c + jnp.where(pred(kb, key_ref[kb]), 1.0, 0.0)
            return c
        c = lax.fori_loop(0, n_grp, grp, jnp.zeros((QBLK, QBLK), F32))
        return jnp.sum(c, axis=0, keepdims=True)

    def thr_step(it, cand):
        trial = cand | lax.shift_left(jnp.int32(1), 31 - it)
        t_s = trial ^ int_min
        cnt = count(lambda kb, key: key >= t_s)
        return jnp.where(cnt >= kf, trial, cand)

    cand = lax.fori_loop(0, 32, thr_step, jnp.zeros((1, QBLK), jnp.int32))
    thr = cand ^ int_min
    n_gt = count(lambda kb, key: key > thr)
    n_ge = count(lambda kb, key: key >= thr)
    need = kf - n_gt
    nbits = (nblk * QBLK - 1).bit_length()

    def tie_search():
        def idx_step(it, pos):
            trial = pos + lax.shift_left(jnp.int32(1), nbits - 1 - it)
            cnt = count(lambda kb, key: jnp.logical_and(key == thr, kb * QBLK + row < trial))
            return jnp.where(cnt < need, trial, pos)
        return lax.fori_loop(0, nbits, idx_step, jnp.zeros((1, QBLK), jnp.int32))

    has_ties = jnp.max(n_ge) > kf
    pos = lax.cond(has_ties, tie_search, lambda: jnp.full((1, QBLK), nblk * QBLK, jnp.int32))

    def mask_tile(kb):
        key = key_ref[kb]
        sel = jnp.logical_or(key > thr, jnp.logical_and(key == thr, kb * QBLK + row <= pos))
        msk_ref[kb] = jnp.where(jnp.logical_and(sel, visible(kb)), 0.0, NEG_INF)

    groupwise(mask_tile)

    q = q_ref[...].astype(BF16)
    qs = jnp.concatenate([q[:, h * B_DH:(h + 1) * B_DH] for h in range(B_HEADS)], axis=0)
    scale = B_DH ** -0.5
    mx_ref[...] = jnp.full(mx_ref.shape, NEG_INF, F32)

    def logit_tile(kb):
        r0 = pl.multiple_of(kb * QBLK, QBLK)
        lg = _dot_nt(k_ref[pl.ds(r0, QBLK), :], qs)
        msk = msk_ref[kb]
        lg = lg * scale + bt_ref[jnp.clip(j - kb, 0, 2)] + jnp.concatenate([msk] * B_HEADS, axis=1)
        lg_ref[kb] = lg
        mx_ref[...] = jnp.maximum(mx_ref[...], lg)

    groupwise(logit_tile)
    mx = jnp.max(mx_ref[...], axis=0, keepdims=True)
    ls_ref[...] = jnp.zeros_like(ls_ref)
    acc_ref[...] = jnp.zeros_like(acc_ref)

    def pv_tile(kb):
        p = jnp.exp(lg_ref[kb] - mx)
        ls_ref[...] += p
        acc_ref[...] += _dot(vt_ref[kb], p.astype(BF16))

    groupwise(pv_tile)
    out_t = acc_ref[...] / jnp.sum(ls_ref[...], axis=0, keepdims=True)
    for h in range(B_HEADS):
        o_ref[:, h * B_DH:(h + 1) * B_DH] = out_t[:, h * QBLK:(h + 1) * QBLK].T.astype(o_ref.dtype)


def dsa(bproj, k, vt, bias_b, seq):
    m = bproj.shape[0]
    bsz = m // seq
    nblk = seq // QBLK
    assert nblk % TOPK_GROUP == 0
    topk = min(TOPK_MAX, seq // 4)
    qw = B_HEADS * B_DH
    small_blk = (2 * qw + B_KV_RANK) // B_SMALL_W
    cols = B_HEADS * QBLK
    return pl.pallas_call(
        functools.partial(_dsa_kernel, nblk=nblk, topk=topk),
        grid=(bsz, nblk),
        in_specs=[pl.BlockSpec((QBLK, qw), lambda b, j: (b * nblk + j, 0)),
                  pl.BlockSpec((QBLK, qw), lambda b, j: (b * nblk + j, 1)),
                  pl.BlockSpec((QBLK, B_SMALL_W), lambda b, j: (b * nblk + j, small_blk)),
                  pl.BlockSpec((seq, B_SMALL_W), lambda b, j: (b, small_blk)),
                  pl.BlockSpec((seq, B_DH), lambda b, j: (b, 0)),
                  pl.BlockSpec((nblk, B_DH, QBLK), lambda b, j: (b, 0, 0)),
                  pl.BlockSpec((3, QBLK, cols), lambda b, j: (0, 0, 0))],
        out_specs=pl.BlockSpec((QBLK, qw), lambda b, j: (b * nblk + j, 0)),
        out_shape=jax.ShapeDtypeStruct((m, qw), BF16),
        scratch_shapes=[pltpu.VMEM((nblk, QBLK, QBLK), jnp.int32),
                        pltpu.VMEM((nblk, QBLK, QBLK), F32),
                        pltpu.VMEM((nblk, QBLK, cols), F32),
                        pltpu.VMEM((QBLK, cols), F32),
                        pltpu.VMEM((QBLK, cols), F32),
                        pltpu.VMEM((B_DH, cols), F32)],
        compiler_params=_cparams(2),
        name="dsa",
    )(bproj, bproj, bproj, bproj, k, vt, bias_b)


def _swa_kernel(sink_ref, q_ref, kc_ref, vc_ref, kp_ref, vp_ref, bt_ref, o_ref):
    j = pl.program_id(1)
    grp = C_HEADS // C_KV_HEADS
    row = lax.broadcasted_iota(jnp.int32, (QBLK, QBLK), 0)
    col = lax.broadcasted_iota(jnp.int32, (QBLK, QBLK), 1)
    cur_ok = jnp.concatenate([row <= col] * grp, axis=1)
    prev_ok = jnp.concatenate([jnp.logical_and(row > col, j > 0)] * grp, axis=1)
    q = q_ref[...].astype(BF16)
    scale = C_DH ** -0.5
    for g in range(C_KV_HEADS):
        heads = range(g * grp, (g + 1) * grp)
        kv_cols = slice(g * C_DH, (g + 1) * C_DH)
        lanes = slice(g * grp * QBLK, (g + 1) * grp * QBLK)
        qs = jnp.concatenate([q[:, h * C_DH:(h + 1) * C_DH] for h in heads], axis=0)
        sink = jnp.concatenate([jnp.full((1, QBLK), sink_ref[h], F32) for h in heads], axis=1)
        lc = _dot_nt(kc_ref[:, kv_cols].astype(BF16), qs)
        lp = _dot_nt(kp_ref[:, kv_cols].astype(BF16), qs)
        lc = jnp.where(cur_ok, lc * scale + bt_ref[0, :, lanes], NEG_INF)
        lp = jnp.where(prev_ok, lp * scale + bt_ref[1, :, lanes], NEG_INF)
        mx = jnp.maximum(jnp.maximum(jnp.max(lc, axis=0, keepdims=True),
                                     jnp.max(lp, axis=0, keepdims=True)), sink)
        pc = jnp.exp(lc - mx)
        pp = jnp.exp(lp - mx)
        den = (jnp.sum(pc, axis=0, keepdims=True) + jnp.sum(pp, axis=0, keepdims=True)
               + jnp.exp(sink - mx))
        o_t = (_dot(vc_ref[:, kv_cols].T.astype(BF16), pc.astype(BF16))
               + _dot(vp_ref[:, kv_cols].T.astype(BF16), pp.astype(BF16))) / den
        for u in range(0, grp, 2):
            pair = jnp.concatenate([o_t[:, u * QBLK:(u + 1) * QBLK],
                                    o_t[:, (u + 1) * QBLK:(u + 2) * QBLK]], axis=0)
            h0 = g * grp + u
            o_ref[:, h0 * C_DH:(h0 + 2) * C_DH] = pair.T.astype(o_ref.dtype)


def swa(cproj, sinks, bias_c, seq):
    m = cproj.shape[0]
    bsz = m // seq
    nblk = seq // QBLK
    qw = C_HEADS * C_DH
    kvw = C_KV_HEADS * C_DH
    kblk = qw // kvw

    def cur(off):
        return pl.BlockSpec((QBLK, kvw), lambda b, j: (b * nblk + j, kblk + off))

    def prev(off):
        return pl.BlockSpec((QBLK, kvw), lambda b, j: (b * nblk + jnp.maximum(j - 1, 0), kblk + off))

    return pl.pallas_call(
        _swa_kernel,
        grid=(bsz, nblk),
        in_specs=[pl.BlockSpec(memory_space=pltpu.SMEM),
                  pl.BlockSpec((QBLK, qw), lambda b, j: (b * nblk + j, 0)),
                  cur(0), cur(1), prev(0), prev(1),
                  pl.BlockSpec((3, QBLK, C_HEADS * QBLK), lambda b, j: (0, 0, 0))],
        out_specs=pl.BlockSpec((QBLK, qw), lambda b, j: (b * nblk + j, 0)),
        out_shape=jax.ShapeDtypeStruct((m, qw), BF16),
        compiler_params=_cparams(2),
        name="swa",
    )(sinks.astype(F32), cproj, cproj, cproj, cproj, cproj, bias_c)


def _pack_w_in(w):
    o = 0
    parts = {}
    for name, width in (("a", 4 * A_WIDTH), ("bq", B_HEADS * B_DH), ("blat", B_KV_RANK),
                        ("biq", IDX_HEADS * IDX_DH), ("bik", IDX_DH), ("biw", IDX_HEADS),
                        ("c", (C_HEADS + 2 * C_KV_HEADS) * C_DH)):
        parts[name] = w[..., o:o + width]
        o += width
    gates = w[..., o:]
    pad = jnp.zeros(w.shape[:-1] + (B_SMALL_W - IDX_DH - IDX_HEADS,), w.dtype)
    wb = jnp.concatenate([parts["bq"], parts["biq"], parts["blat"], parts["bik"], parts["biw"], pad], axis=-1)
    return tuple(t.astype(BF16) for t in (parts["a"], wb, parts["c"], gates))


def kernel(x, c, w_c_down, w_c_up, norm_gains, w_in, lb_logits, hgrn_norm, kv_norm, w_kv_up, rel_table,
           sinks, w_branch, w_out, ffn1_in, ffn1_out, ffn2_in, ffn2_out):
    bsz, seq, d = x.shape
    depth = w_in.shape[0]
    m = bsz * seq
    x = x.reshape(m, d).astype(F32)

    cond = matmul(c.astype(BF16), w_c_down.astype(BF16), out_dtype=BF16, act="silu")
    lbs = lower_bounds(lb_logits)
    bias_b = bias_tiles(rel_table[:, :B_HEADS], keys_on_rows=True)
    bias_c = bias_tiles(rel_table[:, B_HEADS:], keys_on_rows=True)
    lat_block = 2 * B_HEADS * B_DH // B_KV_RANK
    w_c_up, w_kv_up, w_branch, w_out, ffn1_in, ffn1_out, ffn2_in, ffn2_out = (
        t.astype(BF16) for t in (w_c_up, w_kv_up, w_branch, w_out, ffn1_in, ffn1_out, ffn2_in, ffn2_out))
    wa, wb, wc, wg = _pack_w_in(w_in)
    mods = [matmul(cond, w_c_up, layer=l, out_dtype=F32, tn=4096).reshape(bsz, N_MOD, d)
            for l in range(depth)]
    gains = norm_gains.astype(F32)

    h = normmod(x, gains[0], mods[0], seq, gi=0, shift_i=0, scale_i=1)
    for l in range(depth):
        mod, g = mods[l], gains[l]

        u = swiglu_in(h, ffn1_in, l)
        y = matmul(u, ffn1_out, layer=l, out_dtype=BF16)
        x, h = resid_normmod(x, y, g, mod, g, mod, seq, gi=1, gate_i=2, coef=FFN_RES,
                             gi2=2, shift_i=3, scale_i=4)

        aproj = matmul(h, wa, layer=l, out_dtype=F32)
        bproj = matmul(h, wb, layer=l, out_dtype=F32, tn=wb.shape[-1] // 3)
        cproj = matmul(h, wc, layer=l, out_dtype=BF16, tn=wc.shape[-1] // 2)
        ya = hgrn2(aproj, lbs[l], hgrn_norm[l].astype(F32), seq)
        kb, vb = kv_up(bproj, lat_block, kv_norm[l].astype(F32), w_kv_up, l)
        yb = dsa(bproj, kb, vb, bias_b, seq)
        yc = swa(cproj, sinks[l], bias_c, seq)
        mix = merge(h, wg, ya, yb, yc, w_branch, l)
        y = matmul(mix, w_out, layer=l, out_dtype=BF16)
        x, h = resid_normmod(x, y, g, mod, g, mod, seq, gi=3, gate_i=5, coef=1.0,
                             gi2=4, shift_i=6, scale_i=7)

        u = swiglu_in(h, ffn2_in, l)
        y = matmul(u, ffn2_out, layer=l, out_dtype=BF16)
        if l + 1 < depth:
            x, h = resid_normmod(x, y, g, mod, gains[l + 1], mods[l + 1], seq, gi=5, gate_i=8,
                                 coef=FFN_RES, gi2=0, shift_i=0, scale_i=1)
        else:
            x = resid(x, y, g, mod, seq, gi=5, gate_i=8, coef=FFN_RES)
    return x.reshape(bsz, seq, d)
```

```python
import functools
import math

import jax
import jax.numpy as jnp
from jax import lax
from jax.experimental import pallas as pl
from jax.experimental.pallas import tpu as pltpu

A_HEADS = 8
A_DK = 128
A_DV = 128
A_WIDTH = A_HEADS * A_DK
B_HEADS = 8
B_DH = 128
B_KV_RANK = 512
IDX_HEADS = 16
IDX_DH = 64
TOPK_MAX = 256
C_HEADS = 16
C_KV_HEADS = 2
C_DH = 64
WINDOW = 128
REL_BUCKETS = 32
REL_MAX_DIST = 128
N_BRANCH = 3
BRANCH_W = 1024
FFN_RES = 0.5
N_MOD = 9
RMS_EPS = 1e-6
NEG_INF = -1e30
LOG2_E = math.log2(math.e)

VMEM_LIMIT_BYTES = 56 * 1024 * 1024

QBLK = 128
HGRN_CHUNK = 64
HGRN_SUB = 16
HGRN_HEADS_PER_STEP = 4
TOPK_GROUP = 4
B_SMALL_W = 128

BF16 = jnp.bfloat16
F32 = jnp.float32


def _cparams(n_grid_dims):
    return pltpu.CompilerParams(dimension_semantics=("arbitrary",) * n_grid_dims,
                                vmem_limit_bytes=VMEM_LIMIT_BYTES)


def _dot(a, b):
    return jnp.dot(a, b, preferred_element_type=F32)


def _dot_nt(a, b):
    return lax.dot_general(a, b, (((1,), (1,)), ((), ())), preferred_element_type=F32)


def _silu(x):
    return x * jax.nn.sigmoid(x)


def _lower_bound_kernel(x_ref, o_ref):
    x = x_ref[...]
    m = jnp.max(x, axis=0, keepdims=True)
    e = jnp.exp(x - m)
    p = e / jnp.sum(e, axis=0, keepdims=True)
    depth = x.shape[0]
    run = jnp.zeros_like(p[0:1])
    for l in range(depth):
        o_ref[l:l + 1, :] = run
        if l + 1 < depth:
            run = run + p[l + 1:l + 2]


def lower_bounds(lb_logits):
    return pl.pallas_call(
        _lower_bound_kernel,
        out_shape=jax.ShapeDtypeStruct(lb_logits.shape, F32),
        name="lower_bounds",
    )(lb_logits.astype(F32))


def _t5_bucket(dist):
    max_exact = REL_BUCKETS // 2
    d = jnp.maximum(dist, 0)
    df = jnp.maximum(d, 1).astype(F32)
    large = max_exact + (jnp.log(df / max_exact) / math.log(REL_MAX_DIST / max_exact)
                         * (REL_BUCKETS - max_exact)).astype(jnp.int32)
    large = jnp.minimum(large, REL_BUCKETS - 1)
    return jnp.where(d < max_exact, d, large)


def _bias_tile_kernel(tab_ref, o_ref, *, keys_on_rows):
    h = pl.program_id(0)
    row = lax.broadcasted_iota(jnp.int32, (QBLK, QBLK), 0)
    col = lax.broadcasted_iota(jnp.int32, (QBLK, QBLK), 1)
    t_minus_s = col - row if keys_on_rows else row - col
    for r in range(3):
        bucket = _t5_bucket(r * QBLK + t_minus_s)
        tile = jnp.zeros((QBLK, QBLK), F32)
        for b in range(REL_BUCKETS):
            tile = jnp.where(bucket == b, tab_ref[h, b], tile)
        if keys_on_rows:
            o_ref[r] = tile
        else:
            o_ref[r, 0] = tile


def bias_tiles(rel_table, *, keys_on_rows):
    n_heads = rel_table.shape[1]
    if keys_on_rows:
        out_spec = pl.BlockSpec((3, QBLK, QBLK), lambda h: (0, 0, h))
        out_shape = (3, QBLK, n_heads * QBLK)
    else:
        out_spec = pl.BlockSpec((3, 1, QBLK, QBLK), lambda h: (0, h, 0, 0))
        out_shape = (3, n_heads, QBLK, QBLK)
    return pl.pallas_call(
        functools.partial(_bias_tile_kernel, keys_on_rows=keys_on_rows),
        grid=(n_heads,),
        in_specs=[pl.BlockSpec(memory_space=pltpu.SMEM)],
        out_specs=out_spec,
        out_shape=jax.ShapeDtypeStruct(out_shape, F32),
        compiler_params=_cparams(1),
        name="bias_tiles",
    )(rel_table.T.astype(F32))


def _rms(x, g):
    return x * lax.rsqrt(jnp.mean(x * x, axis=-1, keepdims=True) + RMS_EPS) * g


def _modulate(x, g_ref, mod_ref, gi, shift_i, scale_i):
    y = _rms(x, g_ref[gi:gi + 1, :])
    return y * (1.0 + mod_ref[0, scale_i:scale_i + 1, :]) + mod_ref[0, shift_i:shift_i + 1, :]


def _normmod_kernel(x_ref, g_ref, mod_ref, h_ref, *, gi, shift_i, scale_i):
    h_ref[...] = _modulate(x_ref[...], g_ref, mod_ref, gi, shift_i, scale_i).astype(h_ref.dtype)


def normmod(x, gains, mod, seq, *, gi, shift_i, scale_i, tm=256):
    m, d = x.shape
    tpb = seq // tm
    return pl.pallas_call(
        functools.partial(_normmod_kernel, gi=gi, shift_i=shift_i, scale_i=scale_i),
        grid=(m // tm,),
        in_specs=[pl.BlockSpec((tm, d), lambda i: (i, 0)),
                  pl.BlockSpec(gains.shape, lambda i: (0, 0)),
                  pl.BlockSpec((1, N_MOD, d), lambda i: (i // tpb, 0, 0))],
        out_specs=pl.BlockSpec((tm, d), lambda i: (i, 0)),
        out_shape=jax.ShapeDtypeStruct((m, d), BF16),
        compiler_params=_cparams(1),
        name="normmod",
    )(x, gains, mod)


def _resid_kernel(x_ref, y_ref, g_ref, mod_ref, x_out_ref, *, gi, gate_i, coef):
    y = _rms(y_ref[...].astype(F32), g_ref[gi:gi + 1, :])
    x_out_ref[...] = x_ref[...] + coef * mod_ref[0, gate_i:gate_i + 1, :] * y


def resid(x, y, gains, mod, seq, *, gi, gate_i, coef, tm=256):
    m, d = x.shape
    tpb = seq // tm
    row = pl.BlockSpec((tm, d), lambda i: (i, 0))
    return pl.pallas_call(
        functools.partial(_resid_kernel, gi=gi, gate_i=gate_i, coef=coef),
        grid=(m // tm,),
        in_specs=[row, row,
                  pl.BlockSpec(gains.shape, lambda i: (0, 0)),
                  pl.BlockSpec((1, N_MOD, d), lambda i: (i // tpb, 0, 0))],
        out_specs=row,
        out_shape=jax.ShapeDtypeStruct((m, d), F32),
        compiler_params=_cparams(1),
        name="resid",
    )(x, y, gains, mod)


def _resid_normmod_kernel(x_ref, y_ref, g_ref, mod_ref, g2_ref, mod2_ref, x_out_ref, h_ref, *,
                          gi, gate_i, coef, gi2, shift_i, scale_i):
    y = _rms(y_ref[...].astype(F32), g_ref[gi:gi + 1, :])
    x_new = x_ref[...] + coef * mod_ref[0, gate_i:gate_i + 1, :] * y
    x_out_ref[...] = x_new
    h_ref[...] = _modulate(x_new, g2_ref, mod2_ref, gi2, shift_i, scale_i).astype(h_ref.dtype)


def resid_normmod(x, y, gains, mod, gains2, mod2, seq, *, gi, gate_i, coef, gi2, shift_i, scale_i, tm=256):
    m, d = x.shape
    tpb = seq // tm
    row = pl.BlockSpec((tm, d), lambda i: (i, 0))
    gspec = pl.BlockSpec(gains.shape, lambda i: (0, 0))
    mspec = pl.BlockSpec((1, N_MOD, d), lambda i: (i // tpb, 0, 0))
    return pl.pallas_call(
        functools.partial(_resid_normmod_kernel, gi=gi, gate_i=gate_i, coef=coef, gi2=gi2,
                          shift_i=shift_i, scale_i=scale_i),
        grid=(m // tm,),
        in_specs=[row, row, gspec, mspec, gspec, mspec],
        out_specs=[row, row],
        out_shape=[jax.ShapeDtypeStruct((m, d), F32), jax.ShapeDtypeStruct((m, d), BF16)],
        compiler_params=_cparams(1),
        name="resid_normmod",
    )(x, y, gains, mod, gains2, mod2)


def _mm_kernel(a_ref, w_ref, o_ref, *, act):
    acc = _dot(a_ref[...], w_ref[...])
    if act == "silu":
        acc = _silu(acc)
    o_ref[...] = acc.astype(o_ref.dtype)


def _layer_spec(tail_block, tail_index, layer):
    if layer is None:
        return pl.BlockSpec(tail_block, tail_index)
    return pl.BlockSpec((None,) + tail_block, lambda *g: (layer,) + tail_index(*g))


def _mxu_call(body, grid, in_specs, out_spec, out_shape, args, name, cast):
    if cast is None:
        return pl.pallas_call(body, grid=grid, in_specs=in_specs, out_specs=out_spec, out_shape=out_shape,
                              compiler_params=_cparams(2), name=name)(*args)
    src, rows_per_layer, layer = cast
    steps = grid[0] * grid[1]
    rows = rows_per_layer // steps
    assert rows * steps == rows_per_layer and rows % 16 == 0
    cols = src.shape[1]
    n_in = len(in_specs)

    def body_and_cast(*refs):
        refs[-1][...] = refs[n_in][...].astype(BF16)
        body(*refs[:n_in], refs[n_in + 1])

    return pl.pallas_call(
        body_and_cast, grid=grid,
        in_specs=list(in_specs) + [pl.BlockSpec((rows, cols), lambda i, j: (layer * steps + i * grid[1] + j, 0))],
        out_specs=[out_spec, pl.BlockSpec((rows, cols), lambda i, j: (i * grid[1] + j, 0))],
        out_shape=[out_shape, jax.ShapeDtypeStruct((rows_per_layer, cols), BF16)],
        compiler_params=_cparams(2), name=name)(*args, src)


def matmul(a, w, *, out_dtype, layer=None, act=None, cast=None, tm=1024, tn=512):
    m, k = a.shape
    n = w.shape[-1]
    tm, tn = min(tm, m), min(tn, n)
    assert m % tm == 0 and n % tn == 0
    return _mxu_call(
        functools.partial(_mm_kernel, act=act), (m // tm, n // tn),
        [pl.BlockSpec((tm, k), lambda i, j: (i, 0)),
         _layer_spec((k, tn), lambda i, j: (0, j), layer)],
        pl.BlockSpec((tm, tn), lambda i, j: (i, j)),
        jax.ShapeDtypeStruct((m, n), out_dtype), (a, w), "matmul", cast)


def _swiglu_kernel(a_ref, wu_ref, wv_ref, o_ref):
    a = a_ref[...]
    u = _dot(a, wu_ref[...])
    v = _dot(a, wv_ref[...])
    o_ref[...] = (_silu(u) * v).astype(o_ref.dtype)


def swiglu_in(a, w, *, cast=None, tm=1024, tn=512):
    m, k = a.shape
    f = w.shape[-1] // 2
    tm = min(tm, m)
    nf = f // tn
    return _mxu_call(
        _swiglu_kernel, (m // tm, nf),
        [pl.BlockSpec((tm, k), lambda i, j: (i, 0)),
         pl.BlockSpec((k, tn), lambda i, j: (0, j)),
         pl.BlockSpec((k, tn), lambda i, j: (0, j + nf))],
        pl.BlockSpec((tm, tn), lambda i, j: (i, j)),
        jax.ShapeDtypeStruct((m, f), BF16), (a, w, w), "swiglu_in", cast)


def _merge_kernel(h_ref, wga_ref, wgb_ref, wgc_ref, ya_ref, yb_ref, yc_ref, wb_ref, o_ref):
    h = h_ref[...]
    acc = jax.nn.sigmoid(_dot(h, wga_ref[...])) * _dot(ya_ref[...], wb_ref[0])
    acc = acc + jax.nn.sigmoid(_dot(h, wgb_ref[...])) * _dot(yb_ref[...], wb_ref[1])
    acc = acc + jax.nn.sigmoid(_dot(h, wgc_ref[...])) * _dot(yc_ref[...], wb_ref[2])
    o_ref[...] = acc.astype(o_ref.dtype)


def merge(h, w_gates, layer, ya, yb, yc, w_branch, *, cast=None, tm=1024, tn=256):
    m, k = h.shape
    kw = ya.shape[1]
    d = w_branch.shape[-1]
    tm = min(tm, m)
    nd = d // tn
    y_spec = pl.BlockSpec((tm, kw), lambda i, j: (i, 0))
    return _mxu_call(
        _merge_kernel, (m // tm, nd),
        [pl.BlockSpec((tm, k), lambda i, j: (i, 0)),
         _layer_spec((k, tn), lambda i, j: (0, j), layer),
         _layer_spec((k, tn), lambda i, j: (0, j + nd), layer),
         _layer_spec((k, tn), lambda i, j: (0, j + 2 * nd), layer),
         y_spec, y_spec, y_spec,
         pl.BlockSpec((N_BRANCH, kw, tn), lambda i, j: (0, 0, j))],
        pl.BlockSpec((tm, tn), lambda i, j: (i, j)),
        jax.ShapeDtypeStruct((m, d), BF16),
        (h, w_gates, w_gates, w_gates, ya, yb, yc, w_branch), "merge", cast)


def _kvup_kernel(lat_ref, g_ref, w_ref, k_ref, vt_ref):
    h = _rms(lat_ref[...], g_ref[...]).astype(BF16)
    kv = _dot(h, w_ref[...])
    k_ref[...] = kv[:, :B_DH].astype(k_ref.dtype)
    v = kv[:, B_DH:]
    for u in range(vt_ref.shape[0]):
        vt_ref[u] = v[u * QBLK:(u + 1) * QBLK].T.astype(vt_ref.dtype)


def kv_up(bproj, lat_block, kv_norm_g, w_kv_up, layer, *, tm=512):
    m = bproj.shape[0]
    return pl.pallas_call(
        _kvup_kernel,
        grid=(m // tm,),
        in_specs=[pl.BlockSpec((tm, B_KV_RANK), lambda i: (i, lat_block)),
                  pl.BlockSpec((1, B_KV_RANK), lambda i: (0, 0)),
                  _layer_spec((B_KV_RANK, 2 * B_DH), lambda i: (0, 0), layer)],
        out_specs=[pl.BlockSpec((tm, B_DH), lambda i: (i, 0)),
                   pl.BlockSpec((tm // QBLK, B_DH, QBLK), lambda i: (i, 0, 0))],
        out_shape=[jax.ShapeDtypeStruct((m, B_DH), BF16),
                   jax.ShapeDtypeStruct((m // QBLK, B_DH, QBLK), BF16)],
        compiler_params=_cparams(1),
        name="kv_up",
    )(bproj, kv_norm_g.reshape(1, B_KV_RANK), w_kv_up)


def _hgrn_chunk(q, fr, ir, gr, lb, ng, st, tril):
    cs, sub = HGRN_CHUNK, HGRN_SUB
    f = lb + (1.0 - lb) * jax.nn.sigmoid(fr)
    lf = jnp.log(jnp.maximum(f, 1e-20))
    k = 1.0 - f
    v = _silu(ir)
    b = jnp.dot(tril, lf, precision=lax.Precision.HIGHEST, preferred_element_type=F32) * LOG2_E
    o_inter = _dot_nt((q * jnp.exp2(b)).astype(BF16), st.astype(BF16))

    col_s = lax.broadcasted_iota(jnp.int32, (sub, cs), 1)
    blocks = []
    for si in range(cs // sub):
        lo = si * sub
        q_s = q[lo:lo + sub]
        b_s = b[lo:lo + sub]
        if si > 0:
            bref = b[lo - 1:lo]
            qt = (q_s * jnp.exp2(b_s - bref)).astype(BF16)
            kj = (k[:lo] * jnp.exp2(bref - b[:lo])).astype(BF16)
            attn = jnp.concatenate([_dot_nt(qt, kj), jnp.zeros((sub, cs - lo), F32)], axis=1)
        else:
            attn = jnp.zeros((sub, cs), F32)
        for s in range(sub):
            a = q_s * k[lo + s:lo + s + 1] * jnp.exp2(b_s - b[lo + s:lo + s + 1])
            attn = jnp.where(col_s == lo + s, jnp.sum(a, axis=-1, keepdims=True), attn)
        blocks.append(attn)
    attn = jnp.where(tril > 0.0, jnp.concatenate(blocks, axis=0), 0.0)
    o = o_inter + _dot(attn.astype(BF16), v.astype(BF16))

    b_end = b[cs - 1:cs]
    kd = (k * jnp.exp2(b_end - b)).astype(BF16)
    upd = _dot(v.T.astype(BF16), kd)
    st_new = st * jnp.exp2(b_end) + upd
    return _rms(o, ng) * _silu(gr), st_new


def _hgrn_kernel(q_ref, f_ref, i_ref, g_ref, lb_ref, ng_ref, o_ref, st_ref, *, seq):
    cs = HGRN_CHUNK
    st_ref[...] = jnp.zeros_like(st_ref)
    ng = ng_ref[...]
    row_c = lax.broadcasted_iota(jnp.int32, (cs, cs), 0)
    col_c = lax.broadcasted_iota(jnp.int32, (cs, cs), 1)
    tril = (col_c <= row_c).astype(F32)

    def chunk(c, carry):
        r0 = pl.multiple_of(c * cs, cs)
        for h in range(HGRN_HEADS_PER_STEP):
            cols = slice(h * A_DK, (h + 1) * A_DK)
            o, st_new = _hgrn_chunk(q_ref[pl.ds(r0, cs), cols], f_ref[pl.ds(r0, cs), cols],
                                    i_ref[pl.ds(r0, cs), cols], g_ref[pl.ds(r0, cs), cols],
                                    lb_ref[:, cols], ng, st_ref[h], tril)
            st_ref[h] = st_new
            o_ref[pl.ds(r0, cs), cols] = o.astype(o_ref.dtype)
        return carry

    lax.fori_loop(0, seq // cs, chunk, 0, unroll=2)


def hgrn2(aproj, lb, norm_g, seq):
    m = aproj.shape[0]
    bsz = m // seq
    nh = HGRN_HEADS_PER_STEP
    steps = A_HEADS // nh

    def col(part):
        return pl.BlockSpec((seq, nh * A_DK), lambda b, h: (b, part * steps + h))

    return pl.pallas_call(
        functools.partial(_hgrn_kernel, seq=seq),
        grid=(bsz, steps),
        in_specs=[col(0), col(1), col(2), col(3),
                  pl.BlockSpec((1, nh * A_DK), lambda b, h: (0, h)),
                  pl.BlockSpec((1, A_DV), lambda b, h: (0, 0))],
        out_specs=pl.BlockSpec((seq, nh * A_DV), lambda b, h: (b, h)),
        out_shape=jax.ShapeDtypeStruct((m, A_HEADS * A_DV), BF16),
        scratch_shapes=[pltpu.VMEM((nh, A_DV, A_DK), F32)],
        compiler_params=_cparams(2),
        name="hgrn2",
    )(aproj, aproj, aproj, aproj, lb.reshape(1, A_WIDTH), norm_g.reshape(1, A_DV))


def _dsa_kernel(q_ref, iq_ref, smq_ref, smk_ref, k_ref, vt_ref, bt_ref, o_ref,
                key_ref, msk_ref, lg_ref, mx_ref, ls_ref, acc_ref, *, nblk, topk):
    j = pl.program_id(1)
    row = lax.broadcasted_iota(jnp.int32, (QBLK, QBLK), 0)
    col = lax.broadcasted_iota(jnp.int32, (QBLK, QBLK), 1)
    causal = row <= col
    n_grp = j // TOPK_GROUP + 1
    n_pair = j // 2 + 1
    int_min = jnp.int32(-2 ** 31)

    def sort_key(score):
        bits = lax.bitcast_convert_type(score, jnp.int32)
        return bits ^ ((bits >> 31) & jnp.int32(0x7FFFFFFF))

    iq = iq_ref[...].astype(BF16)
    iqs = jnp.concatenate([iq[:, h * IDX_DH:(h + 1) * IDX_DH] for h in range(IDX_HEADS)], axis=0)
    iwt = (smq_ref[:, IDX_DH:IDX_DH + IDX_HEADS] * (IDX_HEADS ** -0.5 * IDX_DH ** -0.5)).T

    def visible(kb):
        return jnp.logical_or(kb < j, jnp.logical_and(kb == j, causal))

    def score_tile(kb):
        r0 = pl.multiple_of(kb * QBLK, QBLK)
        ik = smk_ref[pl.ds(r0, QBLK), 0:IDX_DH].astype(BF16)
        d = _dot_nt(ik, iqs)
        sc = jnp.zeros((QBLK, QBLK), F32)
        for h in range(IDX_HEADS):
            sc = sc + jnp.maximum(d[:, h * QBLK:(h + 1) * QBLK], 0.0) * iwt[h:h + 1, :]
        key_ref[kb] = sort_key(jnp.where(visible(kb), sc, NEG_INF))

    def pairwise(tile_fn):
        def step(kp, carry):
            tile_fn(2 * kp)
            tile_fn(2 * kp + 1)
            return carry
        lax.fori_loop(0, n_pair, step, 0)

    def groupwise(tile_fn):
        def step(gi, carry):
            for u in range(TOPK_GROUP):
                tile_fn(gi * TOPK_GROUP + u)
            return carry
        lax.fori_loop(0, n_grp, step, 0)

    pairwise(score_tile)

    def fill_step(kb, carry):
        key_ref[kb] = sort_key(jnp.full((QBLK, QBLK), NEG_INF, F32))
        return carry

    lax.fori_loop(2 * n_pair, n_grp * TOPK_GROUP, fill_step, 0)

    kf = jnp.float32(topk)

    def count(pred):
        def grp(gi, c):
            for u in range(TOPK_GROUP):
                kb = gi * TOPK_GROUP + u
                c = c + jnp.where(pred(kb, key_ref[kb]), 1.0, 0.0)
            return c
        c = lax.fori_loop(0, n_grp, grp, jnp.zeros((QBLK, QBLK), F32))
        return jnp.sum(c, axis=0, keepdims=True)

    def thr_step(it, cand):
        trial = cand | lax.shift_left(jnp.int32(1), 31 - it)
        t_s = trial ^ int_min
        cnt = count(lambda kb, key: key >= t_s)
        return jnp.where(cnt >= kf, trial, cand)

    cand = lax.fori_loop(0, 32, thr_step, jnp.zeros((1, QBLK), jnp.int32))
    thr = cand ^ int_min
    n_gt = count(lambda kb, key: key > thr)
    n_ge = count(lambda kb, key: key >= thr)
    need = kf - n_gt
    nbits = (nblk * QBLK - 1).bit_length()

    def tie_search():
        def idx_step(it, pos):
            trial = pos + lax.shift_left(jnp.int32(1), nbits - 1 - it)
            cnt = count(lambda kb, key: jnp.logical_and(key == thr, kb * QBLK + row < trial))
            return jnp.where(cnt < need, trial, pos)
        return lax.fori_loop(0, nbits, idx_step, jnp.zeros((1, QBLK), jnp.int32))

    has_ties = jnp.max(n_ge) > kf
    pos = lax.cond(has_ties, tie_search, lambda: jnp.full((1, QBLK), nblk * QBLK, jnp.int32))

    def mask_tile(kb):
        key = key_ref[kb]
        sel = jnp.logical_or(key > thr, jnp.logical_and(key == thr, kb * QBLK + row <= pos))
        msk_ref[kb] = jnp.where(jnp.logical_and(sel, visible(kb)), 0.0, NEG_INF)

    groupwise(mask_tile)

    q = q_ref[...].astype(BF16)
    qs = jnp.concatenate([q[:, h * B_DH:(h + 1) * B_DH] for h in range(B_HEADS)], axis=0)
    scale = B_DH ** -0.5
    mx_ref[...] = jnp.full(mx_ref.shape, NEG_INF, F32)

    def logit_tile(kb):
        r0 = pl.multiple_of(kb * QBLK, QBLK)
        lg = _dot_nt(k_ref[pl.ds(r0, QBLK), :], qs)
        msk = msk_ref[kb]
        lg = lg * scale + bt_ref[jnp.clip(j - kb, 0, 2)] + jnp.concatenate([msk] * B_HEADS, axis=1)
        lg_ref[kb] = lg
        mx_ref[...] = jnp.maximum(mx_ref[...], lg)

    groupwise(logit_tile)
    mx = jnp.max(mx_ref[...], axis=0, keepdims=True)
    ls_ref[...] = jnp.zeros_like(ls_ref)
    acc_ref[...] = jnp.zeros_like(acc_ref)

    def pv_tile(kb):
        p = jnp.exp(lg_ref[kb] - mx)
        ls_ref[...] += p
        acc_ref[...] += _dot(vt_ref[kb], p.astype(BF16))

    groupwise(pv_tile)
    out_t = acc_ref[...] / jnp.sum(ls_ref[...], axis=0, keepdims=True)
    for h in range(B_HEADS):
        o_ref[:, h * B_DH:(h + 1) * B_DH] = out_t[:, h * QBLK:(h + 1) * QBLK].T.astype(o_ref.dtype)


def dsa(bproj, k, vt, bias_b, seq):
    m = bproj.shape[0]
    bsz = m // seq
    nblk = seq // QBLK
    assert nblk % TOPK_GROUP == 0
    topk = min(TOPK_MAX, seq // 4)
    qw = B_HEADS * B_DH
    small_blk = (2 * qw + B_KV_RANK) // B_SMALL_W
    cols = B_HEADS * QBLK
    return pl.pallas_call(
        functools.partial(_dsa_kernel, nblk=nblk, topk=topk),
        grid=(bsz, nblk),
        in_specs=[pl.BlockSpec((QBLK, qw), lambda b, j: (b * nblk + j, 0)),
                  pl.BlockSpec((QBLK, qw), lambda b, j: (b * nblk + j, 1)),
                  pl.BlockSpec((QBLK, B_SMALL_W), lambda b, j: (b * nblk + j, small_blk)),
                  pl.BlockSpec((seq, B_SMALL_W), lambda b, j: (b, small_blk)),
                  pl.BlockSpec((seq, B_DH), lambda b, j: (b, 0)),
                  pl.BlockSpec((nblk, B_DH, QBLK), lambda b, j: (b, 0, 0)),
                  pl.BlockSpec((3, QBLK, cols), lambda b, j: (0, 0, 0))],
        out_specs=pl.BlockSpec((QBLK, qw), lambda b, j: (b * nblk + j, 0)),
        out_shape=jax.ShapeDtypeStruct((m, qw), BF16),
        scratch_shapes=[pltpu.VMEM((nblk, QBLK, QBLK), jnp.int32),
                        pltpu.VMEM((nblk, QBLK, QBLK), F32),
                        pltpu.VMEM((nblk, QBLK, cols), F32),
                        pltpu.VMEM((QBLK, cols), F32),
                        pltpu.VMEM((QBLK, cols), F32),
                        pltpu.VMEM((B_DH, cols), F32)],
        compiler_params=_cparams(2),
        name="dsa",
    )(bproj, bproj, bproj, bproj, k, vt, bias_b)


def _swa_kernel(sink_ref, q_ref, kc_ref, vc_ref, kp_ref, vp_ref, bt_ref, o_ref):
    j = pl.program_id(1)
    grp = C_HEADS // C_KV_HEADS
    row = lax.broadcasted_iota(jnp.int32, (QBLK, QBLK), 0)
    col = lax.broadcasted_iota(jnp.int32, (QBLK, QBLK), 1)
    cur_ok = jnp.concatenate([row <= col] * grp, axis=1)
    prev_ok = jnp.concatenate([jnp.logical_and(row > col, j > 0)] * grp, axis=1)
    q = q_ref[...].astype(BF16)
    scale = C_DH ** -0.5
    for g in range(C_KV_HEADS):
        heads = range(g * grp, (g + 1) * grp)
        kv_cols = slice(g * C_DH, (g + 1) * C_DH)
        lanes = slice(g * grp * QBLK, (g + 1) * grp * QBLK)
        qs = jnp.concatenate([q[:, h * C_DH:(h + 1) * C_DH] for h in heads], axis=0)
        sink = jnp.concatenate([jnp.full((1, QBLK), sink_ref[h], F32) for h in heads], axis=1)
        lc = _dot_nt(kc_ref[:, kv_cols].astype(BF16), qs)
        lp = _dot_nt(kp_ref[:, kv_cols].astype(BF16), qs)
        lc = jnp.where(cur_ok, lc * scale + bt_ref[0, :, lanes], NEG_INF)
        lp = jnp.where(prev_ok, lp * scale + bt_ref[1, :, lanes], NEG_INF)
        mx = jnp.maximum(jnp.maximum(jnp.max(lc, axis=0, keepdims=True),
                                     jnp.max(lp, axis=0, keepdims=True)), sink)
        pc = jnp.exp(lc - mx)
        pp = jnp.exp(lp - mx)
        den = (jnp.sum(pc, axis=0, keepdims=True) + jnp.sum(pp, axis=0, keepdims=True)
               + jnp.exp(sink - mx))
        o_t = (_dot(vc_ref[:, kv_cols].T.astype(BF16), pc.astype(BF16))
               + _dot(vp_ref[:, kv_cols].T.astype(BF16), pp.astype(BF16))) / den
        for u in range(0, grp, 2):
            pair = jnp.concatenate([o_t[:, u * QBLK:(u + 1) * QBLK],
                                    o_t[:, (u + 1) * QBLK:(u + 2) * QBLK]], axis=0)
            h0 = g * grp + u
            o_ref[:, h0 * C_DH:(h0 + 2) * C_DH] = pair.T.astype(o_ref.dtype)


def swa(cproj, sinks, bias_c, seq):
    m = cproj.shape[0]
    bsz = m // seq
    nblk = seq // QBLK
    qw = C_HEADS * C_DH
    kvw = C_KV_HEADS * C_DH
    kblk = qw // kvw

    def cur(off):
        return pl.BlockSpec((QBLK, kvw), lambda b, j: (b * nblk + j, kblk + off))

    def prev(off):
        return pl.BlockSpec((QBLK, kvw), lambda b, j: (b * nblk + jnp.maximum(j - 1, 0), kblk + off))

    return pl.pallas_call(
        _swa_kernel,
        grid=(bsz, nblk),
        in_specs=[pl.BlockSpec(memory_space=pltpu.SMEM),
                  pl.BlockSpec((QBLK, qw), lambda b, j: (b * nblk + j, 0)),
                  cur(0), cur(1), prev(0), prev(1),
                  pl.BlockSpec((3, QBLK, C_HEADS * QBLK), lambda b, j: (0, 0, 0))],
        out_specs=pl.BlockSpec((QBLK, qw), lambda b, j: (b * nblk + j, 0)),
        out_shape=jax.ShapeDtypeStruct((m, qw), BF16),
        compiler_params=_cparams(2),
        name="swa",
    )(sinks.astype(F32), cproj, cproj, cproj, cproj, cproj, bias_c)


def _pack_w_in(w):
    o = 0
    parts = {}
    for name, width in (("a", 4 * A_WIDTH), ("bq", B_HEADS * B_DH), ("blat", B_KV_RANK),
                        ("biq", IDX_HEADS * IDX_DH), ("bik", IDX_DH), ("biw", IDX_HEADS),
                        ("c", (C_HEADS + 2 * C_KV_HEADS) * C_DH)):
        parts[name] = w[..., o:o + width]
        o += width
    gates = w[..., o:]
    pad = jnp.zeros(w.shape[:-1] + (B_SMALL_W - IDX_DH - IDX_HEADS,), w.dtype)
    wb = jnp.concatenate([parts["bq"], parts["biq"], parts["blat"], parts["bik"], parts["biw"], pad], axis=-1)
    return tuple(t.astype(BF16) for t in (parts["a"], wb, parts["c"], gates))


def kernel(x, c, w_c_down, w_c_up, norm_gains, w_in, lb_logits, hgrn_norm, kv_norm, w_kv_up, rel_table,
           sinks, w_branch, w_out, ffn1_in, ffn1_out, ffn2_in, ffn2_out):
    bsz, seq, d = x.shape
    depth = w_in.shape[0]
    m = bsz * seq
    x = x.reshape(m, d).astype(F32)

    cond = matmul(c.astype(BF16), w_c_down.astype(BF16), out_dtype=BF16, act="silu")
    lbs = lower_bounds(lb_logits)
    bias_b = bias_tiles(rel_table[:, :B_HEADS], keys_on_rows=True)
    bias_c = bias_tiles(rel_table[:, B_HEADS:], keys_on_rows=True)
    lat_block = 2 * B_HEADS * B_DH // B_KV_RANK
    w_c_up, w_kv_up = w_c_up.astype(BF16), w_kv_up.astype(BF16)
    wa, wb, wc, wg = _pack_w_in(w_in)
    mods = [matmul(cond, w_c_up, layer=l, out_dtype=F32, tn=4096).reshape(bsz, N_MOD, d)
            for l in range(depth)]
    gains = norm_gains.astype(F32)

    def rows(w):
        return w.reshape(-1, w.shape[-1])
    k_ff, k_out = ffn1_out.shape[1], w_out.shape[1]
    branch_w = BRANCH_W * 2
    branch_rows = w_branch[0].size // branch_w
    src_ffn1_in, src_ffn1_out, src_ffn2_in, src_ffn2_out, src_out = (
        rows(t) for t in (ffn1_in, ffn1_out, ffn2_in, ffn2_out, w_out))
    src_branch = w_branch.reshape(-1, branch_w)

    w1_in = ffn1_in[0].astype(BF16)
    h = normmod(x, gains[0], mods[0], seq, gi=0, shift_i=0, scale_i=1)
    for l in range(depth):
        mod, g = mods[l], gains[l]

        u, w1_out = swiglu_in(h, w1_in, cast=(src_ffn1_out, k_ff, l))
        y, wo = matmul(u, w1_out, out_dtype=BF16, cast=(src_out, k_out, l))
        x, h = resid_normmod(x, y, g, mod, g, mod, seq, gi=1, gate_i=2, coef=FFN_RES,
                             gi2=2, shift_i=3, scale_i=4)

        aproj, w_br = matmul(h, wa, layer=l, out_dtype=F32, cast=(src_branch, branch_rows, l))
        bproj = matmul(h, wb, layer=l, out_dtype=F32, tn=wb.shape[-1] // 3)
        cproj = matmul(h, wc, layer=l, out_dtype=BF16, tn=wc.shape[-1] // 2)
        ya = hgrn2(aproj, lbs[l], hgrn_norm[l].astype(F32), seq)
        kb, vb = kv_up(bproj, lat_block, kv_norm[l].astype(F32), w_kv_up, l)
        yb = dsa(bproj, kb, vb, bias_b, seq)
        yc = swa(cproj, sinks[l], bias_c, seq)
        mix, w2_in = merge(h, wg, l, ya, yb, yc, w_br.reshape(w_branch.shape[1:]),
                           cast=(src_ffn2_in, d, l))
        y, w2_out = matmul(mix, wo, out_dtype=BF16, cast=(src_ffn2_out, k_ff, l))
        x, h = resid_normmod(x, y, g, mod, g, mod, seq, gi=3, gate_i=5, coef=1.0,
                             gi2=4, shift_i=6, scale_i=7)

        if l + 1 < depth:
            u, w1_in = swiglu_in(h, w2_in, cast=(src_ffn1_in, d, l + 1))
            y = matmul(u, w2_out, out_dtype=BF16)
            x, h = resid_normmod(x, y, g, mod, gains[l + 1], mods[l + 1], seq, gi=5, gate_i=8,
                                 coef=FFN_RES, gi2=0, shift_i=0, scale_i=1)
        else:
            u = swiglu_in(h, w2_in)
            y = matmul(u, w2_out, out_dtype=BF16)
            x = resid(x, y, g, mod, seq, gi=5, gate_i=8, coef=FFN_RES)
    return x.reshape(bsz, seq, d)
```

```python
import functools
import math

import jax
import jax.numpy as jnp
from jax import lax
from jax.experimental import pallas as pl
from jax.experimental.pallas import tpu as pltpu

A_HEADS = 8
A_DK = 128
A_DV = 128
A_WIDTH = A_HEADS * A_DK
B_HEADS = 8
B_DH = 128
B_KV_RANK = 512
IDX_HEADS = 16
IDX_DH = 64
TOPK_MAX = 256
C_HEADS = 16
C_KV_HEADS = 2
C_DH = 64
WINDOW = 128
REL_BUCKETS = 32
REL_MAX_DIST = 128
N_BRANCH = 3
BRANCH_W = 1024
FFN_RES = 0.5
N_MOD = 9
RMS_EPS = 1e-6
NEG_INF = -1e30
LOG2_E = math.log2(math.e)

VMEM_LIMIT_BYTES = 56 * 1024 * 1024

QBLK = 128
HGRN_CHUNK = 64
HGRN_SUB = 16
HGRN_HEADS_PER_STEP = 4
TOPK_GROUP = 4
B_SMALL_W = 128

BF16 = jnp.bfloat16
F32 = jnp.float32


def _cparams(n_grid_dims):
    return pltpu.CompilerParams(dimension_semantics=("arbitrary",) * n_grid_dims,
                                vmem_limit_bytes=VMEM_LIMIT_BYTES)


def _dot(a, b):
    return jnp.dot(a, b, preferred_element_type=F32)


def _dot_nt(a, b):
    return lax.dot_general(a, b, (((1,), (1,)), ((), ())), preferred_element_type=F32)


def _silu(x):
    return x * jax.nn.sigmoid(x)


def _lower_bound_kernel(x_ref, o_ref):
    x = x_ref[...]
    m = jnp.max(x, axis=0, keepdims=True)
    e = jnp.exp(x - m)
    p = e / jnp.sum(e, axis=0, keepdims=True)
    depth = x.shape[0]
    run = jnp.zeros_like(p[0:1])
    for l in range(depth):
        o_ref[l:l + 1, :] = run
        if l + 1 < depth:
            run = run + p[l + 1:l + 2]


def lower_bounds(lb_logits):
    return pl.pallas_call(
        _lower_bound_kernel,
        out_shape=jax.ShapeDtypeStruct(lb_logits.shape, F32),
        name="lower_bounds",
    )(lb_logits.astype(F32))


def _t5_bucket(dist):
    max_exact = REL_BUCKETS // 2
    d = jnp.maximum(dist, 0)
    df = jnp.maximum(d, 1).astype(F32)
    large = max_exact + (jnp.log(df / max_exact) / math.log(REL_MAX_DIST / max_exact)
                         * (REL_BUCKETS - max_exact)).astype(jnp.int32)
    large = jnp.minimum(large, REL_BUCKETS - 1)
    return jnp.where(d < max_exact, d, large)


def _bias_tile_kernel(tab_ref, o_ref, *, keys_on_rows):
    h = pl.program_id(0)
    row = lax.broadcasted_iota(jnp.int32, (QBLK, QBLK), 0)
    col = lax.broadcasted_iota(jnp.int32, (QBLK, QBLK), 1)
    t_minus_s = col - row if keys_on_rows else row - col
    for r in range(3):
        bucket = _t5_bucket(r * QBLK + t_minus_s)
        tile = jnp.zeros((QBLK, QBLK), F32)
        for b in range(REL_BUCKETS):
            tile = jnp.where(bucket == b, tab_ref[h, b], tile)
        if keys_on_rows:
            o_ref[r] = tile
        else:
            o_ref[r, 0] = tile


def bias_tiles(rel_table, *, keys_on_rows):
    n_heads = rel_table.shape[1]
    if keys_on_rows:
        out_spec = pl.BlockSpec((3, QBLK, QBLK), lambda h: (0, 0, h))
        out_shape = (3, QBLK, n_heads * QBLK)
    else:
        out_spec = pl.BlockSpec((3, 1, QBLK, QBLK), lambda h: (0, h, 0, 0))
        out_shape = (3, n_heads, QBLK, QBLK)
    return pl.pallas_call(
        functools.partial(_bias_tile_kernel, keys_on_rows=keys_on_rows),
        grid=(n_heads,),
        in_specs=[pl.BlockSpec(memory_space=pltpu.SMEM)],
        out_specs=out_spec,
        out_shape=jax.ShapeDtypeStruct(out_shape, F32),
        compiler_params=_cparams(1),
        name="bias_tiles",
    )(rel_table.T.astype(F32))


def _rms(x, g):
    return x * lax.rsqrt(jnp.mean(x * x, axis=-1, keepdims=True) + RMS_EPS) * g


def _modulate(x, g_ref, mod_ref, gi, shift_i, scale_i):
    y = _rms(x, g_ref[gi:gi + 1, :])
    return y * (1.0 + mod_ref[0, scale_i:scale_i + 1, :]) + mod_ref[0, shift_i:shift_i + 1, :]


def _normmod_kernel(x_ref, g_ref, mod_ref, h_ref, *, gi, shift_i, scale_i):
    h_ref[...] = _modulate(x_ref[...], g_ref, mod_ref, gi, shift_i, scale_i).astype(h_ref.dtype)


def normmod(x, gains, mod, seq, *, gi, shift_i, scale_i, tm=256):
    m, d = x.shape
    tpb = seq // tm
    return pl.pallas_call(
        functools.partial(_normmod_kernel, gi=gi, shift_i=shift_i, scale_i=scale_i),
        grid=(m // tm,),
        in_specs=[pl.BlockSpec((tm, d), lambda i: (i, 0)),
                  pl.BlockSpec(gains.shape, lambda i: (0, 0)),
                  pl.BlockSpec((1, N_MOD, d), lambda i: (i // tpb, 0, 0))],
        out_specs=pl.BlockSpec((tm, d), lambda i: (i, 0)),
        out_shape=jax.ShapeDtypeStruct((m, d), BF16),
        compiler_params=_cparams(1),
        name="normmod",
    )(x, gains, mod)


def _resid_kernel(x_ref, y_ref, g_ref, mod_ref, x_out_ref, *, gi, gate_i, coef):
    y = _rms(y_ref[...].astype(F32), g_ref[gi:gi + 1, :])
    x_out_ref[...] = x_ref[...] + coef * mod_ref[0, gate_i:gate_i + 1, :] * y


def resid(x, y, gains, mod, seq, *, gi, gate_i, coef, tm=256):
    m, d = x.shape
    tpb = seq // tm
    row = pl.BlockSpec((tm, d), lambda i: (i, 0))
    return pl.pallas_call(
        functools.partial(_resid_kernel, gi=gi, gate_i=gate_i, coef=coef),
        grid=(m // tm,),
        in_specs=[row, row,
                  pl.BlockSpec(gains.shape, lambda i: (0, 0)),
                  pl.BlockSpec((1, N_MOD, d), lambda i: (i // tpb, 0, 0))],
        out_specs=row,
        out_shape=jax.ShapeDtypeStruct((m, d), F32),
        compiler_params=_cparams(1),
        name="resid",
    )(x, y, gains, mod)


def _resid_normmod_kernel(x_ref, y_ref, g_ref, mod_ref, g2_ref, mod2_ref, x_out_ref, h_ref, *,
                          gi, gate_i, coef, gi2, shift_i, scale_i):
    y = _rms(y_ref[...].astype(F32), g_ref[gi:gi + 1, :])
    x_new = x_ref[...] + coef * mod_ref[0, gate_i:gate_i + 1, :] * y
    x_out_ref[...] = x_new
    h_ref[...] = _modulate(x_new, g2_ref, mod2_ref, gi2, shift_i, scale_i).astype(h_ref.dtype)


def resid_normmod(x, y, gains, mod, gains2, mod2, seq, *, gi, gate_i, coef, gi2, shift_i, scale_i, tm=256):
    m, d = x.shape
    tpb = seq // tm
    row = pl.BlockSpec((tm, d), lambda i: (i, 0))
    gspec = pl.BlockSpec(gains.shape, lambda i: (0, 0))
    mspec = pl.BlockSpec((1, N_MOD, d), lambda i: (i // tpb, 0, 0))
    return pl.pallas_call(
        functools.partial(_resid_normmod_kernel, gi=gi, gate_i=gate_i, coef=coef, gi2=gi2,
                          shift_i=shift_i, scale_i=scale_i),
        grid=(m // tm,),
        in_specs=[row, row, gspec, mspec, gspec, mspec],
        out_specs=[row, row],
        out_shape=[jax.ShapeDtypeStruct((m, d), F32), jax.ShapeDtypeStruct((m, d), BF16)],
        compiler_params=_cparams(1),
        name="resid_normmod",
    )(x, y, gains, mod, gains2, mod2)


def _mm_kernel(a_ref, w_ref, o_ref, *, act):
    acc = _dot(a_ref[...], w_ref[...])
    if act == "silu":
        acc = _silu(acc)
    o_ref[...] = acc.astype(o_ref.dtype)


def _layer_spec(tail_block, tail_index, layer):
    if layer is None:
        return pl.BlockSpec(tail_block, tail_index)
    return pl.BlockSpec((None,) + tail_block, lambda *g: (layer,) + tail_index(*g))


def _mxu_call(body, grid, in_specs, out_spec, out_shape, args, name, cast):
    if cast is None:
        return pl.pallas_call(body, grid=grid, in_specs=in_specs, out_specs=out_spec, out_shape=out_shape,
                              compiler_params=_cparams(2), name=name)(*args)
    src, rows_per_layer, layer = cast
    steps = grid[0] * grid[1]
    rows = rows_per_layer // steps
    assert rows * steps == rows_per_layer and rows % 16 == 0
    cols = src.shape[1]
    n_in = len(in_specs)

    def body_and_cast(*refs):
        refs[-1][...] = refs[n_in][...].astype(BF16)
        body(*refs[:n_in], refs[n_in + 1])

    return pl.pallas_call(
        body_and_cast, grid=grid,
        in_specs=list(in_specs) + [pl.BlockSpec((rows, cols), lambda i, j: (layer * steps + i * grid[1] + j, 0))],
        out_specs=[out_spec, pl.BlockSpec((rows, cols), lambda i, j: (i * grid[1] + j, 0))],
        out_shape=[out_shape, jax.ShapeDtypeStruct((rows_per_layer, cols), BF16)],
        compiler_params=_cparams(2), name=name)(*args, src)


def matmul(a, w, *, out_dtype, layer=None, act=None, cast=None, tm=1024, tn=512):
    m, k = a.shape
    n = w.shape[-1]
    tm, tn = min(tm, m), min(tn, n)
    assert m % tm == 0 and n % tn == 0
    return _mxu_call(
        functools.partial(_mm_kernel, act=act), (m // tm, n // tn),
        [pl.BlockSpec((tm, k), lambda i, j: (i, 0)),
         _layer_spec((k, tn), lambda i, j: (0, j), layer)],
        pl.BlockSpec((tm, tn), lambda i, j: (i, j)),
        jax.ShapeDtypeStruct((m, n), out_dtype), (a, w), "matmul", cast)


def _swiglu_kernel(a_ref, wu_ref, wv_ref, o_ref):
    a = a_ref[...]
    u = _dot(a, wu_ref[...])
    v = _dot(a, wv_ref[...])
    o_ref[...] = (_silu(u) * v).astype(o_ref.dtype)


def swiglu_in(a, w, *, cast=None, tm=1024, tn=512):
    m, k = a.shape
    f = w.shape[-1] // 2
    tm = min(tm, m)
    nf = f // tn
    return _mxu_call(
        _swiglu_kernel, (m // tm, nf),
        [pl.BlockSpec((tm, k), lambda i, j: (i, 0)),
         pl.BlockSpec((k, tn), lambda i, j: (0, j)),
         pl.BlockSpec((k, tn), lambda i, j: (0, j + nf))],
        pl.BlockSpec((tm, tn), lambda i, j: (i, j)),
        jax.ShapeDtypeStruct((m, f), BF16), (a, w, w), "swiglu_in", cast)


def _merge_kernel(h_ref, wga_ref, wgb_ref, wgc_ref, ya_ref, yb_ref, yc_ref, wb_ref, o_ref):
    h = h_ref[...]
    acc = jax.nn.sigmoid(_dot(h, wga_ref[...])) * _dot(ya_ref[...], wb_ref[0])
    acc = acc + jax.nn.sigmoid(_dot(h, wgb_ref[...])) * _dot(yb_ref[...], wb_ref[1])
    acc = acc + jax.nn.sigmoid(_dot(h, wgc_ref[...])) * _dot(yc_ref[...], wb_ref[2])
    o_ref[...] = acc.astype(o_ref.dtype)


def merge(h, w_gates, ya, yb, yc, w_branch, *, cast=None, tm=1024, tn=256):
    m, k = h.shape
    kw = ya.shape[1]
    d = w_branch.shape[-1]
    tm = min(tm, m)
    nd = d // tn
    y_spec = pl.BlockSpec((tm, kw), lambda i, j: (i, 0))
    return _mxu_call(
        _merge_kernel, (m // tm, nd),
        [pl.BlockSpec((tm, k), lambda i, j: (i, 0)),
         pl.BlockSpec((k, tn), lambda i, j: (0, j)),
         pl.BlockSpec((k, tn), lambda i, j: (0, j + nd)),
         pl.BlockSpec((k, tn), lambda i, j: (0, j + 2 * nd)),
         y_spec, y_spec, y_spec,
         pl.BlockSpec((N_BRANCH, kw, tn), lambda i, j: (0, 0, j))],
        pl.BlockSpec((tm, tn), lambda i, j: (i, j)),
        jax.ShapeDtypeStruct((m, d), BF16),
        (h, w_gates, w_gates, w_gates, ya, yb, yc, w_branch), "merge", cast)


def _kvup_kernel(lat_ref, g_ref, w_ref, k_ref, vt_ref):
    h = _rms(lat_ref[...], g_ref[...]).astype(BF16)
    kv = _dot(h, w_ref[...])
    k_ref[...] = kv[:, :B_DH].astype(k_ref.dtype)
    v = kv[:, B_DH:]
    for u in range(vt_ref.shape[0]):
        vt_ref[u] = v[u * QBLK:(u + 1) * QBLK].T.astype(vt_ref.dtype)


def kv_up(bproj, lat_block, kv_norm_g, w_kv_up, layer, *, tm=512):
    m = bproj.shape[0]
    return pl.pallas_call(
        _kvup_kernel,
        grid=(m // tm,),
        in_specs=[pl.BlockSpec((tm, B_KV_RANK), lambda i: (i, lat_block)),
                  pl.BlockSpec((1, B_KV_RANK), lambda i: (0, 0)),
                  _layer_spec((B_KV_RANK, 2 * B_DH), lambda i: (0, 0), layer)],
        out_specs=[pl.BlockSpec((tm, B_DH), lambda i: (i, 0)),
                   pl.BlockSpec((tm // QBLK, B_DH, QBLK), lambda i: (i, 0, 0))],
        out_shape=[jax.ShapeDtypeStruct((m, B_DH), BF16),
                   jax.ShapeDtypeStruct((m // QBLK, B_DH, QBLK), BF16)],
        compiler_params=_cparams(1),
        name="kv_up",
    )(bproj, kv_norm_g.reshape(1, B_KV_RANK), w_kv_up)


def _hgrn_chunk(q, fr, ir, gr, lb, ng, st, tril):
    cs, sub = HGRN_CHUNK, HGRN_SUB
    f = lb + (1.0 - lb) * jax.nn.sigmoid(fr)
    lf = jnp.log(jnp.maximum(f, 1e-20))
    k = 1.0 - f
    v = _silu(ir)
    b = jnp.dot(tril, lf, precision=lax.Precision.HIGHEST, preferred_element_type=F32) * LOG2_E
    o_inter = _dot_nt((q * jnp.exp2(b)).astype(BF16), st.astype(BF16))

    col_s = lax.broadcasted_iota(jnp.int32, (sub, cs), 1)
    blocks = []
    for si in range(cs // sub):
        lo = si * sub
        q_s = q[lo:lo + sub]
        b_s = b[lo:lo + sub]
        if si > 0:
            bref = b[lo - 1:lo]
            qt = (q_s * jnp.exp2(b_s - bref)).astype(BF16)
            kj = (k[:lo] * jnp.exp2(bref - b[:lo])).astype(BF16)
            attn = jnp.concatenate([_dot_nt(qt, kj), jnp.zeros((sub, cs - lo), F32)], axis=1)
        else:
            attn = jnp.zeros((sub, cs), F32)
        for s in range(sub):
            a = q_s * k[lo + s:lo + s + 1] * jnp.exp2(b_s - b[lo + s:lo + s + 1])
            attn = jnp.where(col_s == lo + s, jnp.sum(a, axis=-1, keepdims=True), attn)
        blocks.append(attn)
    attn = jnp.where(tril > 0.0, jnp.concatenate(blocks, axis=0), 0.0)
    o = o_inter + _dot(attn.astype(BF16), v.astype(BF16))

    b_end = b[cs - 1:cs]
    kd = (k * jnp.exp2(b_end - b)).astype(BF16)
    upd = _dot(v.T.astype(BF16), kd)
    st_new = st * jnp.exp2(b_end) + upd
    return _rms(o, ng) * _silu(gr), st_new


def _hgrn_kernel(q_ref, f_ref, i_ref, g_ref, lb_ref, ng_ref, o_ref, st_ref, *, seq):
    cs = HGRN_CHUNK
    st_ref[...] = jnp.zeros_like(st_ref)
    ng = ng_ref[...]
    row_c = lax.broadcasted_iota(jnp.int32, (cs, cs), 0)
    col_c = lax.broadcasted_iota(jnp.int32, (cs, cs), 1)
    tril = (col_c <= row_c).astype(F32)

    def chunk(c, carry):
        r0 = pl.multiple_of(c * cs, cs)
        for h in range(HGRN_HEADS_PER_STEP):
            cols = slice(h * A_DK, (h + 1) * A_DK)
            o, st_new = _hgrn_chunk(q_ref[pl.ds(r0, cs), cols], f_ref[pl.ds(r0, cs), cols],
                                    i_ref[pl.ds(r0, cs), cols], g_ref[pl.ds(r0, cs), cols],
                                    lb_ref[:, cols], ng, st_ref[h], tril)
            st_ref[h] = st_new
            o_ref[pl.ds(r0, cs), cols] = o.astype(o_ref.dtype)
        return carry

    lax.fori_loop(0, seq // cs, chunk, 0, unroll=2)


def hgrn2(aproj, lb, norm_g, seq):
    m = aproj.shape[0]
    bsz = m // seq
    nh = HGRN_HEADS_PER_STEP
    steps = A_HEADS // nh

    def col(part):
        return pl.BlockSpec((seq, nh * A_DK), lambda b, h: (b, part * steps + h))

    return pl.pallas_call(
        functools.partial(_hgrn_kernel, seq=seq),
        grid=(bsz, steps),
        in_specs=[col(0), col(1), col(2), col(3),
                  pl.BlockSpec((1, nh * A_DK), lambda b, h: (0, h)),
                  pl.BlockSpec((1, A_DV), lambda b, h: (0, 0))],
        out_specs=pl.BlockSpec((seq, nh * A_DV), lambda b, h: (b, h)),
        out_shape=jax.ShapeDtypeStruct((m, A_HEADS * A_DV), BF16),
        scratch_shapes=[pltpu.VMEM((nh, A_DV, A_DK), F32)],
        compiler_params=_cparams(2),
        name="hgrn2",
    )(aproj, aproj, aproj, aproj, lb.reshape(1, A_WIDTH), norm_g.reshape(1, A_DV))


def _dsa_kernel(q_ref, iq_ref, smq_ref, smk_ref, k_ref, vt_ref, bt_ref, o_ref,
                key_ref, msk_ref, lg_ref, mx_ref, ls_ref, acc_ref, *, nblk, topk):
    j = pl.program_id(1)
    row = lax.broadcasted_iota(jnp.int32, (QBLK, QBLK), 0)
    col = lax.broadcasted_iota(jnp.int32, (QBLK, QBLK), 1)
    causal = row <= col
    n_grp = j // TOPK_GROUP + 1
    n_pair = j // 2 + 1
    int_min = jnp.int32(-2 ** 31)

    def sort_key(score):
        bits = lax.bitcast_convert_type(score, jnp.int32)
        return bits ^ ((bits >> 31) & jnp.int32(0x7FFFFFFF))

    iq = iq_ref[...].astype(BF16)
    iqs = jnp.concatenate([iq[:, h * IDX_DH:(h + 1) * IDX_DH] for h in range(IDX_HEADS)], axis=0)
    iwt = (smq_ref[:, IDX_DH:IDX_DH + IDX_HEADS] * (IDX_HEADS ** -0.5 * IDX_DH ** -0.5)).T

    def visible(kb):
        return jnp.logical_or(kb < j, jnp.logical_and(kb == j, causal))

    def score_tile(kb):
        r0 = pl.multiple_of(kb * QBLK, QBLK)
        ik = smk_ref[pl.ds(r0, QBLK), 0:IDX_DH].astype(BF16)
        d = _dot_nt(ik, iqs)
        sc = jnp.zeros((QBLK, QBLK), F32)
        for h in range(IDX_HEADS):
            sc = sc + jnp.maximum(d[:, h * QBLK:(h + 1) * QBLK], 0.0) * iwt[h:h + 1, :]
        key_ref[kb] = sort_key(jnp.where(visible(kb), sc, NEG_INF))

    def pairwise(tile_fn):
        def step(kp, carry):
            tile_fn(2 * kp)
            tile_fn(2 * kp + 1)
            return carry
        lax.fori_loop(0, n_pair, step, 0)

    def groupwise(tile_fn):
        def step(gi, carry):
            for u in range(TOPK_GROUP):
                tile_fn(gi * TOPK_GROUP + u)
            return carry
        lax.fori_loop(0, n_grp, step, 0)

    pairwise(score_tile)

    def fill_step(kb, carry):
        key_ref[kb] = sort_key(jnp.full((QBLK, QBLK), NEG_INF, F32))
        return carry

    lax.fori_loop(2 * n_pair, n_grp * TOPK_GROUP, fill_step, 0)

    kf = jnp.float32(topk)

    def count(pred):
        def pair(kp, c):
            for u in range(2):
                kb = 2 * kp + u
                c = c + jnp.where(pred(kb, key_ref[kb]), 1.0, 0.0)
            return c
        c = lax.fori_loop(0, n_pair, pair, jnp.zeros((QBLK, QBLK), F32))
        return jnp.sum(c, axis=0, keepdims=True)

    def thr_step(it, cand):
        trial = cand | lax.shift_left(jnp.int32(1), 31 - it)
        t_s = trial ^ int_min
        cnt = count(lambda kb, key: key >= t_s)
        return jnp.where(cnt >= kf, trial, cand)

    cand = lax.fori_loop(0, 32, thr_step, jnp.zeros((1, QBLK), jnp.int32))
    thr = cand ^ int_min
    n_gt = count(lambda kb, key: key > thr)
    n_ge = count(lambda kb, key: key >= thr)
    need = kf - n_gt
    nbits = (nblk * QBLK - 1).bit_length()

    def tie_search():
        def idx_step(it, pos):
            trial = pos + lax.shift_left(jnp.int32(1), nbits - 1 - it)
            cnt = count(lambda kb, key: jnp.logical_and(key == thr, kb * QBLK + row < trial))
            return jnp.where(cnt < need, trial, pos)
        return lax.fori_loop(0, nbits, idx_step, jnp.zeros((1, QBLK), jnp.int32))

    has_ties = jnp.max(n_ge) > kf
    pos = lax.cond(has_ties, tie_search, lambda: jnp.full((1, QBLK), nblk * QBLK, jnp.int32))

    def mask_tile(kb):
        key = key_ref[kb]
        sel = jnp.logical_or(key > thr, jnp.logical_and(key == thr, kb * QBLK + row <= pos))
        msk_ref[kb] = jnp.where(jnp.logical_and(sel, visible(kb)), 0.0, NEG_INF)

    groupwise(mask_tile)

    q = q_ref[...].astype(BF16)
    qs = jnp.concatenate([q[:, h * B_DH:(h + 1) * B_DH] for h in range(B_HEADS)], axis=0)
    scale = B_DH ** -0.5
    mx_ref[...] = jnp.full(mx_ref.shape, NEG_INF, F32)

    def logit_tile(kb):
        r0 = pl.multiple_of(kb * QBLK, QBLK)
        lg = _dot_nt(k_ref[pl.ds(r0, QBLK), :], qs)
        msk = msk_ref[kb]
        lg = lg * scale + bt_ref[jnp.clip(j - kb, 0, 2)] + jnp.concatenate([msk] * B_HEADS, axis=1)
        lg_ref[kb] = lg
        mx_ref[...] = jnp.maximum(mx_ref[...], lg)

    groupwise(logit_tile)
    mx = jnp.max(mx_ref[...], axis=0, keepdims=True)
    ls_ref[...] = jnp.zeros_like(ls_ref)
    acc_ref[...] = jnp.zeros_like(acc_ref)

    def pv_tile(kb):
        p = jnp.exp(lg_ref[kb] - mx)
        ls_ref[...] += p
        acc_ref[...] += _dot(vt_ref[kb], p.astype(BF16))

    groupwise(pv_tile)
    out_t = acc_ref[...] / jnp.sum(ls_ref[...], axis=0, keepdims=True)
    for h in range(B_HEADS):
        o_ref[:, h * B_DH:(h + 1) * B_DH] = out_t[:, h * QBLK:(h + 1) * QBLK].T.astype(o_ref.dtype)


def dsa(bproj, k, vt, bias_b, seq):
    m = bproj.shape[0]
    bsz = m // seq
    nblk = seq // QBLK
    assert nblk % TOPK_GROUP == 0
    topk = min(TOPK_MAX, seq // 4)
    qw = B_HEADS * B_DH
    small_blk = (2 * qw + B_KV_RANK) // B_SMALL_W
    cols = B_HEADS * QBLK
    return pl.pallas_call(
        functools.partial(_dsa_kernel, nblk=nblk, topk=topk),
        grid=(bsz, nblk),
        in_specs=[pl.BlockSpec((QBLK, qw), lambda b, j: (b * nblk + j, 0)),
                  pl.BlockSpec((QBLK, qw), lambda b, j: (b * nblk + j, 1)),
                  pl.BlockSpec((QBLK, B_SMALL_W), lambda b, j: (b * nblk + j, small_blk)),
                  pl.BlockSpec((seq, B_SMALL_W), lambda b, j: (b, small_blk)),
                  pl.BlockSpec((seq, B_DH), lambda b, j: (b, 0)),
                  pl.BlockSpec((nblk, B_DH, QBLK), lambda b, j: (b, 0, 0)),
                  pl.BlockSpec((3, QBLK, cols), lambda b, j: (0, 0, 0))],
        out_specs=pl.BlockSpec((QBLK, qw), lambda b, j: (b * nblk + j, 0)),
        out_shape=jax.ShapeDtypeStruct((m, qw), BF16),
        scratch_shapes=[pltpu.VMEM((nblk, QBLK, QBLK), jnp.int32),
                        pltpu.VMEM((nblk, QBLK, QBLK), F32),
                        pltpu.VMEM((nblk, QBLK, cols), F32),
                        pltpu.VMEM((QBLK, cols), F32),
                        pltpu.VMEM((QBLK, cols), F32),
                        pltpu.VMEM((B_DH, cols), F32)],
        compiler_params=_cparams(2),
        name="dsa",
    )(bproj, bproj, bproj, bproj, k, vt, bias_b)


def _swa_kernel(sink_ref, q_ref, kc_ref, vc_ref, kp_ref, vp_ref, bt_ref, o_ref):
    j = pl.program_id(1)
    grp = C_HEADS // C_KV_HEADS
    row = lax.broadcasted_iota(jnp.int32, (QBLK, QBLK), 0)
    col = lax.broadcasted_iota(jnp.int32, (QBLK, QBLK), 1)
    cur_ok = jnp.concatenate([row <= col] * grp, axis=1)
    prev_ok = jnp.concatenate([jnp.logical_and(row > col, j > 0)] * grp, axis=1)
    q = q_ref[...].astype(BF16)
    scale = C_DH ** -0.5
    for g in range(C_KV_HEADS):
        heads = range(g * grp, (g + 1) * grp)
        kv_cols = slice(g * C_DH, (g + 1) * C_DH)
        lanes = slice(g * grp * QBLK, (g + 1) * grp * QBLK)
        qs = jnp.concatenate([q[:, h * C_DH:(h + 1) * C_DH] for h in heads], axis=0)
        sink = jnp.concatenate([jnp.full((1, QBLK), sink_ref[h], F32) for h in heads], axis=1)
        lc = _dot_nt(kc_ref[:, kv_cols].astype(BF16), qs)
        lp = _dot_nt(kp_ref[:, kv_cols].astype(BF16), qs)
        lc = jnp.where(cur_ok, lc * scale + bt_ref[0, :, lanes], NEG_INF)
        lp = jnp.where(prev_ok, lp * scale + bt_ref[1, :, lanes], NEG_INF)
        mx = jnp.maximum(jnp.maximum(jnp.max(lc, axis=0, keepdims=True),
                                     jnp.max(lp, axis=0, keepdims=True)), sink)
        pc = jnp.exp(lc - mx)
        pp = jnp.exp(lp - mx)
        den = (jnp.sum(pc, axis=0, keepdims=True) + jnp.sum(pp, axis=0, keepdims=True)
               + jnp.exp(sink - mx))
        o_t = (_dot(vc_ref[:, kv_cols].T.astype(BF16), pc.astype(BF16))
               + _dot(vp_ref[:, kv_cols].T.astype(BF16), pp.astype(BF16))) / den
        for u in range(0, grp, 2):
            pair = jnp.concatenate([o_t[:, u * QBLK:(u + 1) * QBLK],
                                    o_t[:, (u + 1) * QBLK:(u + 2) * QBLK]], axis=0)
            h0 = g * grp + u
            o_ref[:, h0 * C_DH:(h0 + 2) * C_DH] = pair.T.astype(o_ref.dtype)


def swa(cproj, sinks, bias_c, seq):
    m = cproj.shape[0]
    bsz = m // seq
    nblk = seq // QBLK
    qw = C_HEADS * C_DH
    kvw = C_KV_HEADS * C_DH
    kblk = qw // kvw

    def cur(off):
        return pl.BlockSpec((QBLK, kvw), lambda b, j: (b * nblk + j, kblk + off))

    def prev(off):
        return pl.BlockSpec((QBLK, kvw), lambda b, j: (b * nblk + jnp.maximum(j - 1, 0), kblk + off))

    return pl.pallas_call(
        _swa_kernel,
        grid=(bsz, nblk),
        in_specs=[pl.BlockSpec(memory_space=pltpu.SMEM),
                  pl.BlockSpec((QBLK, qw), lambda b, j: (b * nblk + j, 0)),
                  cur(0), cur(1), prev(0), prev(1),
                  pl.BlockSpec((3, QBLK, C_HEADS * QBLK), lambda b, j: (0, 0, 0))],
        out_specs=pl.BlockSpec((QBLK, qw), lambda b, j: (b * nblk + j, 0)),
        out_shape=jax.ShapeDtypeStruct((m, qw), BF16),
        compiler_params=_cparams(2),
        name="swa",
    )(sinks.astype(F32), cproj, cproj, cproj, cproj, cproj, bias_c)


def _pack_w_in(w):
    o = 0
    parts = {}
    for name, width in (("a", 4 * A_WIDTH), ("bq", B_HEADS * B_DH), ("blat", B_KV_RANK),
                        ("biq", IDX_HEADS * IDX_DH), ("bik", IDX_DH), ("biw", IDX_HEADS),
                        ("c", (C_HEADS + 2 * C_KV_HEADS) * C_DH)):
        parts[name] = w[:, o:o + width]
        o += width
    gates = w[:, o:]
    pad = jnp.zeros((w.shape[0], B_SMALL_W - IDX_DH - IDX_HEADS), w.dtype)
    wb = jnp.concatenate([parts["bq"], parts["biq"], parts["blat"], parts["bik"], parts["biw"], pad], axis=1)
    return parts["a"], wb, parts["c"], gates


def kernel(x, c, w_c_down, w_c_up, norm_gains, w_in, lb_logits, hgrn_norm, kv_norm, w_kv_up, rel_table,
           sinks, w_branch, w_out, ffn1_in, ffn1_out, ffn2_in, ffn2_out):
    bsz, seq, d = x.shape
    depth = w_in.shape[0]
    m = bsz * seq
    x = x.reshape(m, d).astype(F32)

    cond = matmul(c.astype(BF16), w_c_down.astype(BF16), out_dtype=BF16, act="silu")
    lbs = lower_bounds(lb_logits)
    bias_b = bias_tiles(rel_table[:, :B_HEADS], keys_on_rows=True)
    bias_c = bias_tiles(rel_table[:, B_HEADS:], keys_on_rows=True)
    lat_block = 2 * B_HEADS * B_DH // B_KV_RANK
    w_c_up, w_kv_up = w_c_up.astype(BF16), w_kv_up.astype(BF16)
    mods = [matmul(cond, w_c_up, layer=l, out_dtype=F32, tn=4096).reshape(bsz, N_MOD, d)
            for l in range(depth)]
    gains = norm_gains.astype(F32)

    def rows(w):
        return w.reshape(-1, w.shape[-1])
    k_ff, k_out = ffn1_out.shape[1], w_out.shape[1]
    branch_rows = w_branch.shape[1] * w_branch.shape[2]
    src_in, src_ffn1_in, src_ffn1_out, src_ffn2_in, src_ffn2_out, src_out, src_branch = (
        rows(t) for t in (w_in, ffn1_in, ffn1_out, ffn2_in, ffn2_out, w_out, w_branch))

    w1_in = ffn1_in[0].astype(BF16)
    w_in_l = w_in[0].astype(BF16)
    h = normmod(x, gains[0], mods[0], seq, gi=0, shift_i=0, scale_i=1)
    for l in range(depth):
        mod, g = mods[l], gains[l]
        wa, wb, wc, wg = _pack_w_in(w_in_l)

        u, w1_out = swiglu_in(h, w1_in, cast=(src_ffn1_out, k_ff, l))
        y, wo = matmul(u, w1_out, out_dtype=BF16, cast=(src_out, k_out, l))
        x, h = resid_normmod(x, y, g, mod, g, mod, seq, gi=1, gate_i=2, coef=FFN_RES,
                             gi2=2, shift_i=3, scale_i=4)

        aproj = matmul(h, wa, out_dtype=F32)
        bproj, w_br = matmul(h, wb, out_dtype=F32, tn=wb.shape[-1] // 3, cast=(src_branch, branch_rows, l))
        cproj = matmul(h, wc, out_dtype=BF16, tn=wc.shape[-1] // 2)
        ya = hgrn2(aproj, lbs[l], hgrn_norm[l].astype(F32), seq)
        kb, vb = kv_up(bproj, lat_block, kv_norm[l].astype(F32), w_kv_up, l)
        yb = dsa(bproj, kb, vb, bias_b, seq)
        yc = swa(cproj, sinks[l], bias_c, seq)
        mix, w2_in = merge(h, wg, ya, yb, yc, w_br.reshape(w_branch.shape[1:]), cast=(src_ffn2_in, d, l))
        y, w2_out = matmul(mix, wo, out_dtype=BF16, cast=(src_ffn2_out, k_ff, l))
        x, h = resid_normmod(x, y, g, mod, g, mod, seq, gi=3, gate_i=5, coef=1.0,
                             gi2=4, shift_i=6, scale_i=7)

        if l + 1 < depth:
            u, w1_in = swiglu_in(h, w2_in, cast=(src_ffn1_in, d, l + 1))
            y, w_in_l = matmul(u, w2_out, out_dtype=BF16, cast=(src_in, d, l + 1))
            x, h = resid_normmod(x, y, g, mod, gains[l + 1], mods[l + 1], seq, gi=5, gate_i=8,
                                 coef=FFN_RES, gi2=0, shift_i=0, scale_i=1)
        else:
            u = swiglu_in(h, w2_in)
            y = matmul(u, w2_out, out_dtype=BF16)
            x = resid(x, y, g, mod, seq, gi=5, gate_i=8, coef=FFN_RES)
    return x.reshape(bsz, seq, d)
```

```python
import functools
import math

import jax
import jax.numpy as jnp
from jax import lax
from jax.experimental import pallas as pl
from jax.experimental.pallas import tpu as pltpu

A_HEADS = 8
A_DK = 128
A_DV = 128
A_WIDTH = A_HEADS * A_DK
B_HEADS = 8
B_DH = 128
B_KV_RANK = 512
IDX_HEADS = 16
IDX_DH = 64
TOPK_MAX = 256
C_HEADS = 16
C_KV_HEADS = 2
C_DH = 64
WINDOW = 128
REL_BUCKETS = 32
REL_MAX_DIST = 128
N_BRANCH = 3
BRANCH_W = 1024
FFN_RES = 0.5
N_MOD = 9
RMS_EPS = 1e-6
NEG_INF = -1e30
LOG2_E = math.log2(math.e)

VMEM_LIMIT_BYTES = 56 * 1024 * 1024

QBLK = 128
HGRN_CHUNK = 64
HGRN_SUB = 16
HGRN_HEADS_PER_STEP = 4
TOPK_GROUP = 4
B_SMALL_W = 128

PROJ_A = 0
PROJ_BQ = PROJ_A + 4 * A_WIDTH
PROJ_BIQ = PROJ_BQ + B_HEADS * B_DH
PROJ_CQ = PROJ_BIQ + IDX_HEADS * IDX_DH
PROJ_BLAT = PROJ_CQ + C_HEADS * C_DH
PROJ_SMALL = PROJ_BLAT + B_KV_RANK
PROJ_CK = PROJ_SMALL + B_SMALL_W
PROJ_CV = PROJ_CK + C_KV_HEADS * C_DH
PROJ_W = PROJ_CV + C_KV_HEADS * C_DH

BF16 = jnp.bfloat16
F32 = jnp.float32


def _cparams(n_grid_dims):
    return pltpu.CompilerParams(dimension_semantics=("arbitrary",) * n_grid_dims,
                                vmem_limit_bytes=VMEM_LIMIT_BYTES)


def _dot(a, b):
    return jnp.dot(a, b, preferred_element_type=F32)


def _dot_nt(a, b):
    return lax.dot_general(a, b, (((1,), (1,)), ((), ())), preferred_element_type=F32)


def _silu(x):
    return x * jax.nn.sigmoid(x)


def _lower_bound_kernel(x_ref, o_ref):
    x = x_ref[...]
    m = jnp.max(x, axis=0, keepdims=True)
    e = jnp.exp(x - m)
    p = e / jnp.sum(e, axis=0, keepdims=True)
    depth = x.shape[0]
    run = jnp.zeros_like(p[0:1])
    for l in range(depth):
        o_ref[l:l + 1, :] = run
        if l + 1 < depth:
            run = run + p[l + 1:l + 2]


def lower_bounds(lb_logits):
    return pl.pallas_call(
        _lower_bound_kernel,
        out_shape=jax.ShapeDtypeStruct(lb_logits.shape, F32),
        name="lower_bounds",
    )(lb_logits.astype(F32))


def _t5_bucket(dist):
    max_exact = REL_BUCKETS // 2
    d = jnp.maximum(dist, 0)
    df = jnp.maximum(d, 1).astype(F32)
    large = max_exact + (jnp.log(df / max_exact) / math.log(REL_MAX_DIST / max_exact)
                         * (REL_BUCKETS - max_exact)).astype(jnp.int32)
    large = jnp.minimum(large, REL_BUCKETS - 1)
    return jnp.where(d < max_exact, d, large)


def _bias_tile_kernel(tab_ref, o_ref, *, keys_on_rows):
    h = pl.program_id(0)
    row = lax.broadcasted_iota(jnp.int32, (QBLK, QBLK), 0)
    col = lax.broadcasted_iota(jnp.int32, (QBLK, QBLK), 1)
    t_minus_s = col - row if keys_on_rows else row - col
    for r in range(3):
        bucket = _t5_bucket(r * QBLK + t_minus_s)
        tile = jnp.zeros((QBLK, QBLK), F32)
        for b in range(REL_BUCKETS):
            tile = jnp.where(bucket == b, tab_ref[h, b], tile)
        if keys_on_rows:
            o_ref[r] = tile
        else:
            o_ref[r, 0] = tile


def bias_tiles(rel_table, *, keys_on_rows):
    n_heads = rel_table.shape[1]
    if keys_on_rows:
        out_spec = pl.BlockSpec((3, QBLK, QBLK), lambda h: (0, 0, h))
        out_shape = (3, QBLK, n_heads * QBLK)
    else:
        out_spec = pl.BlockSpec((3, 1, QBLK, QBLK), lambda h: (0, h, 0, 0))
        out_shape = (3, n_heads, QBLK, QBLK)
    return pl.pallas_call(
        functools.partial(_bias_tile_kernel, keys_on_rows=keys_on_rows),
        grid=(n_heads,),
        in_specs=[pl.BlockSpec(memory_space=pltpu.SMEM)],
        out_specs=out_spec,
        out_shape=jax.ShapeDtypeStruct(out_shape, F32),
        compiler_params=_cparams(1),
        name="bias_tiles",
    )(rel_table.T.astype(F32))


def _rms(x, g):
    return x * lax.rsqrt(jnp.mean(x * x, axis=-1, keepdims=True) + RMS_EPS) * g


def _modulate(x, g_ref, mod_ref, gi, shift_i, scale_i):
    y = _rms(x, g_ref[gi:gi + 1, :])
    return y * (1.0 + mod_ref[0, scale_i:scale_i + 1, :]) + mod_ref[0, shift_i:shift_i + 1, :]


def _normmod_kernel(x_ref, g_ref, mod_ref, h_ref, *, gi, shift_i, scale_i):
    h_ref[...] = _modulate(x_ref[...], g_ref, mod_ref, gi, shift_i, scale_i).astype(h_ref.dtype)


def normmod(x, gains, mod, seq, *, gi, shift_i, scale_i, tm=256):
    m, d = x.shape
    tpb = seq // tm
    return pl.pallas_call(
        functools.partial(_normmod_kernel, gi=gi, shift_i=shift_i, scale_i=scale_i),
        grid=(m // tm,),
        in_specs=[pl.BlockSpec((tm, d), lambda i: (i, 0)),
                  pl.BlockSpec(gains.shape, lambda i: (0, 0)),
                  pl.BlockSpec((1, N_MOD, d), lambda i: (i // tpb, 0, 0))],
        out_specs=pl.BlockSpec((tm, d), lambda i: (i, 0)),
        out_shape=jax.ShapeDtypeStruct((m, d), BF16),
        compiler_params=_cparams(1),
        name="normmod",
    )(x, gains, mod)


def _resid_kernel(x_ref, y_ref, g_ref, mod_ref, x_out_ref, *, gi, gate_i, coef):
    y = _rms(y_ref[...].astype(F32), g_ref[gi:gi + 1, :])
    x_out_ref[...] = x_ref[...] + coef * mod_ref[0, gate_i:gate_i + 1, :] * y


def resid(x, y, gains, mod, seq, *, gi, gate_i, coef, tm=256):
    m, d = x.shape
    tpb = seq // tm
    row = pl.BlockSpec((tm, d), lambda i: (i, 0))
    return pl.pallas_call(
        functools.partial(_resid_kernel, gi=gi, gate_i=gate_i, coef=coef),
        grid=(m // tm,),
        in_specs=[row, row,
                  pl.BlockSpec(gains.shape, lambda i: (0, 0)),
                  pl.BlockSpec((1, N_MOD, d), lambda i: (i // tpb, 0, 0))],
        out_specs=row,
        out_shape=jax.ShapeDtypeStruct((m, d), F32),
        compiler_params=_cparams(1),
        name="resid",
    )(x, y, gains, mod)


def _resid_normmod_kernel(x_ref, y_ref, g_ref, mod_ref, g2_ref, mod2_ref, x_out_ref, h_ref, *,
                          gi, gate_i, coef, gi2, shift_i, scale_i):
    y = _rms(y_ref[...].astype(F32), g_ref[gi:gi + 1, :])
    x_new = x_ref[...] + coef * mod_ref[0, gate_i:gate_i + 1, :] * y
    x_out_ref[...] = x_new
    h_ref[...] = _modulate(x_new, g2_ref, mod2_ref, gi2, shift_i, scale_i).astype(h_ref.dtype)


def resid_normmod(x, y, gains, mod, gains2, mod2, seq, *, gi, gate_i, coef, gi2, shift_i, scale_i, tm=256):
    m, d = x.shape
    tpb = seq // tm
    row = pl.BlockSpec((tm, d), lambda i: (i, 0))
    gspec = pl.BlockSpec(gains.shape, lambda i: (0, 0))
    mspec = pl.BlockSpec((1, N_MOD, d), lambda i: (i // tpb, 0, 0))
    return pl.pallas_call(
        functools.partial(_resid_normmod_kernel, gi=gi, gate_i=gate_i, coef=coef, gi2=gi2,
                          shift_i=shift_i, scale_i=scale_i),
        grid=(m // tm,),
        in_specs=[row, row, gspec, mspec, gspec, mspec],
        out_specs=[row, row],
        out_shape=[jax.ShapeDtypeStruct((m, d), F32), jax.ShapeDtypeStruct((m, d), BF16)],
        compiler_params=_cparams(1),
        name="resid_normmod",
    )(x, y, gains, mod, gains2, mod2)


def _mm_kernel(a_ref, w_ref, o_ref, *, act):
    acc = _dot(a_ref[...], w_ref[...])
    if act == "silu":
        acc = _silu(acc)
    o_ref[...] = acc.astype(o_ref.dtype)


def _layer_spec(tail_block, tail_index, layer):
    if layer is None:
        return pl.BlockSpec(tail_block, tail_index)
    return pl.BlockSpec((None,) + tail_block, lambda *g: (layer,) + tail_index(*g))


def _mxu_call(body, grid, in_specs, out_spec, out_shape, args, name, cast):
    if cast is None:
        return pl.pallas_call(body, grid=grid, in_specs=in_specs, out_specs=out_spec, out_shape=out_shape,
                              compiler_params=_cparams(2), name=name)(*args)
    src, rows_per_layer, layer = cast
    steps = grid[0] * grid[1]
    rows = rows_per_layer // steps
    assert rows * steps == rows_per_layer and rows % 16 == 0
    cols = src.shape[1]
    n_in = len(in_specs)

    def body_and_cast(*refs):
        refs[-1][...] = refs[n_in][...].astype(BF16)
        body(*refs[:n_in], refs[n_in + 1])

    return pl.pallas_call(
        body_and_cast, grid=grid,
        in_specs=list(in_specs) + [pl.BlockSpec((rows, cols), lambda i, j: (layer * steps + i * grid[1] + j, 0))],
        out_specs=[out_spec, pl.BlockSpec((rows, cols), lambda i, j: (i * grid[1] + j, 0))],
        out_shape=[out_shape, jax.ShapeDtypeStruct((rows_per_layer, cols), BF16)],
        compiler_params=_cparams(2), name=name)(*args, src)


def matmul(a, w, *, out_dtype, layer=None, act=None, cast=None, tm=1024, tn=512):
    m, k = a.shape
    n = w.shape[-1]
    tm, tn = min(tm, m), min(tn, n)
    assert m % tm == 0 and n % tn == 0
    return _mxu_call(
        functools.partial(_mm_kernel, act=act), (m // tm, n // tn),
        [pl.BlockSpec((tm, k), lambda i, j: (i, 0)),
         _layer_spec((k, tn), lambda i, j: (0, j), layer)],
        pl.BlockSpec((tm, tn), lambda i, j: (i, j)),
        jax.ShapeDtypeStruct((m, n), out_dtype), (a, w), "matmul", cast)


def _swiglu_kernel(a_ref, wu_ref, wv_ref, o_ref):
    a = a_ref[...]
    u = _dot(a, wu_ref[...])
    v = _dot(a, wv_ref[...])
    o_ref[...] = (_silu(u) * v).astype(o_ref.dtype)


def swiglu_in(a, w, *, cast=None, tm=1024, tn=512):
    m, k = a.shape
    f = w.shape[-1] // 2
    tm = min(tm, m)
    nf = f // tn
    return _mxu_call(
        _swiglu_kernel, (m // tm, nf),
        [pl.BlockSpec((tm, k), lambda i, j: (i, 0)),
         pl.BlockSpec((k, tn), lambda i, j: (0, j)),
         pl.BlockSpec((k, tn), lambda i, j: (0, j + nf))],
        pl.BlockSpec((tm, tn), lambda i, j: (i, j)),
        jax.ShapeDtypeStruct((m, f), BF16), (a, w, w), "swiglu_in", cast)


def _merge_kernel(h_ref, wga_ref, wgb_ref, wgc_ref, ya_ref, yb_ref, yc_ref, wb_ref, o_ref):
    h = h_ref[...]
    acc = jax.nn.sigmoid(_dot(h, wga_ref[...])) * _dot(ya_ref[...], wb_ref[0])
    acc = acc + jax.nn.sigmoid(_dot(h, wgb_ref[...])) * _dot(yb_ref[...], wb_ref[1])
    acc = acc + jax.nn.sigmoid(_dot(h, wgc_ref[...])) * _dot(yc_ref[...], wb_ref[2])
    o_ref[...] = acc.astype(o_ref.dtype)


def merge(h, w_gates, ya, yb, yc, w_branch, layer, *, cast=None, tm=1024, tn=256):
    m, k = h.shape
    kw = ya.shape[1]
    d = w_branch.shape[-1]
    tm = min(tm, m)
    nd = d // tn
    y_spec = pl.BlockSpec((tm, kw), lambda i, j: (i, 0))
    return _mxu_call(
        _merge_kernel, (m // tm, nd),
        [pl.BlockSpec((tm, k), lambda i, j: (i, 0)),
         _layer_spec((k, tn), lambda i, j: (0, j), layer),
         _layer_spec((k, tn), lambda i, j: (0, j + nd), layer),
         _layer_spec((k, tn), lambda i, j: (0, j + 2 * nd), layer),
         y_spec, y_spec, y_spec,
         _layer_spec((N_BRANCH, kw, tn), lambda i, j: (0, 0, j), layer)],
        pl.BlockSpec((tm, tn), lambda i, j: (i, j)),
        jax.ShapeDtypeStruct((m, d), BF16),
        (h, w_gates, w_gates, w_gates, ya, yb, yc, w_branch), "merge", cast)


def _kvup_kernel(lat_ref, g_ref, w_ref, k_ref, vt_ref):
    h = _rms(lat_ref[...], g_ref[...]).astype(BF16)
    kv = _dot(h, w_ref[...])
    k_ref[...] = kv[:, :B_DH].astype(k_ref.dtype)
    v = kv[:, B_DH:]
    for u in range(vt_ref.shape[0]):
        vt_ref[u] = v[u * QBLK:(u + 1) * QBLK].T.astype(vt_ref.dtype)


def kv_up(proj, kv_norm_g, w_kv_up, layer, *, tm=512):
    m = proj.shape[0]
    return pl.pallas_call(
        _kvup_kernel,
        grid=(m // tm,),
        in_specs=[pl.BlockSpec((tm, B_KV_RANK), lambda i: (i, PROJ_BLAT // B_KV_RANK)),
                  pl.BlockSpec((1, B_KV_RANK), lambda i: (0, 0)),
                  _layer_spec((B_KV_RANK, 2 * B_DH), lambda i: (0, 0), layer)],
        out_specs=[pl.BlockSpec((tm, B_DH), lambda i: (i, 0)),
                   pl.BlockSpec((tm // QBLK, B_DH, QBLK), lambda i: (i, 0, 0))],
        out_shape=[jax.ShapeDtypeStruct((m, B_DH), BF16),
                   jax.ShapeDtypeStruct((m // QBLK, B_DH, QBLK), BF16)],
        compiler_params=_cparams(1),
        name="kv_up",
    )(proj, kv_norm_g.reshape(1, B_KV_RANK), w_kv_up)


def _hgrn_chunk(q, fr, ir, gr, lb, ng, st, tril):
    cs, sub = HGRN_CHUNK, HGRN_SUB
    f = lb + (1.0 - lb) * jax.nn.sigmoid(fr)
    lf = jnp.log(jnp.maximum(f, 1e-20))
    k = 1.0 - f
    v = _silu(ir)
    b = jnp.dot(tril, lf, precision=lax.Precision.HIGHEST, preferred_element_type=F32) * LOG2_E
    o_inter = _dot_nt((q * jnp.exp2(b)).astype(BF16), st.astype(BF16))

    col_s = lax.broadcasted_iota(jnp.int32, (sub, cs), 1)
    blocks = []
    for si in range(cs // sub):
        lo = si * sub
        q_s = q[lo:lo + sub]
        b_s = b[lo:lo + sub]
        if si > 0:
            bref = b[lo - 1:lo]
            qt = (q_s * jnp.exp2(b_s - bref)).astype(BF16)
            kj = (k[:lo] * jnp.exp2(bref - b[:lo])).astype(BF16)
            attn = jnp.concatenate([_dot_nt(qt, kj), jnp.zeros((sub, cs - lo), F32)], axis=1)
        else:
            attn = jnp.zeros((sub, cs), F32)
        for s in range(sub):
            a = q_s * k[lo + s:lo + s + 1] * jnp.exp2(b_s - b[lo + s:lo + s + 1])
            attn = jnp.where(col_s == lo + s, jnp.sum(a, axis=-1, keepdims=True), attn)
        blocks.append(attn)
    attn = jnp.where(tril > 0.0, jnp.concatenate(blocks, axis=0), 0.0)
    o = o_inter + _dot(attn.astype(BF16), v.astype(BF16))

    b_end = b[cs - 1:cs]
    kd = (k * jnp.exp2(b_end - b)).astype(BF16)
    upd = _dot(v.T.astype(BF16), kd)
    st_new = st * jnp.exp2(b_end) + upd
    return _rms(o, ng) * _silu(gr), st_new


def _hgrn_kernel(q_ref, f_ref, i_ref, g_ref, lb_ref, ng_ref, o_ref, st_ref, *, seq):
    cs = HGRN_CHUNK
    st_ref[...] = jnp.zeros_like(st_ref)
    ng = ng_ref[...]
    row_c = lax.broadcasted_iota(jnp.int32, (cs, cs), 0)
    col_c = lax.broadcasted_iota(jnp.int32, (cs, cs), 1)
    tril = (col_c <= row_c).astype(F32)

    def chunk(c, carry):
        r0 = pl.multiple_of(c * cs, cs)
        for h in range(HGRN_HEADS_PER_STEP):
            cols = slice(h * A_DK, (h + 1) * A_DK)
            o, st_new = _hgrn_chunk(q_ref[pl.ds(r0, cs), cols], f_ref[pl.ds(r0, cs), cols],
                                    i_ref[pl.ds(r0, cs), cols], g_ref[pl.ds(r0, cs), cols],
                                    lb_ref[:, cols], ng, st_ref[h], tril)
            st_ref[h] = st_new
            o_ref[pl.ds(r0, cs), cols] = o.astype(o_ref.dtype)
        return carry

    lax.fori_loop(0, seq // cs, chunk, 0, unroll=2)


def hgrn2(proj, lb, norm_g, seq):
    m = proj.shape[0]
    bsz = m // seq
    nh = HGRN_HEADS_PER_STEP
    steps = A_HEADS // nh

    def col(part):
        return pl.BlockSpec((seq, nh * A_DK), lambda b, h: (b, part * steps + h))

    return pl.pallas_call(
        functools.partial(_hgrn_kernel, seq=seq),
        grid=(bsz, steps),
        in_specs=[col(0), col(1), col(2), col(3),
                  pl.BlockSpec((1, nh * A_DK), lambda b, h: (0, h)),
                  pl.BlockSpec((1, A_DV), lambda b, h: (0, 0))],
        out_specs=pl.BlockSpec((seq, nh * A_DV), lambda b, h: (b, h)),
        out_shape=jax.ShapeDtypeStruct((m, A_HEADS * A_DV), BF16),
        scratch_shapes=[pltpu.VMEM((nh, A_DV, A_DK), F32)],
        compiler_params=_cparams(2),
        name="hgrn2",
    )(proj, proj, proj, proj, lb.reshape(1, A_WIDTH), norm_g.reshape(1, A_DV))


def _dsa_kernel(q_ref, iq_ref, smq_ref, smk_ref, k_ref, vt_ref, bt_ref, o_ref,
                key_ref, msk_ref, lg_ref, mx_ref, ls_ref, acc_ref, *, nblk, topk):
    j = pl.program_id(1)
    row = lax.broadcasted_iota(jnp.int32, (QBLK, QBLK), 0)
    col = lax.broadcasted_iota(jnp.int32, (QBLK, QBLK), 1)
    causal = row <= col
    n_grp = j // TOPK_GROUP + 1
    n_pair = j // 2 + 1
    int_min = jnp.int32(-2 ** 31)

    def sort_key(score):
        bits = lax.bitcast_convert_type(score, jnp.int32)
        return bits ^ ((bits >> 31) & jnp.int32(0x7FFFFFFF))

    iq = iq_ref[...].astype(BF16)
    iqs = jnp.concatenate([iq[:, h * IDX_DH:(h + 1) * IDX_DH] for h in range(IDX_HEADS)], axis=0)
    iwt = (smq_ref[:, IDX_DH:IDX_DH + IDX_HEADS] * (IDX_HEADS ** -0.5 * IDX_DH ** -0.5)).T

    def visible(kb):
        return jnp.logical_or(kb < j, jnp.logical_and(kb == j, causal))

    def score_tile(kb):
        r0 = pl.multiple_of(kb * QBLK, QBLK)
        ik = smk_ref[pl.ds(r0, QBLK), 0:IDX_DH].astype(BF16)
        d = _dot_nt(ik, iqs)
        sc = jnp.zeros((QBLK, QBLK), F32)
        for h in range(IDX_HEADS):
            sc = sc + jnp.maximum(d[:, h * QBLK:(h + 1) * QBLK], 0.0) * iwt[h:h + 1, :]
        key_ref[kb] = sort_key(jnp.where(visible(kb), sc, NEG_INF))

    def pairwise(tile_fn):
        def step(kp, carry):
            tile_fn(2 * kp)
            tile_fn(2 * kp + 1)
            return carry
        lax.fori_loop(0, n_pair, step, 0)

    def groupwise(tile_fn):
        def step(gi, carry):
            for u in range(TOPK_GROUP):
                tile_fn(gi * TOPK_GROUP + u)
            return carry
        lax.fori_loop(0, n_grp, step, 0)

    pairwise(score_tile)

    def fill_step(kb, carry):
        key_ref[kb] = sort_key(jnp.full((QBLK, QBLK), NEG_INF, F32))
        return carry

    lax.fori_loop(2 * n_pair, n_grp * TOPK_GROUP, fill_step, 0)

    kf = jnp.float32(topk)

    def count(pred):
        def pair(kp, c):
            for u in range(2):
                kb = 2 * kp + u
                c = c + jnp.where(pred(kb, key_ref[kb]), 1.0, 0.0)
            return c
        c = lax.fori_loop(0, n_pair, pair, jnp.zeros((QBLK, QBLK), F32))
        return jnp.sum(c, axis=0, keepdims=True)

    def thr_step(it, cand):
        trial = cand | lax.shift_left(jnp.int32(1), 31 - it)
        t_s = trial ^ int_min
        cnt = count(lambda kb, key: key >= t_s)
        return jnp.where(cnt >= kf, trial, cand)

    cand = lax.fori_loop(0, 32, thr_step, jnp.zeros((1, QBLK), jnp.int32))
    thr = cand ^ int_min
    n_gt = count(lambda kb, key: key > thr)
    n_ge = count(lambda kb, key: key >= thr)
    need = kf - n_gt
    nbits = (nblk * QBLK - 1).bit_length()

    def tie_search():
        def idx_step(it, pos):
            trial = pos + lax.shift_left(jnp.int32(1), nbits - 1 - it)
            cnt = count(lambda kb, key: jnp.logical_and(key == thr, kb * QBLK + row < trial))
            return jnp.where(cnt < need, trial, pos)
        return lax.fori_loop(0, nbits, idx_step, jnp.zeros((1, QBLK), jnp.int32))

    has_ties = jnp.max(n_ge) > kf
    pos = lax.cond(has_ties, tie_search, lambda: jnp.full((1, QBLK), nblk * QBLK, jnp.int32))

    def mask_tile(kb):
        key = key_ref[kb]
        sel = jnp.logical_or(key > thr, jnp.logical_and(key == thr, kb * QBLK + row <= pos))
        msk_ref[kb] = jnp.where(jnp.logical_and(sel, visible(kb)), 0.0, NEG_INF)

    groupwise(mask_tile)

    q = q_ref[...].astype(BF16)
    qs = jnp.concatenate([q[:, h * B_DH:(h + 1) * B_DH] for h in range(B_HEADS)], axis=0)
    scale = B_DH ** -0.5
    mx_ref[...] = jnp.full(mx_ref.shape, NEG_INF, F32)

    def logit_tile(kb):
        r0 = pl.multiple_of(kb * QBLK, QBLK)
        lg = _dot_nt(k_ref[pl.ds(r0, QBLK), :], qs)
        msk = msk_ref[kb]
        lg = lg * scale + bt_ref[jnp.clip(j - kb, 0, 2)] + jnp.concatenate([msk] * B_HEADS, axis=1)
        lg_ref[kb] = lg
        mx_ref[...] = jnp.maximum(mx_ref[...], lg)

    groupwise(logit_tile)
    mx = jnp.max(mx_ref[...], axis=0, keepdims=True)
    ls_ref[...] = jnp.zeros_like(ls_ref)
    acc_ref[...] = jnp.zeros_like(acc_ref)

    def pv_tile(kb):
        p = jnp.exp(lg_ref[kb] - mx)
        ls_ref[...] += p
        acc_ref[...] += _dot(vt_ref[kb], p.astype(BF16))

    groupwise(pv_tile)
    out_t = acc_ref[...] / jnp.sum(ls_ref[...], axis=0, keepdims=True)
    for h in range(B_HEADS):
        o_ref[:, h * B_DH:(h + 1) * B_DH] = out_t[:, h * QBLK:(h + 1) * QBLK].T.astype(o_ref.dtype)


def dsa(proj, k, vt, bias_b, seq):
    m = proj.shape[0]
    bsz = m // seq
    nblk = seq // QBLK
    assert nblk % TOPK_GROUP == 0
    topk = min(TOPK_MAX, seq // 4)
    qw = B_HEADS * B_DH
    small_blk = PROJ_SMALL // B_SMALL_W
    cols = B_HEADS * QBLK
    return pl.pallas_call(
        functools.partial(_dsa_kernel, nblk=nblk, topk=topk),
        grid=(bsz, nblk),
        in_specs=[pl.BlockSpec((QBLK, qw), lambda b, j: (b * nblk + j, PROJ_BQ // qw)),
                  pl.BlockSpec((QBLK, qw), lambda b, j: (b * nblk + j, PROJ_BIQ // qw)),
                  pl.BlockSpec((QBLK, B_SMALL_W), lambda b, j: (b * nblk + j, small_blk)),
                  pl.BlockSpec((seq, B_SMALL_W), lambda b, j: (b, small_blk)),
                  pl.BlockSpec((seq, B_DH), lambda b, j: (b, 0)),
                  pl.BlockSpec((nblk, B_DH, QBLK), lambda b, j: (b, 0, 0)),
                  pl.BlockSpec((3, QBLK, cols), lambda b, j: (0, 0, 0))],
        out_specs=pl.BlockSpec((QBLK, qw), lambda b, j: (b * nblk + j, 0)),
        out_shape=jax.ShapeDtypeStruct((m, qw), BF16),
        scratch_shapes=[pltpu.VMEM((nblk, QBLK, QBLK), jnp.int32),
                        pltpu.VMEM((nblk, QBLK, QBLK), F32),
                        pltpu.VMEM((nblk, QBLK, cols), F32),
                        pltpu.VMEM((QBLK, cols), F32),
                        pltpu.VMEM((QBLK, cols), F32),
                        pltpu.VMEM((B_DH, cols), F32)],
        compiler_params=_cparams(2),
        name="dsa",
    )(proj, proj, proj, proj, k, vt, bias_b)


def _swa_kernel(sink_ref, q_ref, kc_ref, vc_ref, kp_ref, vp_ref, bt_ref, o_ref):
    j = pl.program_id(1)
    grp = C_HEADS // C_KV_HEADS
    row = lax.broadcasted_iota(jnp.int32, (QBLK, QBLK), 0)
    col = lax.broadcasted_iota(jnp.int32, (QBLK, QBLK), 1)
    cur_ok = jnp.concatenate([row <= col] * grp, axis=1)
    prev_ok = jnp.concatenate([jnp.logical_and(row > col, j > 0)] * grp, axis=1)
    q = q_ref[...].astype(BF16)
    scale = C_DH ** -0.5
    for g in range(C_KV_HEADS):
        heads = range(g * grp, (g + 1) * grp)
        kv_cols = slice(g * C_DH, (g + 1) * C_DH)
        lanes = slice(g * grp * QBLK, (g + 1) * grp * QBLK)
        qs = jnp.concatenate([q[:, h * C_DH:(h + 1) * C_DH] for h in heads], axis=0)
        sink = jnp.concatenate([jnp.full((1, QBLK), sink_ref[h], F32) for h in heads], axis=1)
        lc = _dot_nt(kc_ref[:, kv_cols].astype(BF16), qs)
        lp = _dot_nt(kp_ref[:, kv_cols].astype(BF16), qs)
        lc = jnp.where(cur_ok, lc * scale + bt_ref[0, :, lanes], NEG_INF)
        lp = jnp.where(prev_ok, lp * scale + bt_ref[1, :, lanes], NEG_INF)
        mx = jnp.maximum(jnp.maximum(jnp.max(lc, axis=0, keepdims=True),
                                     jnp.max(lp, axis=0, keepdims=True)), sink)
        pc = jnp.exp(lc - mx)
        pp = jnp.exp(lp - mx)
        den = (jnp.sum(pc, axis=0, keepdims=True) + jnp.sum(pp, axis=0, keepdims=True)
               + jnp.exp(sink - mx))
        o_t = (_dot(vc_ref[:, kv_cols].T.astype(BF16), pc.astype(BF16))
               + _dot(vp_ref[:, kv_cols].T.astype(BF16), pp.astype(BF16))) / den
        for u in range(0, grp, 2):
            pair = jnp.concatenate([o_t[:, u * QBLK:(u + 1) * QBLK],
                                    o_t[:, (u + 1) * QBLK:(u + 2) * QBLK]], axis=0)
            h0 = g * grp + u
            o_ref[:, h0 * C_DH:(h0 + 2) * C_DH] = pair.T.astype(o_ref.dtype)


def swa(proj, sinks, bias_c, seq):
    m = proj.shape[0]
    bsz = m // seq
    nblk = seq // QBLK
    qw = C_HEADS * C_DH
    kvw = C_KV_HEADS * C_DH

    def cur(col):
        return pl.BlockSpec((QBLK, kvw), lambda b, j: (b * nblk + j, col // kvw))

    def prev(col):
        return pl.BlockSpec((QBLK, kvw), lambda b, j: (b * nblk + jnp.maximum(j - 1, 0), col // kvw))

    return pl.pallas_call(
        _swa_kernel,
        grid=(bsz, nblk),
        in_specs=[pl.BlockSpec(memory_space=pltpu.SMEM),
                  pl.BlockSpec((QBLK, qw), lambda b, j: (b * nblk + j, PROJ_CQ // qw)),
                  cur(PROJ_CK), cur(PROJ_CV), prev(PROJ_CK), prev(PROJ_CV),
                  pl.BlockSpec((3, QBLK, C_HEADS * QBLK), lambda b, j: (0, 0, 0))],
        out_specs=pl.BlockSpec((QBLK, qw), lambda b, j: (b * nblk + j, 0)),
        out_shape=jax.ShapeDtypeStruct((m, qw), BF16),
        compiler_params=_cparams(2),
        name="swa",
    )(sinks.astype(F32), proj, proj, proj, proj, proj, bias_c)


def _pack_w_in(w):
    o = 0
    parts = {}
    for name, width in (("a", 4 * A_WIDTH), ("bq", B_HEADS * B_DH), ("blat", B_KV_RANK),
                        ("biq", IDX_HEADS * IDX_DH), ("bik", IDX_DH), ("biw", IDX_HEADS),
                        ("cq", C_HEADS * C_DH), ("ck", C_KV_HEADS * C_DH), ("cv", C_KV_HEADS * C_DH)):
        parts[name] = w[..., o:o + width]
        o += width
    gates = w[..., o:]
    pad = jnp.zeros(w.shape[:-1] + (B_SMALL_W - IDX_DH - IDX_HEADS,), w.dtype)
    proj = jnp.concatenate([parts[n] for n in ("a", "bq", "biq", "cq", "blat", "bik", "biw")] + [pad]
                           + [parts["ck"], parts["cv"]], axis=-1)
    assert proj.shape[-1] == PROJ_W
    return proj.astype(BF16), gates.astype(BF16)


def kernel(x, c, w_c_down, w_c_up, norm_gains, w_in, lb_logits, hgrn_norm, kv_norm, w_kv_up, rel_table,
           sinks, w_branch, w_out, ffn1_in, ffn1_out, ffn2_in, ffn2_out):
    bsz, seq, d = x.shape
    depth = w_in.shape[0]
    m = bsz * seq
    x = x.reshape(m, d).astype(F32)

    cond = matmul(c.astype(BF16), w_c_down.astype(BF16), out_dtype=BF16, act="silu")
    lbs = lower_bounds(lb_logits)
    bias_b = bias_tiles(rel_table[:, :B_HEADS], keys_on_rows=True)
    bias_c = bias_tiles(rel_table[:, B_HEADS:], keys_on_rows=True)
    w_c_up, w_kv_up, w_branch = w_c_up.astype(BF16), w_kv_up.astype(BF16), w_branch.astype(BF16)
    wp, wg = _pack_w_in(w_in)
    mods = [matmul(cond, w_c_up, layer=l, out_dtype=F32, tn=4096).reshape(bsz, N_MOD, d)
            for l in range(depth)]
    gains = norm_gains.astype(F32)

    def rows(w):
        return w.reshape(-1, w.shape[-1])
    k_ff, k_out = ffn1_out.shape[1], w_out.shape[1]
    src_ffn1_in, src_ffn1_out, src_ffn2_in, src_ffn2_out, src_out = (
        rows(t) for t in (ffn1_in, ffn1_out, ffn2_in, ffn2_out, w_out))

    w1_in = ffn1_in[0].astype(BF16)
    h = normmod(x, gains[0], mods[0], seq, gi=0, shift_i=0, scale_i=1)
    for l in range(depth):
        mod, g = mods[l], gains[l]

        u, w1_out = swiglu_in(h, w1_in, cast=(src_ffn1_out, k_ff, l))
        y, wo = matmul(u, w1_out, out_dtype=BF16, tn=1024, cast=(src_out, k_out, l))
        x, h = resid_normmod(x, y, g, mod, g, mod, seq, gi=1, gate_i=2, coef=FFN_RES,
                             gi2=2, shift_i=3, scale_i=4)

        proj = matmul(h, wp, layer=l, out_dtype=F32, tn=PROJ_W // 9)
        ya = hgrn2(proj, lbs[l], hgrn_norm[l].astype(F32), seq)
        kb, vb = kv_up(proj, kv_norm[l].astype(F32), w_kv_up, l)
        yb = dsa(proj, kb, vb, bias_b, seq)
        yc = swa(proj, sinks[l], bias_c, seq)
        mix, w2_in = merge(h, wg, ya, yb, yc, w_branch, l, cast=(src_ffn2_in, d, l))
        y, w2_out = matmul(mix, wo, out_dtype=BF16, tn=1024, cast=(src_ffn2_out, k_ff, l))
        x, h = resid_normmod(x, y, g, mod, g, mod, seq, gi=3, gate_i=5, coef=1.0,
                             gi2=4, shift_i=6, scale_i=7)

        if l + 1 < depth:
            u, w1_in = swiglu_in(h, w2_in, cast=(src_ffn1_in, d, l + 1))
            y = matmul(u, w2_out, out_dtype=BF16, tn=1024)
            x, h = resid_normmod(x, y, g, mod, gains[l + 1], mods[l + 1], seq, gi=5, gate_i=8,
                                 coef=FFN_RES, gi2=0, shift_i=0, scale_i=1)
        else:
            u = swiglu_in(h, w2_in)
            y = matmul(u, w2_out, out_dtype=BF16, tn=1024)
            x = resid(x, y, g, mod, seq, gi=5, gate_i=8, coef=FFN_RES)
    return x.reshape(bsz, seq, d)
```

```python
import functools
import math

import jax
import jax.numpy as jnp
from jax import lax
from jax.experimental import pallas as pl
from jax.experimental.pallas import tpu as pltpu

A_HEADS = 8
A_DK = 128
A_DV = 128
A_WIDTH = A_HEADS * A_DK
B_HEADS = 8
B_DH = 128
B_KV_RANK = 512
IDX_HEADS = 16
IDX_DH = 64
TOPK_MAX = 256
C_HEADS = 16
C_KV_HEADS = 2
C_DH = 64
WINDOW = 128
REL_BUCKETS = 32
REL_MAX_DIST = 128
N_BRANCH = 3
BRANCH_W = 1024
FFN_RES = 0.5
N_MOD = 9
RMS_EPS = 1e-6
NEG_INF = -1e30
LOG2_E = math.log2(math.e)

VMEM_LIMIT_BYTES = 56 * 1024 * 1024

QBLK = 128
HGRN_CHUNK = 64
HGRN_SUB = 16
HGRN_HEADS_PER_STEP = 4
TOPK_GROUP = 4
B_SMALL_W = 128

PROJ_A = 0
PROJ_BQ = PROJ_A + 4 * A_WIDTH
PROJ_BIQ = PROJ_BQ + B_HEADS * B_DH
PROJ_CQ = PROJ_BIQ + IDX_HEADS * IDX_DH
PROJ_BLAT = PROJ_CQ + C_HEADS * C_DH
PROJ_SMALL = PROJ_BLAT + B_KV_RANK
PROJ_CK = PROJ_SMALL + B_SMALL_W
PROJ_CV = PROJ_CK + C_KV_HEADS * C_DH
PROJ_W = -(-(PROJ_CV + C_KV_HEADS * C_DH) // 1024) * 1024

BF16 = jnp.bfloat16
F32 = jnp.float32


def _cparams(n_grid_dims):
    return pltpu.CompilerParams(dimension_semantics=("arbitrary",) * n_grid_dims,
                                vmem_limit_bytes=VMEM_LIMIT_BYTES)


def _dot(a, b):
    return jnp.dot(a, b, preferred_element_type=F32)


def _dot_nt(a, b):
    return lax.dot_general(a, b, (((1,), (1,)), ((), ())), preferred_element_type=F32)


def _silu(x):
    return x * jax.nn.sigmoid(x)


def _lower_bound_kernel(x_ref, o_ref):
    x = x_ref[...]
    m = jnp.max(x, axis=0, keepdims=True)
    e = jnp.exp(x - m)
    p = e / jnp.sum(e, axis=0, keepdims=True)
    depth = x.shape[0]
    run = jnp.zeros_like(p[0:1])
    for l in range(depth):
        o_ref[l:l + 1, :] = run
        if l + 1 < depth:
            run = run + p[l + 1:l + 2]


def lower_bounds(lb_logits):
    return pl.pallas_call(
        _lower_bound_kernel,
        out_shape=jax.ShapeDtypeStruct(lb_logits.shape, F32),
        name="lower_bounds",
    )(lb_logits.astype(F32))


def _t5_bucket(dist):
    max_exact = REL_BUCKETS // 2
    d = jnp.maximum(dist, 0)
    df = jnp.maximum(d, 1).astype(F32)
    large = max_exact + (jnp.log(df / max_exact) / math.log(REL_MAX_DIST / max_exact)
                         * (REL_BUCKETS - max_exact)).astype(jnp.int32)
    large = jnp.minimum(large, REL_BUCKETS - 1)
    return jnp.where(d < max_exact, d, large)


def _bias_tile_kernel(tab_ref, o_ref, *, keys_on_rows):
    h = pl.program_id(0)
    row = lax.broadcasted_iota(jnp.int32, (QBLK, QBLK), 0)
    col = lax.broadcasted_iota(jnp.int32, (QBLK, QBLK), 1)
    t_minus_s = col - row if keys_on_rows else row - col
    for r in range(3):
        bucket = _t5_bucket(r * QBLK + t_minus_s)
        tile = jnp.zeros((QBLK, QBLK), F32)
        for b in range(REL_BUCKETS):
            tile = jnp.where(bucket == b, tab_ref[h, b], tile)
        if keys_on_rows:
            o_ref[r] = tile
        else:
            o_ref[r, 0] = tile


def bias_tiles(rel_table, *, keys_on_rows):
    n_heads = rel_table.shape[1]
    if keys_on_rows:
        out_spec = pl.BlockSpec((3, QBLK, QBLK), lambda h: (0, 0, h))
        out_shape = (3, QBLK, n_heads * QBLK)
    else:
        out_spec = pl.BlockSpec((3, 1, QBLK, QBLK), lambda h: (0, h, 0, 0))
        out_shape = (3, n_heads, QBLK, QBLK)
    return pl.pallas_call(
        functools.partial(_bias_tile_kernel, keys_on_rows=keys_on_rows),
        grid=(n_heads,),
        in_specs=[pl.BlockSpec(memory_space=pltpu.SMEM)],
        out_specs=out_spec,
        out_shape=jax.ShapeDtypeStruct(out_shape, F32),
        compiler_params=_cparams(1),
        name="bias_tiles",
    )(rel_table.T.astype(F32))


def _rms(x, g):
    return x * lax.rsqrt(jnp.mean(x * x, axis=-1, keepdims=True) + RMS_EPS) * g


def _modulate(x, g_ref, mod_ref, gi, shift_i, scale_i):
    y = _rms(x, g_ref[gi:gi + 1, :])
    return y * (1.0 + mod_ref[0, scale_i:scale_i + 1, :]) + mod_ref[0, shift_i:shift_i + 1, :]


def _normmod_kernel(x_ref, g_ref, mod_ref, h_ref, *, gi, shift_i, scale_i):
    h_ref[...] = _modulate(x_ref[...], g_ref, mod_ref, gi, shift_i, scale_i).astype(h_ref.dtype)


def normmod(x, gains, mod, seq, *, gi, shift_i, scale_i, tm=256):
    m, d = x.shape
    tpb = seq // tm
    return pl.pallas_call(
        functools.partial(_normmod_kernel, gi=gi, shift_i=shift_i, scale_i=scale_i),
        grid=(m // tm,),
        in_specs=[pl.BlockSpec((tm, d), lambda i: (i, 0)),
                  pl.BlockSpec(gains.shape, lambda i: (0, 0)),
                  pl.BlockSpec((1, N_MOD, d), lambda i: (i // tpb, 0, 0))],
        out_specs=pl.BlockSpec((tm, d), lambda i: (i, 0)),
        out_shape=jax.ShapeDtypeStruct((m, d), BF16),
        compiler_params=_cparams(1),
        name="normmod",
    )(x, gains, mod)


def _resid_kernel(x_ref, y_ref, g_ref, mod_ref, x_out_ref, *, gi, gate_i, coef):
    y = _rms(y_ref[...].astype(F32), g_ref[gi:gi + 1, :])
    x_out_ref[...] = x_ref[...] + coef * mod_ref[0, gate_i:gate_i + 1, :] * y


def resid(x, y, gains, mod, seq, *, gi, gate_i, coef, tm=256):
    m, d = x.shape
    tpb = seq // tm
    row = pl.BlockSpec((tm, d), lambda i: (i, 0))
    return pl.pallas_call(
        functools.partial(_resid_kernel, gi=gi, gate_i=gate_i, coef=coef),
        grid=(m // tm,),
        in_specs=[row, row,
                  pl.BlockSpec(gains.shape, lambda i: (0, 0)),
                  pl.BlockSpec((1, N_MOD, d), lambda i: (i // tpb, 0, 0))],
        out_specs=row,
        out_shape=jax.ShapeDtypeStruct((m, d), F32),
        compiler_params=_cparams(1),
        name="resid",
    )(x, y, gains, mod)


def _resid_normmod_kernel(x_ref, y_ref, g_ref, mod_ref, g2_ref, mod2_ref, x_out_ref, h_ref, *,
                          gi, gate_i, coef, gi2, shift_i, scale_i):
    y = _rms(y_ref[...].astype(F32), g_ref[gi:gi + 1, :])
    x_new = x_ref[...] + coef * mod_ref[0, gate_i:gate_i + 1, :] * y
    x_out_ref[...] = x_new
    h_ref[...] = _modulate(x_new, g2_ref, mod2_ref, gi2, shift_i, scale_i).astype(h_ref.dtype)


def resid_normmod(x, y, gains, mod, gains2, mod2, seq, *, gi, gate_i, coef, gi2, shift_i, scale_i, tm=256):
    m, d = x.shape
    tpb = seq // tm
    row = pl.BlockSpec((tm, d), lambda i: (i, 0))
    gspec = pl.BlockSpec(gains.shape, lambda i: (0, 0))
    mspec = pl.BlockSpec((1, N_MOD, d), lambda i: (i // tpb, 0, 0))
    return pl.pallas_call(
        functools.partial(_resid_normmod_kernel, gi=gi, gate_i=gate_i, coef=coef, gi2=gi2,
                          shift_i=shift_i, scale_i=scale_i),
        grid=(m // tm,),
        in_specs=[row, row, gspec, mspec, gspec, mspec],
        out_specs=[row, row],
        out_shape=[jax.ShapeDtypeStruct((m, d), F32), jax.ShapeDtypeStruct((m, d), BF16)],
        compiler_params=_cparams(1),
        name="resid_normmod",
    )(x, y, gains, mod, gains2, mod2)


def _mm_kernel(a_ref, w_ref, o_ref, *, act):
    acc = _dot(a_ref[...], w_ref[...])
    if act == "silu":
        acc = _silu(acc)
    o_ref[...] = acc.astype(o_ref.dtype)


def _layer_spec(tail_block, tail_index, layer):
    if layer is None:
        return pl.BlockSpec(tail_block, tail_index)
    return pl.BlockSpec((None,) + tail_block, lambda *g: (layer,) + tail_index(*g))


def _mxu_call(body, grid, in_specs, out_spec, out_shape, args, name, cast):
    if cast is None:
        return pl.pallas_call(body, grid=grid, in_specs=in_specs, out_specs=out_spec, out_shape=out_shape,
                              compiler_params=_cparams(2), name=name)(*args)
    src, rows_per_layer, layer = cast
    steps = grid[0] * grid[1]
    rows = rows_per_layer // steps
    assert rows * steps == rows_per_layer and rows % 16 == 0
    cols = src.shape[1]
    n_in = len(in_specs)

    def body_and_cast(*refs):
        refs[-1][...] = refs[n_in][...].astype(BF16)
        body(*refs[:n_in], refs[n_in + 1])

    return pl.pallas_call(
        body_and_cast, grid=grid,
        in_specs=list(in_specs) + [pl.BlockSpec((rows, cols), lambda i, j: (layer * steps + i * grid[1] + j, 0))],
        out_specs=[out_spec, pl.BlockSpec((rows, cols), lambda i, j: (i * grid[1] + j, 0))],
        out_shape=[out_shape, jax.ShapeDtypeStruct((rows_per_layer, cols), BF16)],
        compiler_params=_cparams(2), name=name)(*args, src)


def matmul(a, w, *, out_dtype, layer=None, act=None, cast=None, tm=1024, tn=512):
    m, k = a.shape
    n = w.shape[-1]
    tm, tn = min(tm, m), min(tn, n)
    assert m % tm == 0 and n % tn == 0
    return _mxu_call(
        functools.partial(_mm_kernel, act=act), (m // tm, n // tn),
        [pl.BlockSpec((tm, k), lambda i, j: (i, 0)),
         _layer_spec((k, tn), lambda i, j: (0, j), layer)],
        pl.BlockSpec((tm, tn), lambda i, j: (i, j)),
        jax.ShapeDtypeStruct((m, n), out_dtype), (a, w), "matmul", cast)


def _swiglu_kernel(a_ref, wu_ref, wv_ref, o_ref):
    a = a_ref[...]
    u = _dot(a, wu_ref[...])
    v = _dot(a, wv_ref[...])
    o_ref[...] = (_silu(u) * v).astype(o_ref.dtype)


def swiglu_in(a, w, *, cast=None, tm=1024, tn=512):
    m, k = a.shape
    f = w.shape[-1] // 2
    tm = min(tm, m)
    nf = f // tn
    return _mxu_call(
        _swiglu_kernel, (m // tm, nf),
        [pl.BlockSpec((tm, k), lambda i, j: (i, 0)),
         pl.BlockSpec((k, tn), lambda i, j: (0, j)),
         pl.BlockSpec((k, tn), lambda i, j: (0, j + nf))],
        pl.BlockSpec((tm, tn), lambda i, j: (i, j)),
        jax.ShapeDtypeStruct((m, f), BF16), (a, w, w), "swiglu_in", cast)


def _merge_kernel(h_ref, wga_ref, wgb_ref, wgc_ref, ya_ref, yb_ref, yc_ref, wb_ref, o_ref):
    h = h_ref[...]
    acc = jax.nn.sigmoid(_dot(h, wga_ref[...])) * _dot(ya_ref[...], wb_ref[0])
    acc = acc + jax.nn.sigmoid(_dot(h, wgb_ref[...])) * _dot(yb_ref[...], wb_ref[1])
    acc = acc + jax.nn.sigmoid(_dot(h, wgc_ref[...])) * _dot(yc_ref[...], wb_ref[2])
    o_ref[...] = acc.astype(o_ref.dtype)


def merge(h, w_gates, ya, yb, yc, w_branch, layer, *, cast=None, tm=1024, tn=256):
    m, k = h.shape
    kw = ya.shape[1]
    d = w_branch.shape[-1]
    tm = min(tm, m)
    nd = d // tn
    y_spec = pl.BlockSpec((tm, kw), lambda i, j: (i, 0))
    return _mxu_call(
        _merge_kernel, (m // tm, nd),
        [pl.BlockSpec((tm, k), lambda i, j: (i, 0)),
         _layer_spec((k, tn), lambda i, j: (0, j), layer),
         _layer_spec((k, tn), lambda i, j: (0, j + nd), layer),
         _layer_spec((k, tn), lambda i, j: (0, j + 2 * nd), layer),
         y_spec, y_spec, y_spec,
         _layer_spec((N_BRANCH, kw, tn), lambda i, j: (0, 0, j), layer)],
        pl.BlockSpec((tm, tn), lambda i, j: (i, j)),
        jax.ShapeDtypeStruct((m, d), BF16),
        (h, w_gates, w_gates, w_gates, ya, yb, yc, w_branch), "merge", cast)


def _kvup_kernel(lat_ref, g_ref, w_ref, k_ref, vt_ref):
    h = _rms(lat_ref[...], g_ref[...]).astype(BF16)
    kv = _dot(h, w_ref[...])
    k_ref[...] = kv[:, :B_DH].astype(k_ref.dtype)
    v = kv[:, B_DH:]
    for u in range(vt_ref.shape[0]):
        vt_ref[u] = v[u * QBLK:(u + 1) * QBLK].T.astype(vt_ref.dtype)


def kv_up(proj, kv_norm_g, w_kv_up, layer, *, tm=512):
    m = proj.shape[0]
    return pl.pallas_call(
        _kvup_kernel,
        grid=(m // tm,),
        in_specs=[pl.BlockSpec((tm, B_KV_RANK), lambda i: (i, PROJ_BLAT // B_KV_RANK)),
                  pl.BlockSpec((1, B_KV_RANK), lambda i: (0, 0)),
                  _layer_spec((B_KV_RANK, 2 * B_DH), lambda i: (0, 0), layer)],
        out_specs=[pl.BlockSpec((tm, B_DH), lambda i: (i, 0)),
                   pl.BlockSpec((tm // QBLK, B_DH, QBLK), lambda i: (i, 0, 0))],
        out_shape=[jax.ShapeDtypeStruct((m, B_DH), BF16),
                   jax.ShapeDtypeStruct((m // QBLK, B_DH, QBLK), BF16)],
        compiler_params=_cparams(1),
        name="kv_up",
    )(proj, kv_norm_g.reshape(1, B_KV_RANK), w_kv_up)


def _hgrn_chunk(q, fr, ir, gr, lb, ng, st, tril):
    cs, sub = HGRN_CHUNK, HGRN_SUB
    f = lb + (1.0 - lb) * jax.nn.sigmoid(fr)
    lf = jnp.log(jnp.maximum(f, 1e-20))
    k = 1.0 - f
    v = _silu(ir)
    b = jnp.dot(tril, lf, precision=lax.Precision.HIGHEST, preferred_element_type=F32) * LOG2_E
    o_inter = _dot_nt((q * jnp.exp2(b)).astype(BF16), st.astype(BF16))

    col_s = lax.broadcasted_iota(jnp.int32, (sub, cs), 1)
    blocks = []
    for si in range(cs // sub):
        lo = si * sub
        q_s = q[lo:lo + sub]
        b_s = b[lo:lo + sub]
        if si > 0:
            bref = b[lo - 1:lo]
            qt = (q_s * jnp.exp2(b_s - bref)).astype(BF16)
            kj = (k[:lo] * jnp.exp2(bref - b[:lo])).astype(BF16)
            attn = jnp.concatenate([_dot_nt(qt, kj), jnp.zeros((sub, cs - lo), F32)], axis=1)
        else:
            attn = jnp.zeros((sub, cs), F32)
        for s in range(sub):
            a = q_s * k[lo + s:lo + s + 1] * jnp.exp2(b_s - b[lo + s:lo + s + 1])
            attn = jnp.where(col_s == lo + s, jnp.sum(a, axis=-1, keepdims=True), attn)
        blocks.append(attn)
    attn = jnp.where(tril > 0.0, jnp.concatenate(blocks, axis=0), 0.0)
    o = o_inter + _dot(attn.astype(BF16), v.astype(BF16))

    b_end = b[cs - 1:cs]
    kd = (k * jnp.exp2(b_end - b)).astype(BF16)
    upd = _dot(v.T.astype(BF16), kd)
    st_new = st * jnp.exp2(b_end) + upd
    return _rms(o, ng) * _silu(gr), st_new


def _hgrn_kernel(q_ref, f_ref, i_ref, g_ref, lb_ref, ng_ref, o_ref, st_ref, *, seq):
    cs = HGRN_CHUNK
    st_ref[...] = jnp.zeros_like(st_ref)
    ng = ng_ref[...]
    row_c = lax.broadcasted_iota(jnp.int32, (cs, cs), 0)
    col_c = lax.broadcasted_iota(jnp.int32, (cs, cs), 1)
    tril = (col_c <= row_c).astype(F32)

    def chunk(c, carry):
        r0 = pl.multiple_of(c * cs, cs)
        for h in range(HGRN_HEADS_PER_STEP):
            cols = slice(h * A_DK, (h + 1) * A_DK)
            o, st_new = _hgrn_chunk(q_ref[pl.ds(r0, cs), cols], f_ref[pl.ds(r0, cs), cols],
                                    i_ref[pl.ds(r0, cs), cols], g_ref[pl.ds(r0, cs), cols],
                                    lb_ref[:, cols], ng, st_ref[h], tril)
            st_ref[h] = st_new
            o_ref[pl.ds(r0, cs), cols] = o.astype(o_ref.dtype)
        return carry

    lax.fori_loop(0, seq // cs, chunk, 0, unroll=2)


def hgrn2(proj, lb, norm_g, seq):
    m = proj.shape[0]
    bsz = m // seq
    nh = HGRN_HEADS_PER_STEP
    steps = A_HEADS // nh

    def col(part):
        return pl.BlockSpec((seq, nh * A_DK), lambda b, h: (b, part * steps + h))

    return pl.pallas_call(
        functools.partial(_hgrn_kernel, seq=seq),
        grid=(bsz, steps),
        in_specs=[col(0), col(1), col(2), col(3),
                  pl.BlockSpec((1, nh * A_DK), lambda b, h: (0, h)),
                  pl.BlockSpec((1, A_DV), lambda b, h: (0, 0))],
        out_specs=pl.BlockSpec((seq, nh * A_DV), lambda b, h: (b, h)),
        out_shape=jax.ShapeDtypeStruct((m, A_HEADS * A_DV), BF16),
        scratch_shapes=[pltpu.VMEM((nh, A_DV, A_DK), F32)],
        compiler_params=_cparams(2),
        name="hgrn2",
    )(proj, proj, proj, proj, lb.reshape(1, A_WIDTH), norm_g.reshape(1, A_DV))


def _dsa_kernel(q_ref, iq_ref, smq_ref, smk_ref, k_ref, vt_ref, bt_ref, o_ref,
                key_ref, msk_ref, lg_ref, mx_ref, ls_ref, acc_ref, *, nblk, topk):
    j = pl.program_id(1)
    row = lax.broadcasted_iota(jnp.int32, (QBLK, QBLK), 0)
    col = lax.broadcasted_iota(jnp.int32, (QBLK, QBLK), 1)
    causal = row <= col
    n_grp = j // TOPK_GROUP + 1
    n_pair = j // 2 + 1
    int_min = jnp.int32(-2 ** 31)

    def sort_key(score):
        bits = lax.bitcast_convert_type(score, jnp.int32)
        return bits ^ ((bits >> 31) & jnp.int32(0x7FFFFFFF))

    iq = iq_ref[...].astype(BF16)
    iqs = jnp.concatenate([iq[:, h * IDX_DH:(h + 1) * IDX_DH] for h in range(IDX_HEADS)], axis=0)
    iwt = (smq_ref[:, IDX_DH:IDX_DH + IDX_HEADS] * (IDX_HEADS ** -0.5 * IDX_DH ** -0.5)).T

    def visible(kb):
        return jnp.logical_or(kb < j, jnp.logical_and(kb == j, causal))

    def score_tile(kb):
        r0 = pl.multiple_of(kb * QBLK, QBLK)
        ik = smk_ref[pl.ds(r0, QBLK), 0:IDX_DH].astype(BF16)
        d = _dot_nt(ik, iqs)
        sc = jnp.zeros((QBLK, QBLK), F32)
        for h in range(IDX_HEADS):
            sc = sc + jnp.maximum(d[:, h * QBLK:(h + 1) * QBLK], 0.0) * iwt[h:h + 1, :]
        key_ref[kb] = sort_key(jnp.where(visible(kb), sc, NEG_INF))

    def pairwise(tile_fn):
        def step(kp, carry):
            tile_fn(2 * kp)
            tile_fn(2 * kp + 1)
            return carry
        lax.fori_loop(0, n_pair, step, 0)

    def groupwise(tile_fn):
        def step(gi, carry):
            for u in range(TOPK_GROUP):
                tile_fn(gi * TOPK_GROUP + u)
            return carry
        lax.fori_loop(0, n_grp, step, 0)

    pairwise(score_tile)

    def fill_step(kb, carry):
        key_ref[kb] = sort_key(jnp.full((QBLK, QBLK), NEG_INF, F32))
        return carry

    lax.fori_loop(2 * n_pair, n_grp * TOPK_GROUP, fill_step, 0)

    kf = jnp.float32(topk)

    def count(pred):
        def pair(kp, c):
            for u in range(2):
                kb = 2 * kp + u
                c = c + jnp.where(pred(kb, key_ref[kb]), 1.0, 0.0)
            return c
        c = lax.fori_loop(0, n_pair, pair, jnp.zeros((QBLK, QBLK), F32))
        return jnp.sum(c, axis=0, keepdims=True)

    def thr_step(it, cand):
        trial = cand | lax.shift_left(jnp.int32(1), 31 - it)
        t_s = trial ^ int_min
        cnt = count(lambda kb, key: key >= t_s)
        return jnp.where(cnt >= kf, trial, cand)

    cand = lax.fori_loop(0, 32, thr_step, jnp.zeros((1, QBLK), jnp.int32))
    thr = cand ^ int_min
    n_gt = count(lambda kb, key: key > thr)
    n_ge = count(lambda kb, key: key >= thr)
    need = kf - n_gt
    nbits = (nblk * QBLK - 1).bit_length()

    def tie_search():
        def idx_step(it, pos):
            trial = pos + lax.shift_left(jnp.int32(1), nbits - 1 - it)
            cnt = count(lambda kb, key: jnp.logical_and(key == thr, kb * QBLK + row < trial))
            return jnp.where(cnt < need, trial, pos)
        return lax.fori_loop(0, nbits, idx_step, jnp.zeros((1, QBLK), jnp.int32))

    has_ties = jnp.max(n_ge) > kf
    pos = lax.cond(has_ties, tie_search, lambda: jnp.full((1, QBLK), nblk * QBLK, jnp.int32))

    def mask_tile(kb):
        key = key_ref[kb]
        sel = jnp.logical_or(key > thr, jnp.logical_and(key == thr, kb * QBLK + row <= pos))
        msk_ref[kb] = jnp.where(jnp.logical_and(sel, visible(kb)), 0.0, NEG_INF)

    groupwise(mask_tile)

    q = q_ref[...].astype(BF16)
    qs = jnp.concatenate([q[:, h * B_DH:(h + 1) * B_DH] for h in range(B_HEADS)], axis=0)
    scale = B_DH ** -0.5
    mx_ref[...] = jnp.full(mx_ref.shape, NEG_INF, F32)

    def logit_tile(kb):
        r0 = pl.multiple_of(kb * QBLK, QBLK)
        lg = _dot_nt(k_ref[pl.ds(r0, QBLK), :], qs)
        msk = msk_ref[kb]
        lg = lg * scale + bt_ref[jnp.clip(j - kb, 0, 2)] + jnp.concatenate([msk] * B_HEADS, axis=1)
        lg_ref[kb] = lg
        mx_ref[...] = jnp.maximum(mx_ref[...], lg)

    groupwise(logit_tile)
    mx = jnp.max(mx_ref[...], axis=0, keepdims=True)
    ls_ref[...] = jnp.zeros_like(ls_ref)
    acc_ref[...] = jnp.zeros_like(acc_ref)

    def pv_tile(kb):
        p = jnp.exp(lg_ref[kb] - mx)
        ls_ref[...] += p
        acc_ref[...] += _dot(vt_ref[kb], p.astype(BF16))

    groupwise(pv_tile)
    out_t = acc_ref[...] / jnp.sum(ls_ref[...], axis=0, keepdims=True)
    for h in range(B_HEADS):
        o_ref[:, h * B_DH:(h + 1) * B_DH] = out_t[:, h * QBLK:(h + 1) * QBLK].T.astype(o_ref.dtype)


def dsa(proj, k, vt, bias_b, seq):
    m = proj.shape[0]
    bsz = m // seq
    nblk = seq // QBLK
    assert nblk % TOPK_GROUP == 0
    topk = min(TOPK_MAX, seq // 4)
    qw = B_HEADS * B_DH
    small_blk = PROJ_SMALL // B_SMALL_W
    cols = B_HEADS * QBLK
    return pl.pallas_call(
        functools.partial(_dsa_kernel, nblk=nblk, topk=topk),
        grid=(bsz, nblk),
        in_specs=[pl.BlockSpec((QBLK, qw), lambda b, j: (b * nblk + j, PROJ_BQ // qw)),
                  pl.BlockSpec((QBLK, qw), lambda b, j: (b * nblk + j, PROJ_BIQ // qw)),
                  pl.BlockSpec((QBLK, B_SMALL_W), lambda b, j: (b * nblk + j, small_blk)),
                  pl.BlockSpec((seq, B_SMALL_W), lambda b, j: (b, small_blk)),
                  pl.BlockSpec((seq, B_DH), lambda b, j: (b, 0)),
                  pl.BlockSpec((nblk, B_DH, QBLK), lambda b, j: (b, 0, 0)),
                  pl.BlockSpec((3, QBLK, cols), lambda b, j: (0, 0, 0))],
        out_specs=pl.BlockSpec((QBLK, qw), lambda b, j: (b * nblk + j, 0)),
        out_shape=jax.ShapeDtypeStruct((m, qw), BF16),
        scratch_shapes=[pltpu.VMEM((nblk, QBLK, QBLK), jnp.int32),
                        pltpu.VMEM((nblk, QBLK, QBLK), F32),
                        pltpu.VMEM((nblk, QBLK, cols), F32),
                        pltpu.VMEM((QBLK, cols), F32),
                        pltpu.VMEM((QBLK, cols), F32),
                        pltpu.VMEM((B_DH, cols), F32)],
        compiler_params=_cparams(2),
        name="dsa",
    )(proj, proj, proj, proj, k, vt, bias_b)


def _swa_kernel(sink_ref, q_ref, kc_ref, vc_ref, kp_ref, vp_ref, bt_ref, o_ref):
    j = pl.program_id(1)
    grp = C_HEADS // C_KV_HEADS
    row = lax.broadcasted_iota(jnp.int32, (QBLK, QBLK), 0)
    col = lax.broadcasted_iota(jnp.int32, (QBLK, QBLK), 1)
    cur_ok = jnp.concatenate([row <= col] * grp, axis=1)
    prev_ok = jnp.concatenate([jnp.logical_and(row > col, j > 0)] * grp, axis=1)
    q = q_ref[...].astype(BF16)
    scale = C_DH ** -0.5
    for g in range(C_KV_HEADS):
        heads = range(g * grp, (g + 1) * grp)
        kv_cols = slice(g * C_DH, (g + 1) * C_DH)
        lanes = slice(g * grp * QBLK, (g + 1) * grp * QBLK)
        qs = jnp.concatenate([q[:, h * C_DH:(h + 1) * C_DH] for h in heads], axis=0)
        sink = jnp.concatenate([jnp.full((1, QBLK), sink_ref[h], F32) for h in heads], axis=1)
        lc = _dot_nt(kc_ref[:, kv_cols].astype(BF16), qs)
        lp = _dot_nt(kp_ref[:, kv_cols].astype(BF16), qs)
        lc = jnp.where(cur_ok, lc * scale + bt_ref[0, :, lanes], NEG_INF)
        lp = jnp.where(prev_ok, lp * scale + bt_ref[1, :, lanes], NEG_INF)
        mx = jnp.maximum(jnp.maximum(jnp.max(lc, axis=0, keepdims=True),
                                     jnp.max(lp, axis=0, keepdims=True)), sink)
        pc = jnp.exp(lc - mx)
        pp = jnp.exp(lp - mx)
        den = (jnp.sum(pc, axis=0, keepdims=True) + jnp.sum(pp, axis=0, keepdims=True)
               + jnp.exp(sink - mx))
        o_t = (_dot(vc_ref[:, kv_cols].T.astype(BF16), pc.astype(BF16))
               + _dot(vp_ref[:, kv_cols].T.astype(BF16), pp.astype(BF16))) / den
        for u in range(0, grp, 2):
            pair = jnp.concatenate([o_t[:, u * QBLK:(u + 1) * QBLK],
                                    o_t[:, (u + 1) * QBLK:(u + 2) * QBLK]], axis=0)
            h0 = g * grp + u
            o_ref[:, h0 * C_DH:(h0 + 2) * C_DH] = pair.T.astype(o_ref.dtype)


def swa(proj, sinks, bias_c, seq):
    m = proj.shape[0]
    bsz = m // seq
    nblk = seq // QBLK
    qw = C_HEADS * C_DH
    kvw = C_KV_HEADS * C_DH

    def cur(col):
        return pl.BlockSpec((QBLK, kvw), lambda b, j: (b * nblk + j, col // kvw))

    def prev(col):
        return pl.BlockSpec((QBLK, kvw), lambda b, j: (b * nblk + jnp.maximum(j - 1, 0), col // kvw))

    return pl.pallas_call(
        _swa_kernel,
        grid=(bsz, nblk),
        in_specs=[pl.BlockSpec(memory_space=pltpu.SMEM),
                  pl.BlockSpec((QBLK, qw), lambda b, j: (b * nblk + j, PROJ_CQ // qw)),
                  cur(PROJ_CK), cur(PROJ_CV), prev(PROJ_CK), prev(PROJ_CV),
                  pl.BlockSpec((3, QBLK, C_HEADS * QBLK), lambda b, j: (0, 0, 0))],
        out_specs=pl.BlockSpec((QBLK, qw), lambda b, j: (b * nblk + j, 0)),
        out_shape=jax.ShapeDtypeStruct((m, qw), BF16),
        compiler_params=_cparams(2),
        name="swa",
    )(sinks.astype(F32), proj, proj, proj, proj, proj, bias_c)


def _pack_w_in(w):
    o = 0
    parts = {}
    for name, width in (("a", 4 * A_WIDTH), ("bq", B_HEADS * B_DH), ("blat", B_KV_RANK),
                        ("biq", IDX_HEADS * IDX_DH), ("bik", IDX_DH), ("biw", IDX_HEADS),
                        ("cq", C_HEADS * C_DH), ("ck", C_KV_HEADS * C_DH), ("cv", C_KV_HEADS * C_DH)):
        parts[name] = w[..., o:o + width]
        o += width
    gates = w[..., o:]
    def zeros(width):
        return jnp.zeros(w.shape[:-1] + (width,), w.dtype)
    proj = jnp.concatenate([parts[n] for n in ("a", "bq", "biq", "cq", "blat", "bik", "biw")]
                           + [zeros(B_SMALL_W - IDX_DH - IDX_HEADS), parts["ck"], parts["cv"]], axis=-1)
    proj = jnp.concatenate([proj, zeros(PROJ_W - proj.shape[-1])], axis=-1)
    return proj.astype(BF16), gates.astype(BF16)


def kernel(x, c, w_c_down, w_c_up, norm_gains, w_in, lb_logits, hgrn_norm, kv_norm, w_kv_up, rel_table,
           sinks, w_branch, w_out, ffn1_in, ffn1_out, ffn2_in, ffn2_out):
    bsz, seq, d = x.shape
    depth = w_in.shape[0]
    m = bsz * seq
    x = x.reshape(m, d).astype(F32)

    cond = matmul(c.astype(BF16), w_c_down.astype(BF16), out_dtype=BF16, act="silu")
    lbs = lower_bounds(lb_logits)
    bias_b = bias_tiles(rel_table[:, :B_HEADS], keys_on_rows=True)
    bias_c = bias_tiles(rel_table[:, B_HEADS:], keys_on_rows=True)
    w_c_up, w_kv_up, w_branch = w_c_up.astype(BF16), w_kv_up.astype(BF16), w_branch.astype(BF16)
    wp, wg = _pack_w_in(w_in)
    mods = [matmul(cond, w_c_up, layer=l, out_dtype=F32, tn=4096).reshape(bsz, N_MOD, d)
            for l in range(depth)]
    gains = norm_gains.astype(F32)

    def rows(w):
        return w.reshape(-1, w.shape[-1])
    k_ff, k_out = ffn1_out.shape[1], w_out.shape[1]
    src_ffn1_in, src_ffn1_out, src_ffn2_in, src_ffn2_out, src_out = (
        rows(t) for t in (ffn1_in, ffn1_out, ffn2_in, ffn2_out, w_out))

    w1_in = ffn1_in[0].astype(BF16)
    h = normmod(x, gains[0], mods[0], seq, gi=0, shift_i=0, scale_i=1)
    for l in range(depth):
        mod, g = mods[l], gains[l]

        u, w1_out = swiglu_in(h, w1_in, cast=(src_ffn1_out, k_ff, l))
        y, wo = matmul(u, w1_out, out_dtype=BF16, tn=1024, cast=(src_out, k_out, l))
        x, h = resid_normmod(x, y, g, mod, g, mod, seq, gi=1, gate_i=2, coef=FFN_RES,
                             gi2=2, shift_i=3, scale_i=4)

        proj = matmul(h, wp, layer=l, out_dtype=F32, tn=1024)
        ya = hgrn2(proj, lbs[l], hgrn_norm[l].astype(F32), seq)
        kb, vb = kv_up(proj, kv_norm[l].astype(F32), w_kv_up, l)
        yb = dsa(proj, kb, vb, bias_b, seq)
        yc = swa(proj, sinks[l], bias_c, seq)
        mix, w2_in = merge(h, wg, ya, yb, yc, w_branch, l, cast=(src_ffn2_in, d, l))
        y, w2_out = matmul(mix, wo, out_dtype=BF16, tn=1024, cast=(src_ffn2_out, k_ff, l))
        x, h = resid_normmod(x, y, g, mod, g, mod, seq, gi=3, gate_i=5, coef=1.0,
                             gi2=4, shift_i=6, scale_i=7)

        if l + 1 < depth:
            u, w1_in = swiglu_in(h, w2_in, cast=(src_ffn1_in, d, l + 1))
            y = matmul(u, w2_out, out_dtype=BF16, tn=1024)
            x, h = resid_normmod(x, y, g, mod, gains[l + 1], mods[l + 1], seq, gi=5, gate_i=8,
                                 coef=FFN_RES, gi2=0, shift_i=0, scale_i=1)
        else:
            u = swiglu_in(h, w2_in)
            y = matmul(u, w2_out, out_dtype=BF16, tn=1024)
            x = resid(x, y, g, mod, seq, gi=5, gate_i=8, coef=FFN_RES)
    return x.reshape(bsz, seq, d)
```

```python
import functools
import math

import jax
import jax.numpy as jnp
from jax import lax
from jax.experimental import pallas as pl
from jax.experimental.pallas import tpu as pltpu

A_HEADS = 8
A_DK = 128
A_DV = 128
A_WIDTH = A_HEADS * A_DK
B_HEADS = 8
B_DH = 128
B_KV_RANK = 512
IDX_HEADS = 16
IDX_DH = 64
TOPK_MAX = 256
C_HEADS = 16
C_KV_HEADS = 2
C_DH = 64
WINDOW = 128
REL_BUCKETS = 32
REL_MAX_DIST = 128
N_BRANCH = 3
BRANCH_W = 1024
FFN_RES = 0.5
N_MOD = 9
RMS_EPS = 1e-6
NEG_INF = -1e30
LOG2_E = math.log2(math.e)

VMEM_LIMIT_BYTES = 56 * 1024 * 1024

QBLK = 128
HGRN_CHUNK = 64
HGRN_SUB = 16
HGRN_HEADS_PER_STEP = 4
TOPK_GROUP = 4
B_SMALL_W = 128

PROJ_A = 0
PROJ_BQ = PROJ_A + 4 * A_WIDTH
PROJ_BIQ = PROJ_BQ + B_HEADS * B_DH
PROJ_CQ = PROJ_BIQ + IDX_HEADS * IDX_DH
PROJ_BLAT = PROJ_CQ + C_HEADS * C_DH
PROJ_SMALL = PROJ_BLAT + B_KV_RANK
PROJ_CK = PROJ_SMALL + B_SMALL_W
PROJ_CV = PROJ_CK + C_KV_HEADS * C_DH
PROJ_W = -(-(PROJ_CV + C_KV_HEADS * C_DH) // 1024) * 1024

BF16 = jnp.bfloat16
F32 = jnp.float32


def _cparams(n_grid_dims):
    return pltpu.CompilerParams(dimension_semantics=("arbitrary",) * n_grid_dims,
                                vmem_limit_bytes=VMEM_LIMIT_BYTES)


def _dot(a, b):
    return jnp.dot(a, b, preferred_element_type=F32)


def _dot_nt(a, b):
    return lax.dot_general(a, b, (((1,), (1,)), ((), ())), preferred_element_type=F32)


def _silu(x):
    return x * jax.nn.sigmoid(x)


def _lower_bound_kernel(x_ref, o_ref):
    x = x_ref[...]
    m = jnp.max(x, axis=0, keepdims=True)
    e = jnp.exp(x - m)
    p = e / jnp.sum(e, axis=0, keepdims=True)
    depth = x.shape[0]
    run = jnp.zeros_like(p[0:1])
    for l in range(depth):
        o_ref[l:l + 1, :] = run
        if l + 1 < depth:
            run = run + p[l + 1:l + 2]


def lower_bounds(lb_logits):
    return pl.pallas_call(
        _lower_bound_kernel,
        out_shape=jax.ShapeDtypeStruct(lb_logits.shape, F32),
        name="lower_bounds",
    )(lb_logits.astype(F32))


def _t5_bucket(dist):
    max_exact = REL_BUCKETS // 2
    d = jnp.maximum(dist, 0)
    df = jnp.maximum(d, 1).astype(F32)
    large = max_exact + (jnp.log(df / max_exact) / math.log(REL_MAX_DIST / max_exact)
                         * (REL_BUCKETS - max_exact)).astype(jnp.int32)
    large = jnp.minimum(large, REL_BUCKETS - 1)
    return jnp.where(d < max_exact, d, large)


def _bias_tile_kernel(tab_ref, o_ref, *, keys_on_rows):
    h = pl.program_id(0)
    row = lax.broadcasted_iota(jnp.int32, (QBLK, QBLK), 0)
    col = lax.broadcasted_iota(jnp.int32, (QBLK, QBLK), 1)
    t_minus_s = col - row if keys_on_rows else row - col
    for r in range(3):
        bucket = _t5_bucket(r * QBLK + t_minus_s)
        tile = jnp.zeros((QBLK, QBLK), F32)
        for b in range(REL_BUCKETS):
            tile = jnp.where(bucket == b, tab_ref[h, b], tile)
        if keys_on_rows:
            o_ref[r] = tile
        else:
            o_ref[r, 0] = tile


def bias_tiles(rel_table, *, keys_on_rows):
    n_heads = rel_table.shape[1]
    if keys_on_rows:
        out_spec = pl.BlockSpec((3, QBLK, QBLK), lambda h: (0, 0, h))
        out_shape = (3, QBLK, n_heads * QBLK)
    else:
        out_spec = pl.BlockSpec((3, 1, QBLK, QBLK), lambda h: (0, h, 0, 0))
        out_shape = (3, n_heads, QBLK, QBLK)
    return pl.pallas_call(
        functools.partial(_bias_tile_kernel, keys_on_rows=keys_on_rows),
        grid=(n_heads,),
        in_specs=[pl.BlockSpec(memory_space=pltpu.SMEM)],
        out_specs=out_spec,
        out_shape=jax.ShapeDtypeStruct(out_shape, F32),
        compiler_params=_cparams(1),
        name="bias_tiles",
    )(rel_table.T.astype(F32))


def _rms(x, g):
    return x * lax.rsqrt(jnp.mean(x * x, axis=-1, keepdims=True) + RMS_EPS) * g


def _modulate(x, g_ref, mod_ref, gi, shift_i, scale_i):
    y = _rms(x, g_ref[gi:gi + 1, :])
    return y * (1.0 + mod_ref[0, scale_i:scale_i + 1, :]) + mod_ref[0, shift_i:shift_i + 1, :]


def _normmod_kernel(x_ref, g_ref, mod_ref, h_ref, *, gi, shift_i, scale_i):
    h_ref[...] = _modulate(x_ref[...], g_ref, mod_ref, gi, shift_i, scale_i).astype(h_ref.dtype)


def normmod(x, gains, mod, seq, *, gi, shift_i, scale_i, tm=256):
    m, d = x.shape
    tpb = seq // tm
    return pl.pallas_call(
        functools.partial(_normmod_kernel, gi=gi, shift_i=shift_i, scale_i=scale_i),
        grid=(m // tm,),
        in_specs=[pl.BlockSpec((tm, d), lambda i: (i, 0)),
                  pl.BlockSpec(gains.shape, lambda i: (0, 0)),
                  pl.BlockSpec((1, N_MOD, d), lambda i: (i // tpb, 0, 0))],
        out_specs=pl.BlockSpec((tm, d), lambda i: (i, 0)),
        out_shape=jax.ShapeDtypeStruct((m, d), BF16),
        compiler_params=_cparams(1),
        name="normmod",
    )(x, gains, mod)


def _resid_kernel(x_ref, y_ref, g_ref, mod_ref, x_out_ref, *, gi, gate_i, coef):
    y = _rms(y_ref[...].astype(F32), g_ref[gi:gi + 1, :])
    x_out_ref[...] = x_ref[...] + coef * mod_ref[0, gate_i:gate_i + 1, :] * y


def resid(x, y, gains, mod, seq, *, gi, gate_i, coef, tm=256):
    m, d = x.shape
    tpb = seq // tm
    row = pl.BlockSpec((tm, d), lambda i: (i, 0))
    return pl.pallas_call(
        functools.partial(_resid_kernel, gi=gi, gate_i=gate_i, coef=coef),
        grid=(m // tm,),
        in_specs=[row, row,
                  pl.BlockSpec(gains.shape, lambda i: (0, 0)),
                  pl.BlockSpec((1, N_MOD, d), lambda i: (i // tpb, 0, 0))],
        out_specs=row,
        out_shape=jax.ShapeDtypeStruct((m, d), F32),
        compiler_params=_cparams(1),
        name="resid",
    )(x, y, gains, mod)


def _resid_normmod_kernel(x_ref, y_ref, g_ref, mod_ref, g2_ref, mod2_ref, x_out_ref, h_ref, *,
                          gi, gate_i, coef, gi2, shift_i, scale_i):
    y = _rms(y_ref[...].astype(F32), g_ref[gi:gi + 1, :])
    x_new = x_ref[...] + coef * mod_ref[0, gate_i:gate_i + 1, :] * y
    x_out_ref[...] = x_new
    h_ref[...] = _modulate(x_new, g2_ref, mod2_ref, gi2, shift_i, scale_i).astype(h_ref.dtype)


def resid_normmod(x, y, gains, mod, gains2, mod2, seq, *, gi, gate_i, coef, gi2, shift_i, scale_i, tm=256):
    m, d = x.shape
    tpb = seq // tm
    row = pl.BlockSpec((tm, d), lambda i: (i, 0))
    gspec = pl.BlockSpec(gains.shape, lambda i: (0, 0))
    mspec = pl.BlockSpec((1, N_MOD, d), lambda i: (i // tpb, 0, 0))
    return pl.pallas_call(
        functools.partial(_resid_normmod_kernel, gi=gi, gate_i=gate_i, coef=coef, gi2=gi2,
                          shift_i=shift_i, scale_i=scale_i),
        grid=(m // tm,),
        in_specs=[row, row, gspec, mspec, gspec, mspec],
        out_specs=[row, row],
        out_shape=[jax.ShapeDtypeStruct((m, d), F32), jax.ShapeDtypeStruct((m, d), BF16)],
        compiler_params=_cparams(1),
        name="resid_normmod",
    )(x, y, gains, mod, gains2, mod2)


def _mm_kernel(a_ref, w_ref, o_ref, *, act):
    acc = _dot(a_ref[...], w_ref[...])
    if act == "silu":
        acc = _silu(acc)
    o_ref[...] = acc.astype(o_ref.dtype)


def _layer_spec(tail_block, tail_index, layer):
    if layer is None:
        return pl.BlockSpec(tail_block, tail_index)
    return pl.BlockSpec((None,) + tail_block, lambda *g: (layer,) + tail_index(*g))


def _mxu_call(body, grid, in_specs, out_spec, out_shape, args, name, cast):
    if cast is None:
        return pl.pallas_call(body, grid=grid, in_specs=in_specs, out_specs=out_spec, out_shape=out_shape,
                              compiler_params=_cparams(2), name=name)(*args)
    src, rows_per_layer, layer = cast
    steps = grid[0] * grid[1]
    rows = rows_per_layer // steps
    assert rows * steps == rows_per_layer and rows % 16 == 0
    cols = src.shape[1]
    n_in = len(in_specs)

    def body_and_cast(*refs):
        refs[-1][...] = refs[n_in][...].astype(BF16)
        body(*refs[:n_in], refs[n_in + 1])

    return pl.pallas_call(
        body_and_cast, grid=grid,
        in_specs=list(in_specs) + [pl.BlockSpec((rows, cols), lambda i, j: (layer * steps + i * grid[1] + j, 0))],
        out_specs=[out_spec, pl.BlockSpec((rows, cols), lambda i, j: (i * grid[1] + j, 0))],
        out_shape=[out_shape, jax.ShapeDtypeStruct((rows_per_layer, cols), BF16)],
        compiler_params=_cparams(2), name=name)(*args, src)


def matmul(a, w, *, out_dtype, layer=None, act=None, cast=None, tm=1024, tn=512):
    m, k = a.shape
    n = w.shape[-1]
    tm, tn = min(tm, m), min(tn, n)
    assert m % tm == 0 and n % tn == 0
    return _mxu_call(
        functools.partial(_mm_kernel, act=act), (m // tm, n // tn),
        [pl.BlockSpec((tm, k), lambda i, j: (i, 0)),
         _layer_spec((k, tn), lambda i, j: (0, j), layer)],
        pl.BlockSpec((tm, tn), lambda i, j: (i, j)),
        jax.ShapeDtypeStruct((m, n), out_dtype), (a, w), "matmul", cast)


def _swiglu_kernel(a_ref, wu_ref, wv_ref, o_ref):
    a = a_ref[...]
    u = _dot(a, wu_ref[...])
    v = _dot(a, wv_ref[...])
    o_ref[...] = (_silu(u) * v).astype(o_ref.dtype)


def swiglu_in(a, w, *, cast=None, tm=1024, tn=512):
    m, k = a.shape
    f = w.shape[-1] // 2
    tm = min(tm, m)
    nf = f // tn
    return _mxu_call(
        _swiglu_kernel, (m // tm, nf),
        [pl.BlockSpec((tm, k), lambda i, j: (i, 0)),
         pl.BlockSpec((k, tn), lambda i, j: (0, j)),
         pl.BlockSpec((k, tn), lambda i, j: (0, j + nf))],
        pl.BlockSpec((tm, tn), lambda i, j: (i, j)),
        jax.ShapeDtypeStruct((m, f), BF16), (a, w, w), "swiglu_in", cast)


def _merge_kernel(h_ref, wga_ref, wgb_ref, wgc_ref, ya_ref, yb_ref, yc_ref, wb_ref, o_ref):
    h = h_ref[...]
    acc = jax.nn.sigmoid(_dot(h, wga_ref[...])) * _dot(ya_ref[...], wb_ref[0])
    acc = acc + jax.nn.sigmoid(_dot(h, wgb_ref[...])) * _dot(yb_ref[...], wb_ref[1])
    acc = acc + jax.nn.sigmoid(_dot(h, wgc_ref[...])) * _dot(yc_ref[...], wb_ref[2])
    o_ref[...] = acc.astype(o_ref.dtype)


def merge(h, w_gates, ya, yb, yc, w_branch, layer, *, cast=None, tm=1024, tn=256):
    m, k = h.shape
    kw = ya.shape[1]
    d = w_branch.shape[-1]
    tm = min(tm, m)
    nd = d // tn
    y_spec = pl.BlockSpec((tm, kw), lambda i, j: (i, 0))
    return _mxu_call(
        _merge_kernel, (m // tm, nd),
        [pl.BlockSpec((tm, k), lambda i, j: (i, 0)),
         _layer_spec((k, tn), lambda i, j: (0, j), layer),
         _layer_spec((k, tn), lambda i, j: (0, j + nd), layer),
         _layer_spec((k, tn), lambda i, j: (0, j + 2 * nd), layer),
         y_spec, y_spec, y_spec,
         _layer_spec((N_BRANCH, kw, tn), lambda i, j: (0, 0, j), layer)],
        pl.BlockSpec((tm, tn), lambda i, j: (i, j)),
        jax.ShapeDtypeStruct((m, d), BF16),
        (h, w_gates, w_gates, w_gates, ya, yb, yc, w_branch), "merge", cast)


def _kvup_kernel(lat_ref, g_ref, w_ref, k_ref, vt_ref):
    h = _rms(lat_ref[...], g_ref[...]).astype(BF16)
    kv = _dot(h, w_ref[...])
    k_ref[...] = kv[:, :B_DH].astype(k_ref.dtype)
    v = kv[:, B_DH:]
    for u in range(vt_ref.shape[0]):
        vt_ref[u] = v[u * QBLK:(u + 1) * QBLK].T.astype(vt_ref.dtype)


def kv_up(proj, kv_norm_g, w_kv_up, layer, *, tm=512):
    m = proj.shape[0]
    return pl.pallas_call(
        _kvup_kernel,
        grid=(m // tm,),
        in_specs=[pl.BlockSpec((tm, B_KV_RANK), lambda i: (i, PROJ_BLAT // B_KV_RANK)),
                  pl.BlockSpec((1, B_KV_RANK), lambda i: (0, 0)),
                  _layer_spec((B_KV_RANK, 2 * B_DH), lambda i: (0, 0), layer)],
        out_specs=[pl.BlockSpec((tm, B_DH), lambda i: (i, 0)),
                   pl.BlockSpec((tm // QBLK, B_DH, QBLK), lambda i: (i, 0, 0))],
        out_shape=[jax.ShapeDtypeStruct((m, B_DH), BF16),
                   jax.ShapeDtypeStruct((m // QBLK, B_DH, QBLK), BF16)],
        compiler_params=_cparams(1),
        name="kv_up",
    )(proj, kv_norm_g.reshape(1, B_KV_RANK), w_kv_up)


def _hgrn_chunk(q, fr, ir, gr, lb, ng, st, tril):
    cs, sub = HGRN_CHUNK, HGRN_SUB
    f = lb + (1.0 - lb) * jax.nn.sigmoid(fr)
    lf = jnp.log(jnp.maximum(f, 1e-20))
    k = 1.0 - f
    v = _silu(ir)
    b = jnp.dot(tril, lf, precision=lax.Precision.HIGHEST, preferred_element_type=F32) * LOG2_E
    o_inter = _dot_nt((q * jnp.exp2(b)).astype(BF16), st.astype(BF16))

    col_s = lax.broadcasted_iota(jnp.int32, (sub, cs), 1)
    blocks = []
    for si in range(cs // sub):
        lo = si * sub
        q_s = q[lo:lo + sub]
        b_s = b[lo:lo + sub]
        if si > 0:
            bref = b[lo - 1:lo]
            qt = (q_s * jnp.exp2(b_s - bref)).astype(BF16)
            kj = (k[:lo] * jnp.exp2(bref - b[:lo])).astype(BF16)
            attn = jnp.concatenate([_dot_nt(qt, kj), jnp.zeros((sub, cs - lo), F32)], axis=1)
        else:
            attn = jnp.zeros((sub, cs), F32)
        for s in range(sub):
            a = q_s * k[lo + s:lo + s + 1] * jnp.exp2(b_s - b[lo + s:lo + s + 1])
            attn = jnp.where(col_s == lo + s, jnp.sum(a, axis=-1, keepdims=True), attn)
        blocks.append(attn)
    attn = jnp.where(tril > 0.0, jnp.concatenate(blocks, axis=0), 0.0)
    o = o_inter + _dot(attn.astype(BF16), v.astype(BF16))

    b_end = b[cs - 1:cs]
    kd = (k * jnp.exp2(b_end - b)).astype(BF16)
    upd = _dot(v.T.astype(BF16), kd)
    st_new = st * jnp.exp2(b_end) + upd
    return _rms(o, ng) * _silu(gr), st_new


def _hgrn_kernel(q_ref, f_ref, i_ref, g_ref, lb_ref, ng_ref, o_ref, st_ref, *, seq):
    cs = HGRN_CHUNK
    st_ref[...] = jnp.zeros_like(st_ref)
    ng = ng_ref[...]
    row_c = lax.broadcasted_iota(jnp.int32, (cs, cs), 0)
    col_c = lax.broadcasted_iota(jnp.int32, (cs, cs), 1)
    tril = (col_c <= row_c).astype(F32)

    def chunk(c, carry):
        r0 = pl.multiple_of(c * cs, cs)
        for h in range(HGRN_HEADS_PER_STEP):
            cols = slice(h * A_DK, (h + 1) * A_DK)
            o, st_new = _hgrn_chunk(q_ref[pl.ds(r0, cs), cols], f_ref[pl.ds(r0, cs), cols],
                                    i_ref[pl.ds(r0, cs), cols], g_ref[pl.ds(r0, cs), cols],
                                    lb_ref[:, cols], ng, st_ref[h], tril)
            st_ref[h] = st_new
            o_ref[pl.ds(r0, cs), cols] = o.astype(o_ref.dtype)
        return carry

    lax.fori_loop(0, seq // cs, chunk, 0, unroll=2)


def hgrn2(proj, lb, norm_g, seq):
    m = proj.shape[0]
    bsz = m // seq
    nh = HGRN_HEADS_PER_STEP
    steps = A_HEADS // nh

    def col(part):
        return pl.BlockSpec((seq, nh * A_DK), lambda b, h: (b, part * steps + h))

    return pl.pallas_call(
        functools.partial(_hgrn_kernel, seq=seq),
        grid=(bsz, steps),
        in_specs=[col(0), col(1), col(2), col(3),
                  pl.BlockSpec((1, nh * A_DK), lambda b, h: (0, h)),
                  pl.BlockSpec((1, A_DV), lambda b, h: (0, 0))],
        out_specs=pl.BlockSpec((seq, nh * A_DV), lambda b, h: (b, h)),
        out_shape=jax.ShapeDtypeStruct((m, A_HEADS * A_DV), BF16),
        scratch_shapes=[pltpu.VMEM((nh, A_DV, A_DK), F32)],
        compiler_params=_cparams(2),
        name="hgrn2",
    )(proj, proj, proj, proj, lb.reshape(1, A_WIDTH), norm_g.reshape(1, A_DV))


def _dsa_kernel(q_ref, iq_ref, smq_ref, smk_ref, k_ref, vt_ref, bt_ref, o_ref,
                key_ref, lg_ref, mx_ref, ls_ref, acc_ref, *, nblk, topk):
    j = pl.program_id(1)
    row = lax.broadcasted_iota(jnp.int32, (QBLK, QBLK), 0)
    col = lax.broadcasted_iota(jnp.int32, (QBLK, QBLK), 1)
    causal = row <= col
    n_grp = j // TOPK_GROUP + 1
    n_pair = j // 2 + 1
    int_min = jnp.int32(-2 ** 31)

    def sort_key(score):
        bits = lax.bitcast_convert_type(score, jnp.int32)
        return bits ^ ((bits >> 31) & jnp.int32(0x7FFFFFFF))

    iq = iq_ref[...].astype(BF16)
    iqs = jnp.concatenate([iq[:, h * IDX_DH:(h + 1) * IDX_DH] for h in range(IDX_HEADS)], axis=0)
    iwt = (smq_ref[:, IDX_DH:IDX_DH + IDX_HEADS] * (IDX_HEADS ** -0.5 * IDX_DH ** -0.5)).T

    def visible(kb):
        return jnp.logical_or(kb < j, jnp.logical_and(kb == j, causal))

    def score_tile(kb):
        r0 = pl.multiple_of(kb * QBLK, QBLK)
        ik = smk_ref[pl.ds(r0, QBLK), 0:IDX_DH].astype(BF16)
        d = _dot_nt(ik, iqs)
        sc = jnp.zeros((QBLK, QBLK), F32)
        for h in range(IDX_HEADS):
            sc = sc + jnp.maximum(d[:, h * QBLK:(h + 1) * QBLK], 0.0) * iwt[h:h + 1, :]
        key_ref[kb] = sort_key(jnp.where(visible(kb), sc, NEG_INF))

    def pairwise(tile_fn):
        def step(kp, carry):
            tile_fn(2 * kp)
            tile_fn(2 * kp + 1)
            return carry
        lax.fori_loop(0, n_pair, step, 0)

    def groupwise(tile_fn):
        def step(gi, carry):
            for u in range(TOPK_GROUP):
                tile_fn(gi * TOPK_GROUP + u)
            return carry
        lax.fori_loop(0, n_grp, step, 0)

    pairwise(score_tile)

    def fill_step(kb, carry):
        key_ref[kb] = sort_key(jnp.full((QBLK, QBLK), NEG_INF, F32))
        return carry

    lax.fori_loop(2 * n_pair, n_grp * TOPK_GROUP, fill_step, 0)

    kf = jnp.float32(topk)

    def count(pred):
        def pair(kp, c):
            for u in range(2):
                kb = 2 * kp + u
                c = c + jnp.where(pred(kb, key_ref[kb]), 1.0, 0.0)
            return c
        c = lax.fori_loop(0, n_pair, pair, jnp.zeros((QBLK, QBLK), F32))
        return jnp.sum(c, axis=0, keepdims=True)

    def thr_step(it, cand):
        trial = cand | lax.shift_left(jnp.int32(1), 31 - it)
        t_s = trial ^ int_min
        cnt = count(lambda kb, key: key >= t_s)
        return jnp.where(cnt >= kf, trial, cand)

    def search():
        return lax.fori_loop(0, 32, thr_step, jnp.zeros((1, QBLK), jnp.int32)) ^ int_min

    thr = lax.cond((j + 1) * QBLK <= topk, lambda: jnp.full((1, QBLK), int_min, jnp.int32), search)
    n_gt = count(lambda kb, key: key > thr)
    n_ge = count(lambda kb, key: key >= thr)
    need = kf - n_gt
    nbits = (nblk * QBLK - 1).bit_length()

    def tie_search():
        def idx_step(it, pos):
            trial = pos + lax.shift_left(jnp.int32(1), nbits - 1 - it)
            cnt = count(lambda kb, key: jnp.logical_and(key == thr, kb * QBLK + row < trial))
            return jnp.where(cnt < need, trial, pos)
        return lax.fori_loop(0, nbits, idx_step, jnp.zeros((1, QBLK), jnp.int32))

    has_ties = jnp.max(n_ge) > kf
    pos = lax.cond(has_ties, tie_search, lambda: jnp.full((1, QBLK), nblk * QBLK, jnp.int32))

    q = q_ref[...].astype(BF16)
    qs = jnp.concatenate([q[:, h * B_DH:(h + 1) * B_DH] for h in range(B_HEADS)], axis=0)
    scale = B_DH ** -0.5
    mx_ref[...] = jnp.full(mx_ref.shape, NEG_INF, F32)

    def logit_tile(kb):
        r0 = pl.multiple_of(kb * QBLK, QBLK)
        lg = _dot_nt(k_ref[pl.ds(r0, QBLK), :], qs)
        key = key_ref[kb]
        sel = jnp.logical_or(key > thr, jnp.logical_and(key == thr, kb * QBLK + row <= pos))
        msk = jnp.where(jnp.logical_and(sel, visible(kb)), 0.0, NEG_INF)
        lg = lg * scale + bt_ref[jnp.clip(j - kb, 0, 2)] + jnp.concatenate([msk] * B_HEADS, axis=1)
        lg_ref[kb] = lg
        mx_ref[...] = jnp.maximum(mx_ref[...], lg)

    groupwise(logit_tile)
    mx = jnp.max(mx_ref[...], axis=0, keepdims=True)
    ls_ref[...] = jnp.zeros_like(ls_ref)
    acc_ref[...] = jnp.zeros_like(acc_ref)

    def pv_tile(kb):
        p = jnp.exp(lg_ref[kb] - mx)
        ls_ref[...] += p
        acc_ref[...] += _dot(vt_ref[kb], p.astype(BF16))

    groupwise(pv_tile)
    out_t = acc_ref[...] / jnp.sum(ls_ref[...], axis=0, keepdims=True)
    for h in range(B_HEADS):
        o_ref[:, h * B_DH:(h + 1) * B_DH] = out_t[:, h * QBLK:(h + 1) * QBLK].T.astype(o_ref.dtype)


def dsa(proj, k, vt, bias_b, seq):
    m = proj.shape[0]
    bsz = m // seq
    nblk = seq // QBLK
    assert nblk % TOPK_GROUP == 0
    topk = min(TOPK_MAX, seq // 4)
    qw = B_HEADS * B_DH
    small_blk = PROJ_SMALL // B_SMALL_W
    cols = B_HEADS * QBLK
    return pl.pallas_call(
        functools.partial(_dsa_kernel, nblk=nblk, topk=topk),
        grid=(bsz, nblk),
        in_specs=[pl.BlockSpec((QBLK, qw), lambda b, j: (b * nblk + j, PROJ_BQ // qw)),
                  pl.BlockSpec((QBLK, qw), lambda b, j: (b * nblk + j, PROJ_BIQ // qw)),
                  pl.BlockSpec((QBLK, B_SMALL_W), lambda b, j: (b * nblk + j, small_blk)),
                  pl.BlockSpec((seq, B_SMALL_W), lambda b, j: (b, small_blk)),
                  pl.BlockSpec((seq, B_DH), lambda b, j: (b, 0)),
                  pl.BlockSpec((nblk, B_DH, QBLK), lambda b, j: (b, 0, 0)),
                  pl.BlockSpec((3, QBLK, cols), lambda b, j: (0, 0, 0))],
        out_specs=pl.BlockSpec((QBLK, qw), lambda b, j: (b * nblk + j, 0)),
        out_shape=jax.ShapeDtypeStruct((m, qw), BF16),
        scratch_shapes=[pltpu.VMEM((nblk, QBLK, QBLK), jnp.int32),
                        pltpu.VMEM((nblk, QBLK, cols), F32),
                        pltpu.VMEM((QBLK, cols), F32),
                        pltpu.VMEM((QBLK, cols), F32),
                        pltpu.VMEM((B_DH, cols), F32)],
        compiler_params=_cparams(2),
        name="dsa",
    )(proj, proj, proj, proj, k, vt, bias_b)


def _swa_kernel(sink_ref, q_ref, kc_ref, vc_ref, kp_ref, vp_ref, bt_ref, o_ref):
    j = pl.program_id(1)
    grp = C_HEADS // C_KV_HEADS
    row = lax.broadcasted_iota(jnp.int32, (QBLK, QBLK), 0)
    col = lax.broadcasted_iota(jnp.int32, (QBLK, QBLK), 1)
    cur_ok = jnp.concatenate([row <= col] * grp, axis=1)
    prev_ok = jnp.concatenate([jnp.logical_and(row > col, j > 0)] * grp, axis=1)
    q = q_ref[...].astype(BF16)
    scale = C_DH ** -0.5
    for g in range(C_KV_HEADS):
        heads = range(g * grp, (g + 1) * grp)
        kv_cols = slice(g * C_DH, (g + 1) * C_DH)
        lanes = slice(g * grp * QBLK, (g + 1) * grp * QBLK)
        qs = jnp.concatenate([q[:, h * C_DH:(h + 1) * C_DH] for h in heads], axis=0)
        sink = jnp.concatenate([jnp.full((1, QBLK), sink_ref[h], F32) for h in heads], axis=1)
        lc = _dot_nt(kc_ref[:, kv_cols].astype(BF16), qs)
        lp = _dot_nt(kp_ref[:, kv_cols].astype(BF16), qs)
        lc = jnp.where(cur_ok, lc * scale + bt_ref[0, :, lanes], NEG_INF)
        lp = jnp.where(prev_ok, lp * scale + bt_ref[1, :, lanes], NEG_INF)
        mx = jnp.maximum(jnp.maximum(jnp.max(lc, axis=0, keepdims=True),
                                     jnp.max(lp, axis=0, keepdims=True)), sink)
        pc = jnp.exp(lc - mx)
        pp = jnp.exp(lp - mx)
        den = (jnp.sum(pc, axis=0, keepdims=True) + jnp.sum(pp, axis=0, keepdims=True)
               + jnp.exp(sink - mx))
        o_t = (_dot(vc_ref[:, kv_cols].T.astype(BF16), pc.astype(BF16))
               + _dot(vp_ref[:, kv_cols].T.astype(BF16), pp.astype(BF16))) / den
        for u in range(0, grp, 2):
            pair = jnp.concatenate([o_t[:, u * QBLK:(u + 1) * QBLK],
                                    o_t[:, (u + 1) * QBLK:(u + 2) * QBLK]], axis=0)
            h0 = g * grp + u
            o_ref[:, h0 * C_DH:(h0 + 2) * C_DH] = pair.T.astype(o_ref.dtype)


def swa(proj, sinks, bias_c, seq):
    m = proj.shape[0]
    bsz = m // seq
    nblk = seq // QBLK
    qw = C_HEADS * C_DH
    kvw = C_KV_HEADS * C_DH

    def cur(col):
        return pl.BlockSpec((QBLK, kvw), lambda b, j: (b * nblk + j, col // kvw))

    def prev(col):
        return pl.BlockSpec((QBLK, kvw), lambda b, j: (b * nblk + jnp.maximum(j - 1, 0), col // kvw))

    return pl.pallas_call(
        _swa_kernel,
        grid=(bsz, nblk),
        in_specs=[pl.BlockSpec(memory_space=pltpu.SMEM),
                  pl.BlockSpec((QBLK, qw), lambda b, j: (b * nblk + j, PROJ_CQ // qw)),
                  cur(PROJ_CK), cur(PROJ_CV), prev(PROJ_CK), prev(PROJ_CV),
                  pl.BlockSpec((3, QBLK, C_HEADS * QBLK), lambda b, j: (0, 0, 0))],
        out_specs=pl.BlockSpec((QBLK, qw), lambda b, j: (b * nblk + j, 0)),
        out_shape=jax.ShapeDtypeStruct((m, qw), BF16),
        compiler_params=_cparams(2),
        name="swa",
    )(sinks.astype(F32), proj, proj, proj, proj, proj, bias_c)


def _pack_w_in(w):
    o = 0
    parts = {}
    for name, width in (("a", 4 * A_WIDTH), ("bq", B_HEADS * B_DH), ("blat", B_KV_RANK),
                        ("biq", IDX_HEADS * IDX_DH), ("bik", IDX_DH), ("biw", IDX_HEADS),
                        ("cq", C_HEADS * C_DH), ("ck", C_KV_HEADS * C_DH), ("cv", C_KV_HEADS * C_DH)):
        parts[name] = w[..., o:o + width]
        o += width
    gates = w[..., o:]
    def zeros(width):
        return jnp.zeros(w.shape[:-1] + (width,), w.dtype)
    proj = jnp.concatenate([parts[n] for n in ("a", "bq", "biq", "cq", "blat", "bik", "biw")]
                           + [zeros(B_SMALL_W - IDX_DH - IDX_HEADS), parts["ck"], parts["cv"]], axis=-1)
    proj = jnp.concatenate([proj, zeros(PROJ_W - proj.shape[-1])], axis=-1)
    return proj.astype(BF16), gates.astype(BF16)


def kernel(x, c, w_c_down, w_c_up, norm_gains, w_in, lb_logits, hgrn_norm, kv_norm, w_kv_up, rel_table,
           sinks, w_branch, w_out, ffn1_in, ffn1_out, ffn2_in, ffn2_out):
    bsz, seq, d = x.shape
    depth = w_in.shape[0]
    m = bsz * seq
    x = x.reshape(m, d).astype(F32)

    cond = matmul(c.astype(BF16), w_c_down.astype(BF16), out_dtype=BF16, act="silu")
    lbs = lower_bounds(lb_logits)
    bias_b = bias_tiles(rel_table[:, :B_HEADS], keys_on_rows=True)
    bias_c = bias_tiles(rel_table[:, B_HEADS:], keys_on_rows=True)
    w_c_up, w_kv_up, w_branch = w_c_up.astype(BF16), w_kv_up.astype(BF16), w_branch.astype(BF16)
    wp, wg = _pack_w_in(w_in)
    mods = [matmul(cond, w_c_up, layer=l, out_dtype=F32, tn=4096).reshape(bsz, N_MOD, d)
            for l in range(depth)]
    gains = norm_gains.astype(F32)

    def rows(w):
        return w.reshape(-1, w.shape[-1])
    k_ff, k_out = ffn1_out.shape[1], w_out.shape[1]
    src_ffn1_in, src_ffn1_out, src_ffn2_in, src_ffn2_out, src_out = (
        rows(t) for t in (ffn1_in, ffn1_out, ffn2_in, ffn2_out, w_out))

    w1_in = ffn1_in[0].astype(BF16)
    h = normmod(x, gains[0], mods[0], seq, gi=0, shift_i=0, scale_i=1)
    for l in range(depth):
        mod, g = mods[l], gains[l]

        u, w1_out = swiglu_in(h, w1_in, cast=(src_ffn1_out, k_ff, l))
        y, wo = matmul(u, w1_out, out_dtype=BF16, tn=1024, cast=(src_out, k_out, l))
        x, h = resid_normmod(x, y, g, mod, g, mod, seq, gi=1, gate_i=2, coef=FFN_RES,
                             gi2=2, shift_i=3, scale_i=4)

        proj = matmul(h, wp, layer=l, out_dtype=F32, tn=1024)
        ya = hgrn2(proj, lbs[l], hgrn_norm[l].astype(F32), seq)
        kb, vb = kv_up(proj, kv_norm[l].astype(F32), w_kv_up, l)
        yb = dsa(proj, kb, vb, bias_b, seq)
        yc = swa(proj, sinks[l], bias_c, seq)
        mix, w2_in = merge(h, wg, ya, yb, yc, w_branch, l, cast=(src_ffn2_in, d, l))
        y, w2_out = matmul(mix, wo, out_dtype=BF16, tn=1024, cast=(src_ffn2_out, k_ff, l))
        x, h = resid_normmod(x, y, g, mod, g, mod, seq, gi=3, gate_i=5, coef=1.0,
                             gi2=4, shift_i=6, scale_i=7)

        if l + 1 < depth:
            u, w1_in = swiglu_in(h, w2_in, cast=(src_ffn1_in, d, l + 1))
            y = matmul(u, w2_out, out_dtype=BF16, tn=1024)
            x, h = resid_normmod(x, y, g, mod, gains[l + 1], mods[l + 1], seq, gi=5, gate_i=8,
                                 coef=FFN_RES, gi2=0, shift_i=0, scale_i=1)
        else:
            u = swiglu_in(h, w2_in)
            y = matmul(u, w2_out, out_dtype=BF16, tn=1024)
            x = resid(x, y, g, mod, seq, gi=5, gate_i=8, coef=FFN_RES)
    return x.reshape(bsz, seq, d)
```

```python
import functools
import math

import jax
import jax.numpy as jnp
from jax import lax
from jax.experimental import pallas as pl
from jax.experimental.pallas import tpu as pltpu

A_HEADS = 8
A_DK = 128
A_DV = 128
A_WIDTH = A_HEADS * A_DK
B_HEADS = 8
B_DH = 128
B_KV_RANK = 512
IDX_HEADS = 16
IDX_DH = 64
TOPK_MAX = 256
C_HEADS = 16
C_KV_HEADS = 2
C_DH = 64
WINDOW = 128
REL_BUCKETS = 32
REL_MAX_DIST = 128
N_BRANCH = 3
BRANCH_W = 1024
FFN_RES = 0.5
N_MOD = 9
RMS_EPS = 1e-6
NEG_INF = -1e30
LOG2_E = math.log2(math.e)

VMEM_LIMIT_BYTES = 56 * 1024 * 1024

QBLK = 128
HGRN_CHUNK = 64
HGRN_SUB = 16
HGRN_HEADS_PER_STEP = 4
TOPK_GROUP = 4
B_SMALL_W = 128

PROJ_A = 0
PROJ_BQ = PROJ_A + 4 * A_WIDTH
PROJ_BIQ = PROJ_BQ + B_HEADS * B_DH
PROJ_CQ = PROJ_BIQ + IDX_HEADS * IDX_DH
PROJ_BLAT = PROJ_CQ + C_HEADS * C_DH
PROJ_SMALL = PROJ_BLAT + B_KV_RANK
PROJ_CK = PROJ_SMALL + B_SMALL_W
PROJ_CV = PROJ_CK + C_KV_HEADS * C_DH
PROJ_W = -(-(PROJ_CV + C_KV_HEADS * C_DH) // 1024) * 1024

BF16 = jnp.bfloat16
F32 = jnp.float32


def _cparams(n_grid_dims):
    return pltpu.CompilerParams(dimension_semantics=("arbitrary",) * n_grid_dims,
                                vmem_limit_bytes=VMEM_LIMIT_BYTES)


def _dot(a, b):
    return jnp.dot(a, b, preferred_element_type=F32)


def _dot_nt(a, b):
    return lax.dot_general(a, b, (((1,), (1,)), ((), ())), preferred_element_type=F32)


def _silu(x):
    return x * jax.nn.sigmoid(x)


def _lower_bound_kernel(x_ref, o_ref):
    x = x_ref[...]
    m = jnp.max(x, axis=0, keepdims=True)
    e = jnp.exp(x - m)
    p = e / jnp.sum(e, axis=0, keepdims=True)
    depth = x.shape[0]
    run = jnp.zeros_like(p[0:1])
    for l in range(depth):
        o_ref[l:l + 1, :] = run
        if l + 1 < depth:
            run = run + p[l + 1:l + 2]


def lower_bounds(lb_logits):
    return pl.pallas_call(
        _lower_bound_kernel,
        out_shape=jax.ShapeDtypeStruct(lb_logits.shape, F32),
        name="lower_bounds",
    )(lb_logits.astype(F32))


def _t5_bucket(dist):
    max_exact = REL_BUCKETS // 2
    d = jnp.maximum(dist, 0)
    df = jnp.maximum(d, 1).astype(F32)
    large = max_exact + (jnp.log(df / max_exact) / math.log(REL_MAX_DIST / max_exact)
                         * (REL_BUCKETS - max_exact)).astype(jnp.int32)
    large = jnp.minimum(large, REL_BUCKETS - 1)
    return jnp.where(d < max_exact, d, large)


def _bias_tile_kernel(tab_ref, o_ref, *, keys_on_rows):
    h = pl.program_id(0)
    row = lax.broadcasted_iota(jnp.int32, (QBLK, QBLK), 0)
    col = lax.broadcasted_iota(jnp.int32, (QBLK, QBLK), 1)
    t_minus_s = col - row if keys_on_rows else row - col
    for r in range(3):
        bucket = _t5_bucket(r * QBLK + t_minus_s)
        tile = jnp.zeros((QBLK, QBLK), F32)
        for b in range(REL_BUCKETS):
            tile = jnp.where(bucket == b, tab_ref[h, b], tile)
        if keys_on_rows:
            o_ref[r] = tile
        else:
            o_ref[r, 0] = tile


def bias_tiles(rel_table, *, keys_on_rows):
    n_heads = rel_table.shape[1]
    if keys_on_rows:
        out_spec = pl.BlockSpec((3, QBLK, QBLK), lambda h: (0, 0, h))
        out_shape = (3, QBLK, n_heads * QBLK)
    else:
        out_spec = pl.BlockSpec((3, 1, QBLK, QBLK), lambda h: (0, h, 0, 0))
        out_shape = (3, n_heads, QBLK, QBLK)
    return pl.pallas_call(
        functools.partial(_bias_tile_kernel, keys_on_rows=keys_on_rows),
        grid=(n_heads,),
        in_specs=[pl.BlockSpec(memory_space=pltpu.SMEM)],
        out_specs=out_spec,
        out_shape=jax.ShapeDtypeStruct(out_shape, F32),
        compiler_params=_cparams(1),
        name="bias_tiles",
    )(rel_table.T.astype(F32))


def _rms(x, g):
    return x * lax.rsqrt(jnp.mean(x * x, axis=-1, keepdims=True) + RMS_EPS) * g


def _modulate(x, g_ref, mod_ref, gi, shift_i, scale_i):
    y = _rms(x, g_ref[gi:gi + 1, :])
    return y * (1.0 + mod_ref[0, scale_i:scale_i + 1, :]) + mod_ref[0, shift_i:shift_i + 1, :]


def _normmod_kernel(x_ref, g_ref, mod_ref, h_ref, *, gi, shift_i, scale_i):
    h_ref[...] = _modulate(x_ref[...], g_ref, mod_ref, gi, shift_i, scale_i).astype(h_ref.dtype)


def normmod(x, gains, mod, seq, *, gi, shift_i, scale_i, tm=256):
    m, d = x.shape
    tpb = seq // tm
    return pl.pallas_call(
        functools.partial(_normmod_kernel, gi=gi, shift_i=shift_i, scale_i=scale_i),
        grid=(m // tm,),
        in_specs=[pl.BlockSpec((tm, d), lambda i: (i, 0)),
                  pl.BlockSpec(gains.shape, lambda i: (0, 0)),
                  pl.BlockSpec((1, N_MOD, d), lambda i: (i // tpb, 0, 0))],
        out_specs=pl.BlockSpec((tm, d), lambda i: (i, 0)),
        out_shape=jax.ShapeDtypeStruct((m, d), BF16),
        compiler_params=_cparams(1),
        name="normmod",
    )(x, gains, mod)


def _resid_kernel(x_ref, y_ref, g_ref, mod_ref, x_out_ref, *, gi, gate_i, coef):
    y = _rms(y_ref[...].astype(F32), g_ref[gi:gi + 1, :])
    x_out_ref[...] = x_ref[...] + coef * mod_ref[0, gate_i:gate_i + 1, :] * y


def resid(x, y, gains, mod, seq, *, gi, gate_i, coef, tm=256):
    m, d = x.shape
    tpb = seq // tm
    row = pl.BlockSpec((tm, d), lambda i: (i, 0))
    return pl.pallas_call(
        functools.partial(_resid_kernel, gi=gi, gate_i=gate_i, coef=coef),
        grid=(m // tm,),
        in_specs=[row, row,
                  pl.BlockSpec(gains.shape, lambda i: (0, 0)),
                  pl.BlockSpec((1, N_MOD, d), lambda i: (i // tpb, 0, 0))],
        out_specs=row,
        out_shape=jax.ShapeDtypeStruct((m, d), F32),
        compiler_params=_cparams(1),
        name="resid",
    )(x, y, gains, mod)


def _resid_normmod_kernel(x_ref, y_ref, g_ref, mod_ref, g2_ref, mod2_ref, x_out_ref, h_ref, *,
                          gi, gate_i, coef, gi2, shift_i, scale_i):
    y = _rms(y_ref[...].astype(F32), g_ref[gi:gi + 1, :])
    x_new = x_ref[...] + coef * mod_ref[0, gate_i:gate_i + 1, :] * y
    x_out_ref[...] = x_new
    h_ref[...] = _modulate(x_new, g2_ref, mod2_ref, gi2, shift_i, scale_i).astype(h_ref.dtype)


def resid_normmod(x, y, gains, mod, gains2, mod2, seq, *, gi, gate_i, coef, gi2, shift_i, scale_i, tm=256):
    m, d = x.shape
    tpb = seq // tm
    row = pl.BlockSpec((tm, d), lambda i: (i, 0))
    gspec = pl.BlockSpec(gains.shape, lambda i: (0, 0))
    mspec = pl.BlockSpec((1, N_MOD, d), lambda i: (i // tpb, 0, 0))
    return pl.pallas_call(
        functools.partial(_resid_normmod_kernel, gi=gi, gate_i=gate_i, coef=coef, gi2=gi2,
                          shift_i=shift_i, scale_i=scale_i),
        grid=(m // tm,),
        in_specs=[row, row, gspec, mspec, gspec, mspec],
        out_specs=[row, row],
        out_shape=[jax.ShapeDtypeStruct((m, d), F32), jax.ShapeDtypeStruct((m, d), BF16)],
        compiler_params=_cparams(1),
        name="resid_normmod",
    )(x, y, gains, mod, gains2, mod2)


def _mm_kernel(a_ref, w_ref, o_ref, *, act):
    acc = _dot(a_ref[...], w_ref[...])
    if act == "silu":
        acc = _silu(acc)
    o_ref[...] = acc.astype(o_ref.dtype)


def _layer_spec(tail_block, tail_index, layer):
    if layer is None:
        return pl.BlockSpec(tail_block, tail_index)
    return pl.BlockSpec((None,) + tail_block, lambda *g: (layer,) + tail_index(*g))


def _mxu_call(body, grid, in_specs, out_spec, out_shape, args, name, cast):
    if cast is None:
        return pl.pallas_call(body, grid=grid, in_specs=in_specs, out_specs=out_spec, out_shape=out_shape,
                              compiler_params=_cparams(2), name=name)(*args)
    src, rows_per_layer, layer = cast
    steps = grid[0] * grid[1]
    rows = rows_per_layer // steps
    assert rows * steps == rows_per_layer and rows % 16 == 0
    cols = src.shape[1]
    n_in = len(in_specs)

    def body_and_cast(*refs):
        refs[-1][...] = refs[n_in][...].astype(BF16)
        body(*refs[:n_in], refs[n_in + 1])

    return pl.pallas_call(
        body_and_cast, grid=grid,
        in_specs=list(in_specs) + [pl.BlockSpec((rows, cols), lambda i, j: (layer * steps + i * grid[1] + j, 0))],
        out_specs=[out_spec, pl.BlockSpec((rows, cols), lambda i, j: (i * grid[1] + j, 0))],
        out_shape=[out_shape, jax.ShapeDtypeStruct((rows_per_layer, cols), BF16)],
        compiler_params=_cparams(2), name=name)(*args, src)


def matmul(a, w, *, out_dtype, layer=None, act=None, cast=None, tm=1024, tn=512):
    m, k = a.shape
    n = w.shape[-1]
    tm, tn = min(tm, m), min(tn, n)
    assert m % tm == 0 and n % tn == 0
    return _mxu_call(
        functools.partial(_mm_kernel, act=act), (m // tm, n // tn),
        [pl.BlockSpec((tm, k), lambda i, j: (i, 0)),
         _layer_spec((k, tn), lambda i, j: (0, j), layer)],
        pl.BlockSpec((tm, tn), lambda i, j: (i, j)),
        jax.ShapeDtypeStruct((m, n), out_dtype), (a, w), "matmul", cast)


def _swiglu_kernel(a_ref, wu_ref, wv_ref, o_ref):
    a = a_ref[...]
    u = _dot(a, wu_ref[...])
    v = _dot(a, wv_ref[...])
    o_ref[...] = (_silu(u) * v).astype(o_ref.dtype)


def swiglu_in(a, w, *, cast=None, tm=1024, tn=512):
    m, k = a.shape
    f = w.shape[-1] // 2
    tm = min(tm, m)
    nf = f // tn
    return _mxu_call(
        _swiglu_kernel, (m // tm, nf),
        [pl.BlockSpec((tm, k), lambda i, j: (i, 0)),
         pl.BlockSpec((k, tn), lambda i, j: (0, j)),
         pl.BlockSpec((k, tn), lambda i, j: (0, j + nf))],
        pl.BlockSpec((tm, tn), lambda i, j: (i, j)),
        jax.ShapeDtypeStruct((m, f), BF16), (a, w, w), "swiglu_in", cast)


def _merge_kernel(h_ref, wga_ref, wgb_ref, wgc_ref, ya_ref, yb_ref, yc_ref, wb_ref, o_ref):
    h = h_ref[...]
    acc = jax.nn.sigmoid(_dot(h, wga_ref[...])) * _dot(ya_ref[...], wb_ref[0])
    acc = acc + jax.nn.sigmoid(_dot(h, wgb_ref[...])) * _dot(yb_ref[...], wb_ref[1])
    acc = acc + jax.nn.sigmoid(_dot(h, wgc_ref[...])) * _dot(yc_ref[...], wb_ref[2])
    o_ref[...] = acc.astype(o_ref.dtype)


def merge(h, w_gates, ya, yb, yc, w_branch, layer, *, cast=None, tm=1024, tn=256):
    m, k = h.shape
    kw = ya.shape[1]
    d = w_branch.shape[-1]
    tm = min(tm, m)
    nd = d // tn
    y_spec = pl.BlockSpec((tm, kw), lambda i, j: (i, 0))
    return _mxu_call(
        _merge_kernel, (m // tm, nd),
        [pl.BlockSpec((tm, k), lambda i, j: (i, 0)),
         _layer_spec((k, tn), lambda i, j: (0, j), layer),
         _layer_spec((k, tn), lambda i, j: (0, j + nd), layer),
         _layer_spec((k, tn), lambda i, j: (0, j + 2 * nd), layer),
         y_spec, y_spec, y_spec,
         _layer_spec((N_BRANCH, kw, tn), lambda i, j: (0, 0, j), layer)],
        pl.BlockSpec((tm, tn), lambda i, j: (i, j)),
        jax.ShapeDtypeStruct((m, d), BF16),
        (h, w_gates, w_gates, w_gates, ya, yb, yc, w_branch), "merge", cast)


def _kvup_kernel(lat_ref, g_ref, w_ref, k_ref, vt_ref):
    h = _rms(lat_ref[...], g_ref[...]).astype(BF16)
    kv = _dot(h, w_ref[...])
    k_ref[...] = kv[:, :B_DH].astype(k_ref.dtype)
    v = kv[:, B_DH:]
    for u in range(vt_ref.shape[0]):
        vt_ref[u] = v[u * QBLK:(u + 1) * QBLK].T.astype(vt_ref.dtype)


def kv_up(proj, kv_norm_g, w_kv_up, layer, *, tm=512):
    m = proj.shape[0]
    return pl.pallas_call(
        _kvup_kernel,
        grid=(m // tm,),
        in_specs=[pl.BlockSpec((tm, B_KV_RANK), lambda i: (i, PROJ_BLAT // B_KV_RANK)),
                  pl.BlockSpec((1, B_KV_RANK), lambda i: (0, 0)),
                  _layer_spec((B_KV_RANK, 2 * B_DH), lambda i: (0, 0), layer)],
        out_specs=[pl.BlockSpec((tm, B_DH), lambda i: (i, 0)),
                   pl.BlockSpec((tm // QBLK, B_DH, QBLK), lambda i: (i, 0, 0))],
        out_shape=[jax.ShapeDtypeStruct((m, B_DH), BF16),
                   jax.ShapeDtypeStruct((m // QBLK, B_DH, QBLK), BF16)],
        compiler_params=_cparams(1),
        name="kv_up",
    )(proj, kv_norm_g.reshape(1, B_KV_RANK), w_kv_up)


def _hgrn_chunk(q, fr, ir, gr, lb, ng, st, tril):
    cs, sub = HGRN_CHUNK, HGRN_SUB
    f = lb + (1.0 - lb) * jax.nn.sigmoid(fr)
    lf = jnp.log(jnp.maximum(f, 1e-20))
    k = 1.0 - f
    v = _silu(ir)
    b = jnp.dot(tril, lf, precision=lax.Precision.HIGHEST, preferred_element_type=F32) * LOG2_E
    o_inter = _dot_nt((q * jnp.exp2(b)).astype(BF16), st.astype(BF16))

    col_s = lax.broadcasted_iota(jnp.int32, (sub, cs), 1)
    blocks = []
    for si in range(cs // sub):
        lo = si * sub
        q_s = q[lo:lo + sub]
        b_s = b[lo:lo + sub]
        if si > 0:
            bref = b[lo - 1:lo]
            qt = (q_s * jnp.exp2(b_s - bref)).astype(BF16)
            kj = (k[:lo] * jnp.exp2(bref - b[:lo])).astype(BF16)
            attn = jnp.concatenate([_dot_nt(qt, kj), jnp.zeros((sub, cs - lo), F32)], axis=1)
        else:
            attn = jnp.zeros((sub, cs), F32)
        for s in range(sub):
            a = q_s * k[lo + s:lo + s + 1] * jnp.exp2(b_s - b[lo + s:lo + s + 1])
            attn = jnp.where(col_s == lo + s, jnp.sum(a, axis=-1, keepdims=True), attn)
        blocks.append(attn)
    attn = jnp.where(tril > 0.0, jnp.concatenate(blocks, axis=0), 0.0)
    o = o_inter + _dot(attn.astype(BF16), v.astype(BF16))

    b_end = b[cs - 1:cs]
    kd = (k * jnp.exp2(b_end - b)).astype(BF16)
    upd = _dot(v.T.astype(BF16), kd)
    st_new = st * jnp.exp2(b_end) + upd
    return _rms(o, ng) * _silu(gr), st_new


def _hgrn_kernel(q_ref, f_ref, i_ref, g_ref, lb_ref, ng_ref, o_ref, st_ref, *, seq):
    cs = HGRN_CHUNK
    st_ref[...] = jnp.zeros_like(st_ref)
    ng = ng_ref[...]
    row_c = lax.broadcasted_iota(jnp.int32, (cs, cs), 0)
    col_c = lax.broadcasted_iota(jnp.int32, (cs, cs), 1)
    tril = (col_c <= row_c).astype(F32)

    def chunk(c, carry):
        r0 = pl.multiple_of(c * cs, cs)
        for h in range(HGRN_HEADS_PER_STEP):
            cols = slice(h * A_DK, (h + 1) * A_DK)
            o, st_new = _hgrn_chunk(q_ref[pl.ds(r0, cs), cols], f_ref[pl.ds(r0, cs), cols],
                                    i_ref[pl.ds(r0, cs), cols], g_ref[pl.ds(r0, cs), cols],
                                    lb_ref[:, cols], ng, st_ref[h], tril)
            st_ref[h] = st_new
            o_ref[pl.ds(r0, cs), cols] = o.astype(o_ref.dtype)
        return carry

    lax.fori_loop(0, seq // cs, chunk, 0, unroll=2)


def hgrn2(proj, lb, norm_g, seq):
    m = proj.shape[0]
    bsz = m // seq
    nh = HGRN_HEADS_PER_STEP
    steps = A_HEADS // nh

    def col(part):
        return pl.BlockSpec((seq, nh * A_DK), lambda b, h: (b, part * steps + h))

    return pl.pallas_call(
        functools.partial(_hgrn_kernel, seq=seq),
        grid=(bsz, steps),
        in_specs=[col(0), col(1), col(2), col(3),
                  pl.BlockSpec((1, nh * A_DK), lambda b, h: (0, h)),
                  pl.BlockSpec((1, A_DV), lambda b, h: (0, 0))],
        out_specs=pl.BlockSpec((seq, nh * A_DV), lambda b, h: (b, h)),
        out_shape=jax.ShapeDtypeStruct((m, A_HEADS * A_DV), BF16),
        scratch_shapes=[pltpu.VMEM((nh, A_DV, A_DK), F32)],
        compiler_params=_cparams(2),
        name="hgrn2",
    )(proj, proj, proj, proj, lb.reshape(1, A_WIDTH), norm_g.reshape(1, A_DV))


def _dsa_kernel(q_ref, iq_ref, smq_ref, smk_ref, k_ref, vt_ref, bt_ref, o_ref,
                key_ref, lg_ref, mx_ref, ls_ref, acc_ref, *, nblk, topk):
    j = pl.program_id(1)
    row = lax.broadcasted_iota(jnp.int32, (QBLK, QBLK), 0)
    col = lax.broadcasted_iota(jnp.int32, (QBLK, QBLK), 1)
    causal = row <= col
    n_grp = j // TOPK_GROUP + 1
    n_pair = j // 2 + 1
    int_min = jnp.int32(-2 ** 31)

    def sort_key(score):
        bits = lax.bitcast_convert_type(score, jnp.int32)
        return bits ^ ((bits >> 31) & jnp.int32(0x7FFFFFFF))

    iq = iq_ref[...].astype(BF16)
    iqs = jnp.concatenate([iq[:, h * IDX_DH:(h + 1) * IDX_DH] for h in range(IDX_HEADS)], axis=0)
    iwt = (smq_ref[:, IDX_DH:IDX_DH + IDX_HEADS] * (IDX_HEADS ** -0.5 * IDX_DH ** -0.5)).T

    def visible(kb):
        return jnp.logical_or(kb < j, jnp.logical_and(kb == j, causal))

    def score_tile(kb):
        r0 = pl.multiple_of(kb * QBLK, QBLK)
        ik = smk_ref[pl.ds(r0, QBLK), 0:IDX_DH].astype(BF16)
        d = _dot_nt(ik, iqs)
        sc = jnp.zeros((QBLK, QBLK), F32)
        for h in range(IDX_HEADS):
            sc = sc + jnp.maximum(d[:, h * QBLK:(h + 1) * QBLK], 0.0) * iwt[h:h + 1, :]
        key_ref[kb] = sort_key(jnp.where(visible(kb), sc, NEG_INF))

    def pairwise(tile_fn):
        def step(kp, carry):
            tile_fn(2 * kp)
            tile_fn(2 * kp + 1)
            return carry
        lax.fori_loop(0, n_pair, step, 0)

    def groupwise(tile_fn):
        def step(gi, carry):
            for u in range(TOPK_GROUP):
                tile_fn(gi * TOPK_GROUP + u)
            return carry
        lax.fori_loop(0, n_grp, step, 0)

    pairwise(score_tile)

    def fill_step(kb, carry):
        key_ref[kb] = sort_key(jnp.full((QBLK, QBLK), NEG_INF, F32))
        return carry

    lax.fori_loop(2 * n_pair, n_grp * TOPK_GROUP, fill_step, 0)

    kf = jnp.float32(topk)

    def count(pred):
        def pair(kp, c):
            for u in range(2):
                kb = 2 * kp + u
                hit = jnp.where(pred(kb, key_ref[kb]), 1.0, 0.0)
                c = c + jnp.sum(hit.reshape(4, QBLK // 4, QBLK), axis=0)
            return c
        c = lax.fori_loop(0, n_pair, pair, jnp.zeros((QBLK // 4, QBLK), F32))
        return jnp.sum(c, axis=0, keepdims=True)

    def thr_step(it, cand):
        trial = cand | lax.shift_left(jnp.int32(1), 31 - it)
        t_s = trial ^ int_min
        cnt = count(lambda kb, key: key >= t_s)
        return jnp.where(cnt >= kf, trial, cand)

    def search():
        return lax.fori_loop(0, 32, thr_step, jnp.zeros((1, QBLK), jnp.int32)) ^ int_min

    thr = lax.cond((j + 1) * QBLK <= topk, lambda: jnp.full((1, QBLK), int_min, jnp.int32), search)
    n_gt = count(lambda kb, key: key > thr)
    n_ge = count(lambda kb, key: key >= thr)
    need = kf - n_gt
    nbits = (nblk * QBLK - 1).bit_length()

    def tie_search():
        def idx_step(it, pos):
            trial = pos + lax.shift_left(jnp.int32(1), nbits - 1 - it)
            cnt = count(lambda kb, key: jnp.logical_and(key == thr, kb * QBLK + row < trial))
            return jnp.where(cnt < need, trial, pos)
        return lax.fori_loop(0, nbits, idx_step, jnp.zeros((1, QBLK), jnp.int32))

    has_ties = jnp.max(n_ge) > kf
    pos = lax.cond(has_ties, tie_search, lambda: jnp.full((1, QBLK), nblk * QBLK, jnp.int32))

    q = q_ref[...].astype(BF16)
    qs = jnp.concatenate([q[:, h * B_DH:(h + 1) * B_DH] for h in range(B_HEADS)], axis=0)
    scale = B_DH ** -0.5
    mx_ref[...] = jnp.full(mx_ref.shape, NEG_INF, F32)

    def logit_tile(kb):
        r0 = pl.multiple_of(kb * QBLK, QBLK)
        lg = _dot_nt(k_ref[pl.ds(r0, QBLK), :], qs)
        key = key_ref[kb]
        sel = jnp.logical_or(key > thr, jnp.logical_and(key == thr, kb * QBLK + row <= pos))
        msk = jnp.where(jnp.logical_and(sel, visible(kb)), 0.0, NEG_INF)
        lg = lg * scale + bt_ref[jnp.clip(j - kb, 0, 2)] + jnp.concatenate([msk] * B_HEADS, axis=1)
        lg_ref[kb] = lg
        mx_ref[...] = jnp.maximum(mx_ref[...], lg)

    groupwise(logit_tile)
    mx = jnp.max(mx_ref[...], axis=0, keepdims=True)
    ls_ref[...] = jnp.zeros_like(ls_ref)
    acc_ref[...] = jnp.zeros_like(acc_ref)

    def pv_tile(kb):
        p = jnp.exp(lg_ref[kb] - mx)
        ls_ref[...] += p
        acc_ref[...] += _dot(vt_ref[kb], p.astype(BF16))

    groupwise(pv_tile)
    out_t = acc_ref[...] / jnp.sum(ls_ref[...], axis=0, keepdims=True)
    for h in range(B_HEADS):
        o_ref[:, h * B_DH:(h + 1) * B_DH] = out_t[:, h * QBLK:(h + 1) * QBLK].T.astype(o_ref.dtype)


def dsa(proj, k, vt, bias_b, seq):
    m = proj.shape[0]
    bsz = m // seq
    nblk = seq // QBLK
    assert nblk % TOPK_GROUP == 0
    topk = min(TOPK_MAX, seq // 4)
    qw = B_HEADS * B_DH
    small_blk = PROJ_SMALL // B_SMALL_W
    cols = B_HEADS * QBLK
    return pl.pallas_call(
        functools.partial(_dsa_kernel, nblk=nblk, topk=topk),
        grid=(bsz, nblk),
        in_specs=[pl.BlockSpec((QBLK, qw), lambda b, j: (b * nblk + j, PROJ_BQ // qw)),
                  pl.BlockSpec((QBLK, qw), lambda b, j: (b * nblk + j, PROJ_BIQ // qw)),
                  pl.BlockSpec((QBLK, B_SMALL_W), lambda b, j: (b * nblk + j, small_blk)),
                  pl.BlockSpec((seq, B_SMALL_W), lambda b, j: (b, small_blk)),
                  pl.BlockSpec((seq, B_DH), lambda b, j: (b, 0)),
                  pl.BlockSpec((nblk, B_DH, QBLK), lambda b, j: (b, 0, 0)),
                  pl.BlockSpec((3, QBLK, cols), lambda b, j: (0, 0, 0))],
        out_specs=pl.BlockSpec((QBLK, qw), lambda b, j: (b * nblk + j, 0)),
        out_shape=jax.ShapeDtypeStruct((m, qw), BF16),
        scratch_shapes=[pltpu.VMEM((nblk, QBLK, QBLK), jnp.int32),
                        pltpu.VMEM((nblk, QBLK, cols), F32),
                        pltpu.VMEM((QBLK, cols), F32),
                        pltpu.VMEM((QBLK, cols), F32),
                        pltpu.VMEM((B_DH, cols), F32)],
        compiler_params=_cparams(2),
        name="dsa",
    )(proj, proj, proj, proj, k, vt, bias_b)


def _swa_kernel(sink_ref, q_ref, kc_ref, vc_ref, kp_ref, vp_ref, bt_ref, o_ref):
    j = pl.program_id(1)
    grp = C_HEADS // C_KV_HEADS
    row = lax.broadcasted_iota(jnp.int32, (QBLK, QBLK), 0)
    col = lax.broadcasted_iota(jnp.int32, (QBLK, QBLK), 1)
    cur_ok = jnp.concatenate([row <= col] * grp, axis=1)
    prev_ok = jnp.concatenate([jnp.logical_and(row > col, j > 0)] * grp, axis=1)
    q = q_ref[...].astype(BF16)
    scale = C_DH ** -0.5
    for g in range(C_KV_HEADS):
        heads = range(g * grp, (g + 1) * grp)
        kv_cols = slice(g * C_DH, (g + 1) * C_DH)
        lanes = slice(g * grp * QBLK, (g + 1) * grp * QBLK)
        qs = jnp.concatenate([q[:, h * C_DH:(h + 1) * C_DH] for h in heads], axis=0)
        sink = jnp.concatenate([jnp.full((1, QBLK), sink_ref[h], F32) for h in heads], axis=1)
        lc = _dot_nt(kc_ref[:, kv_cols].astype(BF16), qs)
        lp = _dot_nt(kp_ref[:, kv_cols].astype(BF16), qs)
        lc = jnp.where(cur_ok, lc * scale + bt_ref[0, :, lanes], NEG_INF)
        lp = jnp.where(prev_ok, lp * scale + bt_ref[1, :, lanes], NEG_INF)
        mx = jnp.maximum(jnp.maximum(jnp.max(lc, axis=0, keepdims=True),
                                     jnp.max(lp, axis=0, keepdims=True)), sink)
        pc = jnp.exp(lc - mx)
        pp = jnp.exp(lp - mx)
        den = (jnp.sum(pc, axis=0, keepdims=True) + jnp.sum(pp, axis=0, keepdims=True)
               + jnp.exp(sink - mx))
        o_t = (_dot(vc_ref[:, kv_cols].T.astype(BF16), pc.astype(BF16))
               + _dot(vp_ref[:, kv_cols].T.astype(BF16), pp.astype(BF16))) / den
        for u in range(0, grp, 2):
            pair = jnp.concatenate([o_t[:, u * QBLK:(u + 1) * QBLK],
                                    o_t[:, (u + 1) * QBLK:(u + 2) * QBLK]], axis=0)
            h0 = g * grp + u
            o_ref[:, h0 * C_DH:(h0 + 2) * C_DH] = pair.T.astype(o_ref.dtype)


def swa(proj, sinks, bias_c, seq):
    m = proj.shape[0]
    bsz = m // seq
    nblk = seq // QBLK
    qw = C_HEADS * C_DH
    kvw = C_KV_HEADS * C_DH

    def cur(col):
        return pl.BlockSpec((QBLK, kvw), lambda b, j: (b * nblk + j, col // kvw))

    def prev(col):
        return pl.BlockSpec((QBLK, kvw), lambda b, j: (b * nblk + jnp.maximum(j - 1, 0), col // kvw))

    return pl.pallas_call(
        _swa_kernel,
        grid=(bsz, nblk),
        in_specs=[pl.BlockSpec(memory_space=pltpu.SMEM),
                  pl.BlockSpec((QBLK, qw), lambda b, j: (b * nblk + j, PROJ_CQ // qw)),
                  cur(PROJ_CK), cur(PROJ_CV), prev(PROJ_CK), prev(PROJ_CV),
                  pl.BlockSpec((3, QBLK, C_HEADS * QBLK), lambda b, j: (0, 0, 0))],
        out_specs=pl.BlockSpec((QBLK, qw), lambda b, j: (b * nblk + j, 0)),
        out_shape=jax.ShapeDtypeStruct((m, qw), BF16),
        compiler_params=_cparams(2),
        name="swa",
    )(sinks.astype(F32), proj, proj, proj, proj, proj, bias_c)


def _pack_w_in(w):
    o = 0
    parts = {}
    for name, width in (("a", 4 * A_WIDTH), ("bq", B_HEADS * B_DH), ("blat", B_KV_RANK),
                        ("biq", IDX_HEADS * IDX_DH), ("bik", IDX_DH), ("biw", IDX_HEADS),
                        ("cq", C_HEADS * C_DH), ("ck", C_KV_HEADS * C_DH), ("cv", C_KV_HEADS * C_DH)):
        parts[name] = w[..., o:o + width]
        o += width
    gates = w[..., o:]
    def zeros(width):
        return jnp.zeros(w.shape[:-1] + (width,), w.dtype)
    proj = jnp.concatenate([parts[n] for n in ("a", "bq", "biq", "cq", "blat", "bik", "biw")]
                           + [zeros(B_SMALL_W - IDX_DH - IDX_HEADS), parts["ck"], parts["cv"]], axis=-1)
    proj = jnp.concatenate([proj, zeros(PROJ_W - proj.shape[-1])], axis=-1)
    return proj.astype(BF16), gates.astype(BF16)


def kernel(x, c, w_c_down, w_c_up, norm_gains, w_in, lb_logits, hgrn_norm, kv_norm, w_kv_up, rel_table,
           sinks, w_branch, w_out, ffn1_in, ffn1_out, ffn2_in, ffn2_out):
    bsz, seq, d = x.shape
    depth = w_in.shape[0]
    m = bsz * seq
    x = x.reshape(m, d).astype(F32)

    cond = matmul(c.astype(BF16), w_c_down.astype(BF16), out_dtype=BF16, act="silu")
    lbs = lower_bounds(lb_logits)
    bias_b = bias_tiles(rel_table[:, :B_HEADS], keys_on_rows=True)
    bias_c = bias_tiles(rel_table[:, B_HEADS:], keys_on_rows=True)
    w_c_up, w_kv_up, w_branch = w_c_up.astype(BF16), w_kv_up.astype(BF16), w_branch.astype(BF16)
    wp, wg = _pack_w_in(w_in)
    mods = [matmul(cond, w_c_up, layer=l, out_dtype=F32, tn=4096).reshape(bsz, N_MOD, d)
            for l in range(depth)]
    gains = norm_gains.astype(F32)

    def rows(w):
        return w.reshape(-1, w.shape[-1])
    k_ff, k_out = ffn1_out.shape[1], w_out.shape[1]
    src_ffn1_in, src_ffn1_out, src_ffn2_in, src_ffn2_out, src_out = (
        rows(t) for t in (ffn1_in, ffn1_out, ffn2_in, ffn2_out, w_out))

    w1_in = ffn1_in[0].astype(BF16)
    h = normmod(x, gains[0], mods[0], seq, gi=0, shift_i=0, scale_i=1)
    for l in range(depth):
        mod, g = mods[l], gains[l]

        u, w1_out = swiglu_in(h, w1_in, cast=(src_ffn1_out, k_ff, l))
        y, wo = matmul(u, w1_out, out_dtype=BF16, tn=1024, cast=(src_out, k_out, l))
        x, h = resid_normmod(x, y, g, mod, g, mod, seq, gi=1, gate_i=2, coef=FFN_RES,
                             gi2=2, shift_i=3, scale_i=4)

        proj = matmul(h, wp, layer=l, out_dtype=F32, tn=1024)
        ya = hgrn2(proj, lbs[l], hgrn_norm[l].astype(F32), seq)
        kb, vb = kv_up(proj, kv_norm[l].astype(F32), w_kv_up, l)
        yb = dsa(proj, kb, vb, bias_b, seq)
        yc = swa(proj, sinks[l], bias_c, seq)
        mix, w2_in = merge(h, wg, ya, yb, yc, w_branch, l, cast=(src_ffn2_in, d, l))
        y, w2_out = matmul(mix, wo, out_dtype=BF16, tn=1024, cast=(src_ffn2_out, k_ff, l))
        x, h = resid_normmod(x, y, g, mod, g, mod, seq, gi=3, gate_i=5, coef=1.0,
                             gi2=4, shift_i=6, scale_i=7)

        if l + 1 < depth:
            u, w1_in = swiglu_in(h, w2_in, cast=(src_ffn1_in, d, l + 1))
            y = matmul(u, w2_out, out_dtype=BF16, tn=1024)
            x, h = resid_normmod(x, y, g, mod, gains[l + 1], mods[l + 1], seq, gi=5, gate_i=8,
                                 coef=FFN_RES, gi2=0, shift_i=0, scale_i=1)
        else:
            u = swiglu_in(h, w2_in)
            y = matmul(u, w2_out, out_dtype=BF16, tn=1024)
            x = resid(x, y, g, mod, seq, gi=5, gate_i=8, coef=FFN_RES)
    return x.reshape(bsz, seq, d)
```
